```python
import math
import jax
import jax.numpy as jnp
from jax import lax
import numpy as np


D_MODEL = 2048
BATCH = 2
SEQ = 4096
DEPTH = 4

HEAD_DIM = 128
N_MIXERS = 4
HEADS_PER_MIXER = D_MODEL // HEAD_DIM // N_MIXERS
GROUP_WIDTH = HEADS_PER_MIXER * HEAD_DIM
MIX_WIDTH = N_MIXERS * GROUP_WIDTH
Q_BLOCK = 128
ROPE_THETA = 500000.0
ROPE_FRACTION = 4
NSA_KV_WIDTH = HEAD_DIM
NSA_CMP_BLOCK = 32
NSA_CMP_STRIDE = 16
NSA_CMP_HIDDEN = 256
NSA_SLC_BLOCK = 64
NSA_TOPK = 16
NSA_LOCAL_BLOCKS = 2
NSA_WINDOW = 512
DIFF_QK_DIM = HEAD_DIM // 2
D_FF = 5632
CONV_WIDTH = 3
EPS = 1e-6
NEG_INF = -1e30
FORCE_SCORE = 1e9

IN_SPLITS = (
    GROUP_WIDTH,
    NSA_KV_WIDTH, NSA_KV_WIDTH,
    NSA_KV_WIDTH, NSA_KV_WIDTH,
    NSA_KV_WIDTH, NSA_KV_WIDTH,
    3 * HEADS_PER_MIXER,
    GROUP_WIDTH, GROUP_WIDTH, GROUP_WIDTH,
    GROUP_WIDTH, GROUP_WIDTH, GROUP_WIDTH, HEADS_PER_MIXER,
    GROUP_WIDTH, GROUP_WIDTH, GROUP_WIDTH,
)
IN_COLS = sum(IN_SPLITS)

kernel_name = 'hybrid_parallel_heads_nsa_sb_fox_diff'


def _rms_norm(x, g):
    xf = x.astype(jnp.float32)
    y = xf * lax.rsqrt(jnp.mean(xf * xf, axis=-1, keepdims=True) + EPS)
    return (y * g.astype(jnp.float32)).astype(x.dtype)


def _rope_tables(seq, rot_dim):
    pos = jnp.arange(seq, dtype=jnp.float32)
    inv = ROPE_THETA ** (-jnp.arange(0, rot_dim, 2, dtype=jnp.float32) / rot_dim)
    ang = pos[:, None] * inv[None, :]
    return jnp.cos(ang), jnp.sin(ang)


def _partial_rope(x, cos, sin):
    half = cos.shape[-1]
    rot = 2 * half
    xf = x.astype(jnp.float32)
    x1, x2, rest = xf[..., :half], xf[..., half:rot], xf[..., rot:]
    out = jnp.concatenate([x1 * cos - x2 * sin, x2 * cos + x1 * sin, rest], axis=-1)
    return out.astype(x.dtype)


def _heads(t, n_heads):
    b, s, _ = t.shape
    return t.reshape(b, s, n_heads, -1).transpose(0, 2, 1, 3)


def _merge_heads(o):
    b, h, s, d = o.shape
    return o.transpose(0, 2, 1, 3).reshape(b, s, h * d)


def _sweep(body, n_blocks):
    out = lax.map(body, jnp.arange(n_blocks))
    nb, b, h, qb, d = out.shape
    return out.transpose(1, 2, 0, 3, 4).reshape(b, h, nb * qb, d)


def _stick_breaking_attention(q, k, v):
    b, h, s_len, d = q.shape
    scale = d ** -0.5
    kpos = jnp.arange(s_len)

    def body(i):
        q0 = i * Q_BLOCK
        qb = lax.dynamic_slice_in_dim(q, q0, Q_BLOCK, axis=2)
        qpos = q0 + jnp.arange(Q_BLOCK)
        past = kpos[None, :] < qpos[:, None]
        z = jnp.einsum('bhqd,bhkd->bhqk', qb, k).astype(jnp.float32) * scale
        log_keep = jnp.where(past, jax.nn.log_sigmoid(-z), 0.0)
        after = lax.cumsum(log_keep, axis=3, reverse=True) - log_keep
        w = jnp.where(past, jnp.exp(jax.nn.log_sigmoid(z) + after), 0.0)
        return jnp.einsum('bhqk,bhkd->bhqd', w.astype(v.dtype), v)

    return _sweep(body, s_len // Q_BLOCK)


def _forgetting_attention(q, k, v, log_f):
    b, h, s_len, d = q.shape
    scale = d ** -0.5
    cum = jnp.cumsum(log_f, axis=-1)
    kpos = jnp.arange(s_len)

    def body(i):
        q0 = i * Q_BLOCK
        qb = lax.dynamic_slice_in_dim(q, q0, Q_BLOCK, axis=2)
        cq = lax.dynamic_slice_in_dim(cum, q0, Q_BLOCK, axis=2)
        qpos = q0 + jnp.arange(Q_BLOCK)
        causal = kpos[None, :] <= qpos[:, None]
        s = jnp.einsum('bhqd,bhkd->bhqk', qb, k).astype(jnp.float32) * scale
        s = s + cq[..., :, None] - cum[..., None, :]
        p = jax.nn.softmax(jnp.where(causal, s, NEG_INF), axis=-1)
        return jnp.einsum('bhqk,bhkd->bhqd', p.astype(v.dtype), v)

    return _sweep(body, s_len // Q_BLOCK)


def _differential_attention(q1, q2, k1, k2, v, lam):
    b, h, s_len, dq = q1.shape
    scale = dq ** -0.5
    kpos = jnp.arange(s_len)

    def body(i):
        q0 = i * Q_BLOCK
        qpos = q0 + jnp.arange(Q_BLOCK)
        causal = kpos[None, :] <= qpos[:, None]

        def probs(q, k):
            qb = lax.dynamic_slice_in_dim(q, q0, Q_BLOCK, axis=2)
            s = jnp.einsum('bhqd,bhkd->bhqk', qb, k).astype(jnp.float32) * scale
            return jax.nn.softmax(jnp.where(causal, s, NEG_INF), axis=-1)

        w = probs(q1, k1) - lam * probs(q2, k2)
        return jnp.einsum('bhqk,bhkd->bhqd', w.astype(v.dtype), v)

    return _sweep(body, s_len // Q_BLOCK)


def _nsa_attention(q, q_rot, k_cmp, v_cmp, k_slc, v_slc, k_win, v_win, gates,
                   pe_k, pe_v, cmp_k1, cmp_k2, cmp_v1, cmp_v2):
    b, h, s_len, d = q.shape
    scale = d ** -0.5
    n_c = (s_len - NSA_CMP_BLOCK) // NSA_CMP_STRIDE + 1
    n_sel = s_len // NSA_SLC_BLOCK
    tpos = jnp.arange(s_len)

    blk = np.arange(n_c)[:, None] * NSA_CMP_STRIDE + np.arange(NSA_CMP_BLOCK)[None, :]

    def compress(t, pe, w1, w2):
        tb = (t[:, blk] + pe).reshape(b, n_c, NSA_CMP_BLOCK * d)
        return jax.nn.gelu(tb @ w1) @ w2

    kc = compress(k_cmp, pe_k, cmp_k1, cmp_k2)
    vc = compress(v_cmp, pe_v, cmp_v1, cmp_v2)
    c_end = np.arange(n_c) * NSA_CMP_STRIDE + NSA_CMP_BLOCK - 1
    c_valid = c_end[None, :] <= tpos[:, None]
    sc = jnp.einsum('bhsd,bcd->bhsc', q, kc).astype(jnp.float32) * scale
    p_cmp = jax.nn.softmax(jnp.where(c_valid, sc, NEG_INF), axis=-1) * c_valid
    o_cmp = jnp.einsum('bhsc,bcd->bhsd', p_cmp.astype(vc.dtype), vc)

    overlap = np.zeros((n_c, n_sel), np.float32)
    for r in range(NSA_CMP_BLOCK // NSA_CMP_STRIDE):
        sub = np.arange(n_c) * NSA_CMP_STRIDE + r * NSA_CMP_STRIDE
        np.add.at(overlap, (np.arange(n_c), sub // NSA_SLC_BLOCK), 1.0)
    imp = jnp.einsum('bsc,cj->bsj', p_cmp.sum(axis=1), overlap)
    blk_id = jnp.arange(n_sel)
    cur = tpos // NSA_SLC_BLOCK
    sel_valid = blk_id[None, :] <= cur[:, None]
    forced = sel_valid & ((blk_id[None, :] == 0) |
                          (blk_id[None, :] >= cur[:, None] - (NSA_LOCAL_BLOCKS - 1)))
    score = jnp.where(forced, FORCE_SCORE, jnp.where(sel_valid, imp, NEG_INF))
    n_top = min(NSA_TOPK, n_sel)
    _, top_idx = lax.top_k(score, n_top)

    ks_blocks = k_slc.reshape(b, n_sel, NSA_SLC_BLOCK, d)
    vs_blocks = v_slc.reshape(b, n_sel, NSA_SLC_BLOCK, d)
    gather = jax.vmap(lambda blocks, idx: blocks[idx])

    def slc_body(i):
        q0 = i * Q_BLOCK
        qb = lax.dynamic_slice_in_dim(q_rot, q0, Q_BLOCK, axis=2)
        ib = lax.dynamic_slice_in_dim(top_idx, q0, Q_BLOCK, axis=1)
        ks = gather(ks_blocks, ib)
        vs = gather(vs_blocks, ib)
        qpos = q0 + jnp.arange(Q_BLOCK)
        kpos = ib[..., None] * NSA_SLC_BLOCK + jnp.arange(NSA_SLC_BLOCK)
        valid = kpos <= qpos[None, :, None, None]
        s = jnp.einsum('bhqd,bqnld->bhqnl', qb, ks).astype(jnp.float32) * scale
        s = jnp.where(valid[:, None], s, NEG_INF).reshape(b, h, Q_BLOCK, -1)
        p = jax.nn.softmax(s, axis=-1).reshape(b, h, Q_BLOCK, n_top, NSA_SLC_BLOCK)
        return jnp.einsum('bhqnl,bqnld->bhqd', p.astype(vs.dtype), vs)

    o_slc = _sweep(slc_body, s_len // Q_BLOCK)

    span = NSA_WINDOW + Q_BLOCK
    kw_pad = jnp.pad(k_win, ((0, 0), (NSA_WINDOW, 0), (0, 0)))
    vw_pad = jnp.pad(v_win, ((0, 0), (NSA_WINDOW, 0), (0, 0)))

    def win_body(i):
        q0 = i * Q_BLOCK
        qb = lax.dynamic_slice_in_dim(q_rot, q0, Q_BLOCK, axis=2)
        kb = lax.dynamic_slice_in_dim(kw_pad, q0, span, axis=1)
        vb = lax.dynamic_slice_in_dim(vw_pad, q0, span, axis=1)
        qpos = q0 + jnp.arange(Q_BLOCK)
        kpos = q0 - NSA_WINDOW + jnp.arange(span)
        valid = ((kpos[None, :] <= qpos[:, None]) & (kpos[None, :] > qpos[:, None] - NSA_WINDOW)
                 & (kpos[None, :] >= 0))
        s = jnp.einsum('bhqd,bkd->bhqk', qb, kb).astype(jnp.float32) * scale
        p = jax.nn.softmax(jnp.where(valid, s, NEG_INF), axis=-1)
        return jnp.einsum('bhqk,bkd->bhqd', p.astype(vb.dtype), vb)

    o_win = _sweep(win_body, s_len // Q_BLOCK)

    return gates[..., 0:1] * o_cmp + gates[..., 1:2] * o_slc + gates[..., 2:3] * o_win


def setup_inputs(seed: int = 0) -> dict:
    key = jax.random.key(seed)
    ks = jax.random.split(key, 20)
    f32 = jnp.float32

    def nrm(k, shape, scale):
        return jax.random.normal(k, shape, f32) * scale

    def gain(k, shape):
        return 1.0 + 0.05 * jax.random.normal(k, shape, f32)

    cmp_in = NSA_CMP_BLOCK * HEAD_DIM
    return {
        'x': nrm(ks[0], (BATCH, SEQ, D_MODEL), 1.0),
        'norm_mix_pre': gain(ks[1], (DEPTH, D_MODEL)),
        'norm_mix_post': gain(ks[2], (DEPTH, D_MODEL)),
        'norm_mlp_pre': gain(ks[3], (DEPTH, D_MODEL)),
        'norm_mlp_post': gain(ks[4], (DEPTH, D_MODEL)),
        'w_in': nrm(ks[5], (DEPTH, D_MODEL, IN_COLS), D_MODEL ** -0.5),
        'w_out': nrm(ks[6], (DEPTH, MIX_WIDTH, D_MODEL), MIX_WIDTH ** -0.5),
        'nsa_pe_k': nrm(ks[7], (DEPTH, NSA_CMP_BLOCK, HEAD_DIM), 0.1),
        'nsa_pe_v': nrm(ks[8], (DEPTH, NSA_CMP_BLOCK, HEAD_DIM), 0.1),
        'nsa_cmp_k1': nrm(ks[9], (DEPTH, cmp_in, NSA_CMP_HIDDEN), cmp_in ** -0.5),
        'nsa_cmp_k2': nrm(ks[10], (DEPTH, NSA_CMP_HIDDEN, HEAD_DIM), NSA_CMP_HIDDEN ** -0.5),
        'nsa_cmp_v1': nrm(ks[11], (DEPTH, cmp_in, NSA_CMP_HIDDEN), cmp_in ** -0.5),
        'nsa_cmp_v2': nrm(ks[12], (DEPTH, NSA_CMP_HIDDEN, HEAD_DIM), NSA_CMP_HIDDEN ** -0.5),
        'fox_forget_bias': 2.0 + nrm(ks[13], (DEPTH, HEADS_PER_MIXER), 0.5),
        'diff_lambda': nrm(ks[14], (DEPTH, 4, DIFF_QK_DIM), 0.1),
        'diff_norm': gain(ks[15], (DEPTH, HEAD_DIM)),
        'mlp_w_up': nrm(ks[16], (DEPTH, D_MODEL, 2 * D_FF), D_MODEL ** -0.5),
        'mlp_conv_w': nrm(ks[17], (DEPTH, CONV_WIDTH, 2 * D_FF), 0.6),
        'mlp_conv_b': nrm(ks[18], (DEPTH, 2 * D_FF), 0.01),
        'mlp_w_down': nrm(ks[19], (DEPTH, D_FF, D_MODEL), D_FF ** -0.5),
    }


def reference(x, norm_mix_pre, norm_mix_post, norm_mlp_pre, norm_mlp_post, w_in, w_out,
              nsa_pe_k, nsa_pe_v, nsa_cmp_k1, nsa_cmp_k2, nsa_cmp_v1, nsa_cmp_v2,
              fox_forget_bias, diff_lambda, diff_norm, mlp_w_up, mlp_conv_w, mlp_conv_b,
              mlp_w_down):
    b, s_len, _ = x.shape
    h_n = HEADS_PER_MIXER
    cos_h, sin_h = _rope_tables(s_len, HEAD_DIM // ROPE_FRACTION)
    cos_d, sin_d = _rope_tables(s_len, DIFF_QK_DIM // ROPE_FRACTION)
    split_at = np.cumsum(IN_SPLITS)[:-1].tolist()

    for l in range(DEPTH):
        hin = _rms_norm(x, norm_mix_pre[l])
        proj = hin @ w_in[l]
        (a_q, a_kc, a_vc, a_ks, a_vs, a_kw, a_vw, a_g,
         b_q, b_k, b_v,
         c_q, c_k, c_v, c_f,
         d_q, d_k, d_v) = jnp.split(proj, split_at, axis=-1)

        qa = _heads(a_q, h_n)
        gates = jax.nn.sigmoid(a_g.reshape(b, s_len, h_n, 3).transpose(0, 2, 1, 3))
        o_a = _nsa_attention(qa, _partial_rope(qa, cos_h, sin_h), a_kc, a_vc,
                             _partial_rope(a_ks, cos_h, sin_h), a_vs,
                             _partial_rope(a_kw, cos_h, sin_h), a_vw, gates,
                             nsa_pe_k[l], nsa_pe_v[l], nsa_cmp_k1[l], nsa_cmp_k2[l],
                             nsa_cmp_v1[l], nsa_cmp_v2[l])

        o_b = _stick_breaking_attention(_heads(b_q, h_n), _heads(b_k, h_n), _heads(b_v, h_n))

        log_f = jax.nn.log_sigmoid((c_f + fox_forget_bias[l]).astype(jnp.float32)).transpose(0, 2, 1)
        o_c = _forgetting_attention(_heads(c_q, h_n), _heads(c_k, h_n), _heads(c_v, h_n), log_f)

        dqh = _heads(d_q, h_n)
        dkh = _heads(d_k, h_n)
        q1 = _partial_rope(dqh[..., :DIFF_QK_DIM], cos_d, sin_d)
        q2 = _partial_rope(dqh[..., DIFF_QK_DIM:], cos_d, sin_d)
        k1 = _partial_rope(dkh[..., :DIFF_QK_DIM], cos_d, sin_d)
        k2 = _partial_rope(dkh[..., DIFF_QK_DIM:], cos_d, sin_d)
        lam_init = 0.8 - 0.6 * math.exp(-0.3 * l)
        lp = diff_lambda[l].astype(jnp.float32)
        lam = jnp.exp(jnp.sum(lp[0] * lp[1])) - jnp.exp(jnp.sum(lp[2] * lp[3])) + lam_init
        o_d = _differential_attention(q1, q2, k1, k2, _heads(d_v, h_n), lam)
        o_d = _rms_norm(o_d, diff_norm[l]) * (1.0 - lam_init)

        mixed = jnp.concatenate([_merge_heads(o_a), _merge_heads(o_b),
                                 _merge_heads(o_c), _merge_heads(o_d)], axis=-1) @ w_out[l]
        x = x + _rms_norm(mixed, norm_mix_post[l])

        hin = _rms_norm(x, norm_mlp_pre[l])
        u = hin @ mlp_w_up[l]
        u_pad = jnp.pad(u, ((0, 0), (CONV_WIDTH - 1, 0), (0, 0)))
        conv = mlp_conv_b[l] + mlp_conv_w[l][0] * u_pad[:, 0:s_len]
        for tap in range(1, CONV_WIDTH):
            conv = conv + mlp_conv_w[l][tap] * u_pad[:, tap:tap + s_len]
        gate, up = jnp.split(conv, 2, axis=-1)
        y = (jax.nn.gelu(gate, approximate=True) * up) @ mlp_w_down[l]
        x = x + _rms_norm(y, norm_mlp_post[l])

    return x
```

```python
import functools
import math

import jax
import jax.numpy as jnp
import numpy as np
from jax import lax
from jax.experimental import pallas as pl
from jax.experimental.pallas import tpu as pltpu

F32 = jnp.float32
BF16 = jnp.bfloat16

LANES = 128
HEAD_DIM = 128
N_HEADS = 4
GROUP = N_HEADS * HEAD_DIM
NSA_CMP_BLOCK = 32
NSA_CMP_STRIDE = 16
NSA_SLC_BLOCK = 64
NSA_TOPK = 16
NSA_WINDOW = 512
DIFF_QK = HEAD_DIM // 2
EPS = 1e-6
NEG_INF = -1e30
FORCE_SCORE = 1e9
ROPE_THETA = 500000.0
SCALE_HEAD = HEAD_DIM ** -0.5
SCALE_DIFF = DIFF_QK ** -0.5

_IN_SPLITS = (GROUP, 128, 128, 128, 128, 128, 128, 3 * N_HEADS,
              GROUP, GROUP, GROUP, GROUP, GROUP, GROUP, N_HEADS, GROUP, GROUP, GROUP)
COL_AQ, COL_AKC, COL_AVC, COL_AKS, COL_AVS, COL_AKW, COL_AVW, COL_AG = 0, 4, 5, 6, 7, 8, 9, 10
COL_BQ, COL_BK, COL_BV = 11, 15, 19
COL_CQ, COL_CK, COL_CV, COL_CF = 23, 27, 31, 35
COL_DQ, COL_DK, COL_DV = 36, 40, 44
PROJ_BLOCKS = 48

VMEM_LIMIT = 56 * 1024 * 1024


def _params(n_axes, vmem=VMEM_LIMIT):
    return pltpu.CompilerParams(dimension_semantics=("arbitrary",) * n_axes,
                                vmem_limit_bytes=vmem)


def _rms(x, g):
    ms = jnp.mean(x * x, axis=-1, keepdims=True)
    return x * lax.rsqrt(ms + EPS) * g


def _dot(a, b):
    return jnp.dot(a, b, preferred_element_type=F32)


def _dot_nt(a, b):
    return lax.dot_general(a, b, (((1,), (1,)), ((), ())), preferred_element_type=F32)


def _split3(x):
    hi = x.astype(BF16)
    r = x - hi.astype(F32)
    mid = r.astype(BF16)
    lo = (r - mid.astype(F32)).astype(BF16)
    return hi, mid, lo


def _dot3(x, m):
    hi, mid, lo = _split3(x)
    return _dot(hi, m) + _dot(mid, m) + _dot(lo, m)


def _dot3_left(m, x):
    hi, mid, lo = _split3(x)
    return _dot(m, hi) + _dot(m, mid) + _dot(m, lo)


def _rope(x, c, s, half):
    lane = lax.broadcasted_iota(jnp.int32, (1, LANES), 1)
    first = (lane % (2 * half)) < half
    xs = jnp.where(first, pltpu.roll(x, LANES - half, 1), pltpu.roll(x, half, 1))
    return x * c + xs * s


def _flash_step(s, vt, m, l, acc):
    m_new = jnp.maximum(m, jnp.max(s, axis=1, keepdims=True))
    alpha = jnp.exp(m - m_new)
    p = jnp.exp(s - m_new)
    l = alpha * l + jnp.sum(p, axis=1, keepdims=True)
    acc = alpha * acc + _dot(p.astype(BF16), vt)
    return m_new, l, acc


def _flash_init(rows):
    return (jnp.full((rows, 1), NEG_INF, F32), jnp.zeros((rows, 1), F32),
            jnp.zeros((rows, HEAD_DIM), F32))


def _norm_matmul_kernel(x_ref, g_ref, w_ref, o_ref, xn_ref):
    @pl.when(pl.program_id(1) == 0)
    def _():
        xn_ref[...] = _rms(x_ref[...], g_ref[...]).astype(BF16)

    o_ref[...] = _dot(xn_ref[...], w_ref[...])


def _norm_matmul(x2d, g, w, tm=512, tn=512):
    t, d = x2d.shape
    n = w.shape[1]
    return pl.pallas_call(
        _norm_matmul_kernel,
        grid=(t // tm, n // tn),
        in_specs=[pl.BlockSpec((tm, d), lambda i, j: (i, 0)),
                  pl.BlockSpec((1, d), lambda i, j: (0, 0)),
                  pl.BlockSpec((d, tn), lambda i, j: (0, j))],
        out_specs=pl.BlockSpec((tm, tn), lambda i, j: (i, j)),
        out_shape=jax.ShapeDtypeStruct((t, n), F32),
        scratch_shapes=[pltpu.VMEM((tm, d), BF16)],
        compiler_params=_params(2),
        name="norm_matmul",
    )(x2d, g, w)


def _sb_kernel(q_ref, k_ref, v_ref, o_ref, kb_ref, vb_ref, *, tq, tk):
    i = pl.program_id(2)

    @pl.when(i == 0)
    def _():
        kb_ref[...] = k_ref[...].astype(BF16)
        vb_ref[...] = v_ref[...].astype(BF16)

    r = tq // tk
    q = (q_ref[...] * SCALE_HEAD).astype(BF16)
    row = i * tq + lax.broadcasted_iota(jnp.int32, (tq, 1), 0)
    jr = lax.broadcasted_iota(jnp.int32, (tk, 2 * tk), 0)
    jc = lax.broadcasted_iota(jnp.int32, (tk, 2 * tk), 1)
    u = jnp.where((jc >= tk) | (jr > jc), 1.0, 0.0).astype(BF16)

    def tile(j, c, acc, masked):
        start = pl.multiple_of(j * tk, tk)
        kt = kb_ref[pl.ds(start, tk), :]
        vt = vb_ref[pl.ds(start, tk), :]
        z = _dot_nt(q, kt)
        ls = jnp.minimum(z, 0.0) - jnp.log1p(jnp.exp(-jnp.abs(z)))
        lk = ls - z
        if masked:
            col = j * tk + lax.broadcasted_iota(jnp.int32, (1, tk), 1)
            past = col < row
            lk = jnp.where(past, lk, 0.0)
        suf = _dot3(lk, u)
        e = jnp.exp(ls + suf[:, :tk] + c)
        if masked:
            e = jnp.where(past, e, 0.0)
        acc = acc + _dot(e.astype(BF16), vt)
        return c + suf[:, tk:], acc

    c = jnp.zeros((tq, tk), F32)
    acc = jnp.zeros((tq, HEAD_DIM), F32)
    for d in range(r):
        c, acc = tile(i * r + (r - 1 - d), c, acc, True)
    c, acc = lax.fori_loop(0, i * r, lambda jj, ca: tile(i * r - 1 - jj, ca[0], ca[1], False),
                           (c, acc))
    o_ref[...] = acc.astype(BF16)


def _sb_attention(proj, tq=256, tk=128):
    b, s, _ = proj.shape
    return pl.pallas_call(
        functools.partial(_sb_kernel, tq=tq, tk=tk),
        grid=(b, N_HEADS, s // tq),
        in_specs=[pl.BlockSpec((None, tq, HEAD_DIM), lambda bi, h, i: (bi, i, COL_BQ + h)),
                  pl.BlockSpec((None, s, HEAD_DIM), lambda bi, h, i: (bi, 0, COL_BK + h)),
                  pl.BlockSpec((None, s, HEAD_DIM), lambda bi, h, i: (bi, 0, COL_BV + h))],
        out_specs=pl.BlockSpec((None, tq, HEAD_DIM), lambda bi, h, i: (bi, i, h)),
        out_shape=jax.ShapeDtypeStruct((b, s, GROUP), BF16),
        scratch_shapes=[pltpu.VMEM((s, HEAD_DIM), BF16), pltpu.VMEM((s, HEAD_DIM), BF16)],
        compiler_params=_params(3),
        name="sb_attention",
    )(proj, proj, proj)


def _fox_cum_kernel(cf_ref, bias_ref, col_ref, row_ref, *, chunk):
    s = cf_ref.shape[0]
    lf = jax.nn.log_sigmoid(cf_ref[...] + bias_ref[...])
    rr = lax.broadcasted_iota(jnp.int32, (chunk, chunk), 0)
    cc = lax.broadcasted_iota(jnp.int32, (chunk, chunk), 1)
    tri = jnp.where(rr >= cc, 1.0, 0.0).astype(BF16)
    carry = jnp.zeros((1, LANES), F32)
    for c in range(s // chunk):
        pre = _dot3_left(tri, lf[c * chunk:(c + 1) * chunk]) + carry
        col_ref[c * chunk:(c + 1) * chunk, :] = pre
        carry = pre[chunk - 1:chunk, :]
    row_ref[...] = jnp.transpose(col_ref[...])[:8, :]


def _fox_cum(proj, bias_row, chunk=256):
    b, s, _ = proj.shape
    return pl.pallas_call(
        functools.partial(_fox_cum_kernel, chunk=chunk),
        grid=(b,),
        in_specs=[pl.BlockSpec((None, s, LANES), lambda bi: (bi, 0, COL_CF)),
                  pl.BlockSpec((1, LANES), lambda bi: (0, 0))],
        out_specs=[pl.BlockSpec((None, s, LANES), lambda bi: (bi, 0, 0)),
                   pl.BlockSpec((None, 8, s), lambda bi: (bi, 0, 0))],
        out_shape=[jax.ShapeDtypeStruct((b, s, LANES), F32),
                   jax.ShapeDtypeStruct((b, 8, s), F32)],
        compiler_params=_params(1),
        name="fox_cum",
    )(proj, bias_row)


def _fox_kernel(q_ref, k_ref, v_ref, cq_ref, ck_ref, o_ref, kb_ref, vb_ref, *, tq, tk):
    h = pl.program_id(1)
    i = pl.program_id(2)

    @pl.when(i == 0)
    def _():
        kb_ref[...] = k_ref[...].astype(BF16)
        vb_ref[...] = v_ref[...].astype(BF16)

    r = tq // tk
    q = (q_ref[...] * SCALE_HEAD).astype(BF16)
    row = i * tq + lax.broadcasted_iota(jnp.int32, (tq, 1), 0)
    lane = lax.broadcasted_iota(jnp.int32, (1, LANES), 1)
    cq = jnp.sum(jnp.where(lane == h, cq_ref[...], 0.0), axis=1, keepdims=True)

    def tile(j, carry, masked):
        start = pl.multiple_of(j * tk, tk)
        kt = kb_ref[pl.ds(start, tk), :]
        vt = vb_ref[pl.ds(start, tk), :]
        s = _dot_nt(q, kt) + cq - ck_ref[pl.ds(j, 1), :]
        if masked:
            col = j * tk + lax.broadcasted_iota(jnp.int32, (1, tk), 1)
            s = jnp.where(col <= row, s, NEG_INF)
        return _flash_step(s, vt, *carry)

    carry = lax.fori_loop(0, i * r, lambda j, ca: tile(j, ca, False), _flash_init(tq))
    for d in range(r):
        carry = tile(i * r + d, carry, True)
    _, l, acc = carry
    o_ref[...] = (acc * (1.0 / l)).astype(BF16)


def _fox_attention(proj, cum_col, cum_row, tq=256, tk=128):
    b, s, _ = proj.shape
    ck = cum_row.reshape(b, 8, s // tk, tk)
    return pl.pallas_call(
        functools.partial(_fox_kernel, tq=tq, tk=tk),
        grid=(b, N_HEADS, s // tq),
        in_specs=[pl.BlockSpec((None, tq, HEAD_DIM), lambda bi, h, i: (bi, i, COL_CQ + h)),
                  pl.BlockSpec((None, s, HEAD_DIM), lambda bi, h, i: (bi, 0, COL_CK + h)),
                  pl.BlockSpec((None, s, HEAD_DIM), lambda bi, h, i: (bi, 0, COL_CV + h)),
                  pl.BlockSpec((None, tq, LANES), lambda bi, h, i: (bi, i, 0)),
                  pl.BlockSpec((None, None, s // tk, tk), lambda bi, h, i: (bi, h, 0, 0))],
        out_specs=pl.BlockSpec((None, tq, HEAD_DIM), lambda bi, h, i: (bi, i, h)),
        out_shape=jax.ShapeDtypeStruct((b, s, GROUP), BF16),
        scratch_shapes=[pltpu.VMEM((s, HEAD_DIM), BF16), pltpu.VMEM((s, HEAD_DIM), BF16)],
        compiler_params=_params(3),
        name="fox_attention",
    )(proj, proj, proj, cum_col, ck)


def _diff_kernel(q_ref, k_ref, v_ref, cq_ref, sq_ref, ck_ref, sk_ref, dl_ref, gn_ref, o_ref,
                 kb_ref, vb_ref, *, tq, tk):
    i = pl.program_id(2)
    half = DIFF_QK // 4 // 2

    @pl.when(i == 0)
    def _():
        kb_ref[...] = _rope(k_ref[...], ck_ref[...], sk_ref[...], half).astype(BF16)
        vb_ref[...] = v_ref[...].astype(BF16)

    r = tq // tk
    lane = lax.broadcasted_iota(jnp.int32, (1, LANES), 1)
    qr = _rope(q_ref[...], cq_ref[...], sq_ref[...], half) * SCALE_DIFF
    q = jnp.concatenate([jnp.where(lane < DIFF_QK, qr, 0.0),
                         jnp.where(lane >= DIFF_QK, qr, 0.0)], axis=0).astype(BF16)
    row1 = i * tq + lax.broadcasted_iota(jnp.int32, (tq, 1), 0)
    row = jnp.concatenate([row1, row1], axis=0)

    def tile(j, carry, masked):
        start = pl.multiple_of(j * tk, tk)
        kt = kb_ref[pl.ds(start, tk), :]
        vt = vb_ref[pl.ds(start, tk), :]
        s = _dot_nt(q, kt)
        if masked:
            col = j * tk + lax.broadcasted_iota(jnp.int32, (1, tk), 1)
            s = jnp.where(col <= row, s, NEG_INF)
        return _flash_step(s, vt, *carry)

    carry = lax.fori_loop(0, i * r, lambda j, ca: tile(j, ca, False), _flash_init(2 * tq))
    for d in range(r):
        carry = tile(i * r + d, carry, True)
    _, l, acc = carry
    o = acc * (1.0 / l)
    dl = dl_ref[...]
    lam = (jnp.exp(jnp.sum(dl[0:1] * dl[1:2], axis=1, keepdims=True))
           - jnp.exp(jnp.sum(dl[2:3] * dl[3:4], axis=1, keepdims=True)) + dl[4:5, 0:1])
    od = o[:tq] - lam * o[tq:]
    o_ref[...] = (_rms(od, gn_ref[...]) * dl[5:6, 0:1]).astype(BF16)


def _diff_attention(proj, cos_t, sin_t, dl, gn, tq=256, tk=128):
    b, s, _ = proj.shape
    return pl.pallas_call(
        functools.partial(_diff_kernel, tq=tq, tk=tk),
        grid=(b, N_HEADS, s // tq),
        in_specs=[pl.BlockSpec((None, tq, HEAD_DIM), lambda bi, h, i: (bi, i, COL_DQ + h)),
                  pl.BlockSpec((None, s, HEAD_DIM), lambda bi, h, i: (bi, 0, COL_DK + h)),
                  pl.BlockSpec((None, s, HEAD_DIM), lambda bi, h, i: (bi, 0, COL_DV + h)),
                  pl.BlockSpec((tq, LANES), lambda bi, h, i: (i, 0)),
                  pl.BlockSpec((tq, LANES), lambda bi, h, i: (i, 0)),
                  pl.BlockSpec((s, LANES), lambda bi, h, i: (0, 0)),
                  pl.BlockSpec((s, LANES), lambda bi, h, i: (0, 0)),
                  pl.BlockSpec((8, LANES), lambda bi, h, i: (0, 0)),
                  pl.BlockSpec((1, LANES), lambda bi, h, i: (0, 0))],
        out_specs=pl.BlockSpec((None, tq, HEAD_DIM), lambda bi, h, i: (bi, i, h)),
        out_shape=jax.ShapeDtypeStruct((b, s, GROUP), BF16),
        scratch_shapes=[pltpu.VMEM((s, HEAD_DIM), BF16), pltpu.VMEM((s, HEAD_DIM), BF16)],
        compiler_params=_params(3),
        name="diff_attention",
    )(proj, proj, proj, cos_t, sin_t, cos_t, sin_t, dl, gn)


def _cmp_kernel(k2_ref, v2_ref, pek_ref, pev_ref, wk1_ref, wk2_ref, wv1_ref, wv2_ref,
                kc_ref, vc_ref):
    nb = k2_ref.shape[0]
    w = NSA_CMP_STRIDE * HEAD_DIM

    def compress(x2, pe, w1_ref, w2_ref):
        a = _dot((x2 + pe[:, :w]).astype(BF16), w1_ref[:w, :])
        bb = _dot((x2 + pe[:, w:]).astype(BF16), w1_ref[w:, :])
        hid = jax.nn.gelu(a + pltpu.roll(bb, nb - 1, 0), approximate=True)
        return _dot(hid.astype(BF16), w2_ref[...])

    kc_ref[...] = compress(k2_ref[...], pek_ref[...], wk1_ref, wk2_ref).astype(BF16)
    vc_ref[...] = compress(v2_ref[...], pev_ref[...], wv1_ref, wv2_ref).astype(BF16)


def _nsa_compress(k2, v2, pek, pev, wk1, wk2, wv1, wv2):
    b, nb, w = k2.shape
    hid = wk1.shape[1]
    full = lambda *shape: pl.BlockSpec(shape, lambda bi: (0,) * len(shape))
    return pl.pallas_call(
        _cmp_kernel,
        grid=(b,),
        in_specs=[pl.BlockSpec((None, nb, w), lambda bi: (bi, 0, 0)),
                  pl.BlockSpec((None, nb, w), lambda bi: (bi, 0, 0)),
                  full(1, 2 * w), full(1, 2 * w),
                  full(2 * w, hid), full(hid, HEAD_DIM), full(2 * w, hid), full(hid, HEAD_DIM)],
        out_specs=[pl.BlockSpec((None, nb, HEAD_DIM), lambda bi: (bi, 0, 0)),
                   pl.BlockSpec((None, nb, HEAD_DIM), lambda bi: (bi, 0, 0))],
        out_shape=[jax.ShapeDtypeStruct((b, nb, HEAD_DIM), BF16),
                   jax.ShapeDtypeStruct((b, nb, HEAD_DIM), BF16)],
        compiler_params=_params(1),
        name="nsa_compress",
    )(k2, v2, pek, pev, wk1, wk2, wv1, wv2)


def _nsa_kernel(q_ref, g_ref, kc_ref, vc_ref, ks_ref, vs_ref, kw_ref, vw_ref,
                cq_ref, sq_ref, ck_ref, sk_ref, o_ref,
                ksb_ref, vsb_ref, kwb_ref, vwb_ref, *, tq, tk, n_top):
    i = pl.program_id(1)
    half = HEAD_DIM // 4 // 2
    nb = kc_ref.shape[0]
    n_sel = NSA_SLC_BLOCK
    hq = N_HEADS * tq

    @pl.when(i == 0)
    def _():
        ksb_ref[...] = _rope(ks_ref[...], ck_ref[...], sk_ref[...], half).astype(BF16)
        kwb_ref[...] = _rope(kw_ref[...], ck_ref[...], sk_ref[...], half).astype(BF16)
        vsb_ref[...] = vs_ref[...].astype(BF16)
        vwb_ref[...] = vw_ref[...].astype(BF16)

    r = tq // tk
    q = q_ref[...]
    heads = [q[:, h * HEAD_DIM:(h + 1) * HEAD_DIM] for h in range(N_HEADS)]
    cqt, sqt = cq_ref[...], sq_ref[...]
    q_raw = jnp.concatenate([x * SCALE_HEAD for x in heads], axis=0).astype(BF16)
    q_rot = jnp.concatenate([_rope(x, cqt, sqt, half) * SCALE_HEAD for x in heads],
                            axis=0).astype(BF16)
    row = i * tq + lax.broadcasted_iota(jnp.int32, (tq, 1), 0)
    rep = lambda a: jnp.concatenate([a] * N_HEADS, axis=0)

    cidx = lax.broadcasted_iota(jnp.int32, (1, nb), 1)
    c_valid = rep((cidx * NSA_CMP_STRIDE + (NSA_CMP_BLOCK - 1)) <= row)
    sc = jnp.where(c_valid, _dot_nt(q_raw, kc_ref[...]), NEG_INF)
    e = jnp.exp(sc - jnp.max(sc, axis=1, keepdims=True))
    p = jnp.where(c_valid, e * (1.0 / jnp.sum(e, axis=1, keepdims=True)), 0.0)
    o_cmp = _dot(p.astype(BF16), vc_ref[...])
    p_sum = p[0:tq]
    for h in range(1, N_HEADS):
        p_sum = p_sum + p[h * tq:(h + 1) * tq]
    oc = lax.broadcasted_iota(jnp.int32, (nb, LANES), 0)
    oj = lax.broadcasted_iota(jnp.int32, (nb, LANES), 1)
    per = NSA_SLC_BLOCK // NSA_CMP_STRIDE
    overlap = (jnp.where(oc // per == oj, 1.0, 0.0)
               + jnp.where((oc + 1) // per == oj, 1.0, 0.0)).astype(BF16)
    imp = _dot3(p_sum, overlap)

    imp_t = jnp.transpose(imp)[:n_sel, :]
    jb = lax.broadcasted_iota(jnp.int32, (n_sel, 1), 0)
    cur = (i * tq + lax.broadcasted_iota(jnp.int32, (1, tq), 1)) // NSA_SLC_BLOCK
    sel_valid = jb <= cur
    forced = sel_valid & ((jb == 0) | (jb >= cur - 1))
    score = jnp.where(forced, FORCE_SCORE, jnp.where(sel_valid, imp_t, NEG_INF))
    cnt = jnp.zeros((n_sel, tq), F32)
    for ii in range(n_sel):
        ri = score[ii:ii + 1, :]
        beats = (ri > score) | ((ri == score) & (jb > ii))
        cnt = cnt + jnp.where(beats, 1.0, 0.0)
    sel_t = jnp.where((cnt < n_top) & sel_valid, 1.0, 0.0)
    sel = jnp.transpose(jnp.concatenate([sel_t, jnp.zeros_like(sel_t)], axis=0)).astype(BF16)

    er = lax.broadcasted_iota(jnp.int32, (LANES, tk), 0)
    ec = lax.broadcasted_iota(jnp.int32, (LANES, tk), 1)

    def slc_tile(j, carry, diag):
        start = pl.multiple_of(j * tk, tk)
        kt = ksb_ref[pl.ds(start, tk), :]
        vt = vsb_ref[pl.ds(start, tk), :]
        expand = jnp.where(er == (j * tk + ec) // NSA_SLC_BLOCK, 1.0, 0.0).astype(BF16)
        ok = _dot(sel, expand) > 0.5
        if diag:
            col = j * tk + lax.broadcasted_iota(jnp.int32, (1, tk), 1)
            ok = ok & (col <= row)
        s = jnp.where(rep(ok), _dot_nt(q_rot, kt), NEG_INF)
        return _flash_step(s, vt, *carry)

    carry = lax.fori_loop(0, i * r, lambda j, ca: slc_tile(j, ca, False), _flash_init(hq))
    for d in range(r):
        carry = slc_tile(i * r + d, carry, True)
    o_slc = carry[2] * (1.0 / carry[1])

    def win_tile(j, carry):
        start = pl.multiple_of(j * tk, tk)
        kt = kwb_ref[pl.ds(start, tk), :]
        vt = vwb_ref[pl.ds(start, tk), :]
        col = j * tk + lax.broadcasted_iota(jnp.int32, (1, tk), 1)
        ok = (col <= row) & (col > row - NSA_WINDOW)
        s = jnp.where(rep(ok), _dot_nt(q_rot, kt), NEG_INF)
        return _flash_step(s, vt, *carry)

    j_lo = jnp.maximum(i * r - NSA_WINDOW // tk, 0)
    carry = lax.fori_loop(j_lo, i * r + r, win_tile, _flash_init(hq))
    o_win = carry[2] * (1.0 / carry[1])

    g = jax.nn.sigmoid(g_ref[...])
    for h in range(N_HEADS):
        rows = slice(h * tq, (h + 1) * tq)
        o_ref[:, h * HEAD_DIM:(h + 1) * HEAD_DIM] = (
            g[:, 3 * h:3 * h + 1] * o_cmp[rows] + g[:, 3 * h + 1:3 * h + 2] * o_slc[rows]
            + g[:, 3 * h + 2:3 * h + 3] * o_win[rows]).astype(BF16)


def _nsa_attention(proj, kc, vc, cos_t, sin_t, tq=128, tk=128):
    b, s, _ = proj.shape
    nb = kc.shape[1]
    n_top = min(NSA_TOPK, s // NSA_SLC_BLOCK)
    kv = lambda col: pl.BlockSpec((None, s, HEAD_DIM), lambda bi, i: (bi, 0, col))
    return pl.pallas_call(
        functools.partial(_nsa_kernel, tq=tq, tk=tk, n_top=n_top),
        grid=(b, s // tq),
        in_specs=[pl.BlockSpec((None, tq, GROUP), lambda bi, i: (bi, i, 0)),
                  pl.BlockSpec((None, tq, LANES), lambda bi, i: (bi, i, COL_AG)),
                  pl.BlockSpec((None, nb, HEAD_DIM), lambda bi, i: (bi, 0, 0)),
                  pl.BlockSpec((None, nb, HEAD_DIM), lambda bi, i: (bi, 0, 0)),
                  kv(COL_AKS), kv(COL_AVS), kv(COL_AKW), kv(COL_AVW),
                  pl.BlockSpec((tq, LANES), lambda bi, i: (i, 0)),
                  pl.BlockSpec((tq, LANES), lambda bi, i: (i, 0)),
                  pl.BlockSpec((s, LANES), lambda bi, i: (0, 0)),
                  pl.BlockSpec((s, LANES), lambda bi, i: (0, 0))],
        out_specs=pl.BlockSpec((None, tq, GROUP), lambda bi, i: (bi, i, 0)),
        out_shape=jax.ShapeDtypeStruct((b, s, GROUP), BF16),
        scratch_shapes=[pltpu.VMEM((s, HEAD_DIM), BF16)] * 4,
        compiler_params=_params(2),
        name="nsa_attention",
    )(proj, proj, kc, vc, proj, proj, proj, proj, cos_t, sin_t, cos_t, sin_t)


def _out_kernel(oa_ref, ob_ref, oc_ref, od_ref, w_ref, x_ref, g_ref, o_ref):
    acc = _dot(oa_ref[...], w_ref[0:GROUP, :])
    acc = acc + _dot(ob_ref[...], w_ref[GROUP:2 * GROUP, :])
    acc = acc + _dot(oc_ref[...], w_ref[2 * GROUP:3 * GROUP, :])
    acc = acc + _dot(od_ref[...], w_ref[3 * GROUP:4 * GROUP, :])
    o_ref[...] = x_ref[...] + _rms(acc, g_ref[...])


def _out_proj(oa, ob, oc, od, w, x2d, g, tm=512):
    t, d = x2d.shape
    mix = pl.BlockSpec((tm, GROUP), lambda i: (i, 0))
    return pl.pallas_call(
        _out_kernel,
        grid=(t // tm,),
        in_specs=[mix, mix, mix, mix,
                  pl.BlockSpec((4 * GROUP, d), lambda i: (0, 0)),
                  pl.BlockSpec((tm, d), lambda i: (i, 0)),
                  pl.BlockSpec((1, d), lambda i: (0, 0))],
        out_specs=pl.BlockSpec((tm, d), lambda i: (i, 0)),
        out_shape=jax.ShapeDtypeStruct((t, d), F32),
        compiler_params=_params(1),
        name="out_proj",
    )(oa, ob, oc, od, w, x2d, g)


HALO = 16


def _mlp_kernel(x_ref, xh_ref, gpre_ref, wg_ref, wu_ref, cwg_ref, cwu_ref, cbg_ref, cbu_ref,
                wd_ref, gpost_ref, o_ref, xn_ref, acc_ref, *, tm, tiles_per_seq):
    i = pl.program_id(0)
    f = pl.program_id(1)

    @pl.when(f == 0)
    def _():
        xn_ref[HALO:, :] = _rms(x_ref[...], gpre_ref[...]).astype(BF16)
        halo = _rms(xh_ref[...], gpre_ref[...])
        seq_start = (i % tiles_per_seq) == 0
        xn_ref[:HALO, :] = jnp.where(seq_start, 0.0, halo).astype(BF16)
        acc_ref[...] = jnp.zeros_like(acc_ref)

    xn = xn_ref[...]

    def conv(u, cw_ref, cb_ref):
        cw = cw_ref[...]
        out = cb_ref[...] + cw[0:1] * u[HALO - 2:HALO - 2 + tm]
        out = out + cw[1:2] * u[HALO - 1:HALO - 1 + tm]
        return out + cw[2:3] * u[HALO:HALO + tm]

    gate = conv(_dot(xn, wg_ref[...]), cwg_ref, cbg_ref)
    up = conv(_dot(xn, wu_ref[...]), cwu_ref, cbu_ref)
    hid = (jax.nn.gelu(gate, approximate=True) * up).astype(BF16)
    acc_ref[...] += _dot(hid, wd_ref[...])

    @pl.when(f == pl.num_programs(1) - 1)
    def _():
        o_ref[...] = x_ref[...] + _rms(acc_ref[...], gpost_ref[...])


def _mlp(x2d, gpre, w_up, conv_w, conv_b, w_down, gpost, seq, tm=512, tf=512):
    t, d = x2d.shape
    ff = w_down.shape[0]
    nf = ff // tf
    hb = tm // HALO
    return pl.pallas_call(
        functools.partial(_mlp_kernel, tm=tm, tiles_per_seq=seq // tm),
        grid=(t // tm, nf),
        in_specs=[pl.BlockSpec((tm, d), lambda i, f: (i, 0)),
                  pl.BlockSpec((HALO, d), lambda i, f: (jnp.maximum(i * hb - 1, 0), 0)),
                  pl.BlockSpec((1, d), lambda i, f: (0, 0)),
                  pl.BlockSpec((d, tf), lambda i, f: (0, f)),
                  pl.BlockSpec((d, tf), lambda i, f: (0, nf + f)),
                  pl.BlockSpec((3, tf), lambda i, f: (0, f)),
                  pl.BlockSpec((3, tf), lambda i, f: (0, nf + f)),
                  pl.BlockSpec((1, tf), lambda i, f: (0, f)),
                  pl.BlockSpec((1, tf), lambda i, f: (0, nf + f)),
                  pl.BlockSpec((tf, d), lambda i, f: (f, 0)),
                  pl.BlockSpec((1, d), lambda i, f: (0, 0))],
        out_specs=pl.BlockSpec((tm, d), lambda i, f: (i, 0)),
        out_shape=jax.ShapeDtypeStruct((t, d), F32),
        scratch_shapes=[pltpu.VMEM((tm + HALO, d), BF16), pltpu.VMEM((tm, d), F32)],
        compiler_params=_params(2),
        name="conv_geglu_mlp",
    )(x2d, x2d, gpre, w_up, w_up, conv_w, conv_w, conv_b, conv_b, w_down, gpost)


def _pad_w_in(w):
    parts = jnp.split(w, np.cumsum(_IN_SPLITS)[:-1].tolist(), axis=-1)
    out = []
    for p in parts:
        extra = (-p.shape[-1]) % LANES
        out.append(jnp.pad(p, ((0, 0), (0, 0), (0, extra))) if extra else p)
    return jnp.concatenate(out, axis=-1)


def _rope_lane_tables(seq, rot_dim, period):
    pos = jnp.arange(seq, dtype=F32)
    inv = ROPE_THETA ** (-jnp.arange(0, rot_dim, 2, dtype=F32) / rot_dim)
    ang = pos[:, None] * inv[None, :]
    cos, sin = jnp.cos(ang), jnp.sin(ang)
    rest = period - rot_dim
    c = jnp.concatenate([cos, cos, jnp.ones((seq, rest), F32)], axis=-1)
    s = jnp.concatenate([-sin, sin, jnp.zeros((seq, rest), F32)], axis=-1)
    reps = LANES // period
    return jnp.tile(c, (1, reps)), jnp.tile(s, (1, reps))


@jax.jit
def _forward(x, norm_mix_pre, norm_mix_post, norm_mlp_pre, norm_mlp_post, w_in, w_out,
             nsa_pe_k, nsa_pe_v, nsa_cmp_k1, nsa_cmp_k2, nsa_cmp_v1, nsa_cmp_v2,
             fox_forget_bias, diff_lambda, diff_norm, mlp_w_up, mlp_conv_w, mlp_conv_b,
             mlp_w_down):
    b, s, d = x.shape
    depth = w_in.shape[0]
    cos_h, sin_h = _rope_lane_tables(s, HEAD_DIM // 4, LANES)
    cos_d, sin_d = _rope_lane_tables(s, DIFF_QK // 4, DIFF_QK)
    w_in_b = _pad_w_in(w_in).astype(BF16)
    w_out_b = w_out.astype(BF16)
    w_up_b = mlp_w_up.astype(BF16)
    w_down_b = mlp_w_down.astype(BF16)
    k1_b, k2_b = nsa_cmp_k1.astype(BF16), nsa_cmp_k2.astype(BF16)
    v1_b, v2_b = nsa_cmp_v1.astype(BF16), nsa_cmp_v2.astype(BF16)
    fox_bias = jnp.pad(fox_forget_bias, ((0, 0), (0, LANES - N_HEADS)))
    row2 = lambda a: a.reshape(1, -1)
    cmp_w = NSA_CMP_STRIDE * HEAD_DIM

    xt = x.reshape(b * s, d)
    for l in range(depth):
        proj = _norm_matmul(xt, row2(norm_mix_pre[l]), w_in_b[l]).reshape(b, s, PROJ_BLOCKS * LANES)

        k2 = proj[:, :, COL_AKC * LANES:(COL_AKC + 1) * LANES].reshape(b, s // NSA_CMP_STRIDE, cmp_w)
        v2 = proj[:, :, COL_AVC * LANES:(COL_AVC + 1) * LANES].reshape(b, s // NSA_CMP_STRIDE, cmp_w)
        kc, vc = _nsa_compress(k2, v2, row2(nsa_pe_k[l]), row2(nsa_pe_v[l]),
                               k1_b[l], k2_b[l], v1_b[l], v2_b[l])
        o_a = _nsa_attention(proj, kc, vc, cos_h, sin_h)

        o_b = _sb_attention(proj)

        cum_col, cum_row = _fox_cum(proj, fox_bias[l:l + 1])
        o_c = _fox_attention(proj, cum_col, cum_row)

        lam_init = 0.8 - 0.6 * math.exp(-0.3 * l)
        dl = jnp.concatenate([
            jnp.pad(diff_lambda[l], ((0, 0), (0, LANES - DIFF_QK))),
            jnp.full((1, LANES), lam_init, F32),
            jnp.full((1, LANES), 1.0 - lam_init, F32),
            jnp.zeros((2, LANES), F32)], axis=0)
        o_d = _diff_attention(proj, cos_d, sin_d, dl, row2(diff_norm[l]))

        flat = lambda o: o.reshape(b * s, GROUP)
        xt = _out_proj(flat(o_a), flat(o_b), flat(o_c), flat(o_d), w_out_b[l], xt,
                       row2(norm_mix_post[l]))
        xt = _mlp(xt, row2(norm_mlp_pre[l]), w_up_b[l], mlp_conv_w[l], row2(mlp_conv_b[l]),
                  w_down_b[l], row2(norm_mlp_post[l]), s)
    return xt.reshape(b, s, d)


def kernel(x, norm_mix_pre, norm_mix_post, norm_mlp_pre, norm_mlp_post, w_in, w_out, nsa_pe_k, nsa_pe_v, nsa_cmp_k1, nsa_cmp_k2, nsa_cmp_v1, nsa_cmp_v2, fox_forget_bias, diff_lambda, diff_norm, mlp_w_up, mlp_conv_w, mlp_conv_b, mlp_w_down):
    return _forward(x, norm_mix_pre, norm_mix_post, norm_mlp_pre, norm_mlp_post, w_in, w_out,
                    nsa_pe_k, nsa_pe_v, nsa_cmp_k1, nsa_cmp_k2, nsa_cmp_v1, nsa_cmp_v2,
                    fox_forget_bias, diff_lambda, diff_norm, mlp_w_up, mlp_conv_w, mlp_conv_b,
                    mlp_w_down)
```

```python
import functools
import math

import jax
import jax.numpy as jnp
import numpy as np
from jax import lax
from jax.experimental import pallas as pl
from jax.experimental.pallas import tpu as pltpu

F32 = jnp.float32
BF16 = jnp.bfloat16

LANES = 128
HEAD_DIM = 128
N_HEADS = 4
GROUP = N_HEADS * HEAD_DIM
NSA_CMP_BLOCK = 32
NSA_CMP_STRIDE = 16
NSA_SLC_BLOCK = 64
NSA_TOPK = 16
NSA_WINDOW = 512
DIFF_QK = HEAD_DIM // 2
EPS = 1e-6
NEG_INF = -1e30
FORCE_SCORE = 1e9
ROPE_THETA = 500000.0
SCALE_HEAD = HEAD_DIM ** -0.5
SCALE_DIFF = DIFF_QK ** -0.5

_IN_SPLITS = (GROUP, 128, 128, 128, 128, 128, 128, 3 * N_HEADS,
              GROUP, GROUP, GROUP, GROUP, GROUP, GROUP, N_HEADS, GROUP, GROUP, GROUP)
COL_AQ, COL_AKC, COL_AVC, COL_AKS, COL_AVS, COL_AKW, COL_AVW, COL_AG = 0, 4, 5, 6, 7, 8, 9, 10
COL_BQ, COL_BK, COL_BV = 11, 15, 19
COL_CQ, COL_CK, COL_CV, COL_CF = 23, 27, 31, 35
COL_DQ, COL_DK, COL_DV = 36, 40, 44
PROJ_BLOCKS = 48

VMEM_LIMIT = 56 * 1024 * 1024


def _params(n_axes, vmem=VMEM_LIMIT):
    return pltpu.CompilerParams(dimension_semantics=("arbitrary",) * n_axes,
                                vmem_limit_bytes=vmem)


def _rms(x, g):
    ms = jnp.mean(x * x, axis=-1, keepdims=True)
    return x * lax.rsqrt(ms + EPS) * g


def _dot(a, b):
    return jnp.dot(a, b, preferred_element_type=F32)


def _dot_nt(a, b):
    return lax.dot_general(a, b, (((1,), (1,)), ((), ())), preferred_element_type=F32)


def _split3(x):
    hi = x.astype(BF16)
    r = x - hi.astype(F32)
    mid = r.astype(BF16)
    lo = (r - mid.astype(F32)).astype(BF16)
    return hi, mid, lo


def _dot3(x, m):
    hi, mid, lo = _split3(x)
    return _dot(hi, m) + _dot(mid, m) + _dot(lo, m)


def _dot2(x, m):
    hi = x.astype(BF16)
    lo = (x - hi.astype(F32)).astype(BF16)
    return _dot(hi, m) + _dot(lo, m)


def _dot3_left(m, x):
    hi, mid, lo = _split3(x)
    return _dot(m, hi) + _dot(m, mid) + _dot(m, lo)


def _rope(x, c, s, half):
    lane = lax.broadcasted_iota(jnp.int32, (1, LANES), 1)
    first = (lane % (2 * half)) < half
    xs = jnp.where(first, pltpu.roll(x, LANES - half, 1), pltpu.roll(x, half, 1))
    return x * c + xs * s


def _lane_fold(x, op):
    out = x[:, :LANES]
    for b in range(1, x.shape[1] // LANES):
        out = op(out, x[:, b * LANES:(b + 1) * LANES])
    return out


def _row_max(s):
    return jnp.max(_lane_fold(s, jnp.maximum), axis=1, keepdims=True)


def _flash_step(s, vt, m, l, acc):
    m_new = jnp.maximum(m, _row_max(s))
    alpha = jnp.exp(m - m_new)
    p = jnp.exp(s - m_new)
    l = alpha * l + _lane_fold(p, jnp.add)
    acc = alpha * acc + _dot(p.astype(BF16), vt)
    return m_new, l, acc


def _flash_init(rows):
    return (jnp.full((rows, 1), NEG_INF, F32), jnp.zeros((rows, LANES), F32),
            jnp.zeros((rows, HEAD_DIM), F32))


def _flash_finish(carry):
    _, l, acc = carry
    return acc * (1.0 / jnp.sum(l, axis=1, keepdims=True))


def _flash_sweep(lo, n_full, n_diag, score_fn, value_fn, mask_fn, rows):
    def body(j, carry):
        s_next = score_fn(j + 1)
        return (s_next,) + _flash_step(carry[0], value_fn(j), *carry[1:])

    carry = lax.fori_loop(lo, n_full, body, (score_fn(lo),) + _flash_init(rows))
    s, carry = carry[0], carry[1:]
    for d in range(n_diag):
        if d > 0:
            s = score_fn(n_full + d)
        carry = _flash_step(mask_fn(n_full + d, s), value_fn(n_full + d), *carry)
    return _flash_finish(carry)


def _norm_matmul_kernel(x_ref, g_ref, w_ref, o_ref, xn_ref):
    @pl.when(pl.program_id(1) == 0)
    def _():
        xn_ref[...] = _rms(x_ref[...], g_ref[...]).astype(BF16)

    o_ref[...] = _dot(xn_ref[...], w_ref[...])


def _norm_matmul(x2d, g, w, tm=512, tn=512):
    t, d = x2d.shape
    n = w.shape[1]
    return pl.pallas_call(
        _norm_matmul_kernel,
        grid=(t // tm, n // tn),
        in_specs=[pl.BlockSpec((tm, d), lambda i, j: (i, 0)),
                  pl.BlockSpec((1, d), lambda i, j: (0, 0)),
                  pl.BlockSpec((d, tn), lambda i, j: (0, j))],
        out_specs=pl.BlockSpec((tm, tn), lambda i, j: (i, j)),
        out_shape=jax.ShapeDtypeStruct((t, n), F32),
        scratch_shapes=[pltpu.VMEM((tm, d), BF16)],
        compiler_params=_params(2),
        name="norm_matmul",
    )(x2d, g, w)


def _sb_kernel(q_ref, k_ref, v_ref, o_ref, kb_ref, vb_ref, *, tq, tk):
    i = pl.program_id(2)

    @pl.when(i == 0)
    def _():
        kb_ref[...] = k_ref[...].astype(BF16)
        vb_ref[...] = v_ref[...].astype(BF16)

    assert tq == tk
    nblk = tk // LANES
    q = (q_ref[...] * SCALE_HEAD).astype(BF16)
    row = i * tq + lax.broadcasted_iota(jnp.int32, (tq, 1), 0)
    col = i * tq + lax.broadcasted_iota(jnp.int32, (1, tk), 1)
    past = col < row
    jr = lax.broadcasted_iota(jnp.int32, (tk, tk), 0)
    jc = lax.broadcasted_iota(jnp.int32, (tk, tk), 1)
    u = jnp.where((jr // LANES == jc // LANES) & (jr > jc), 1.0, 0.0).astype(BF16)

    def logits(j, masked):
        kt = kb_ref[pl.ds(pl.multiple_of(j * tk, tk), tk), :]
        z = _dot_nt(q, kt)
        ls = jnp.minimum(z, 0.0) - jnp.log1p(jnp.exp(-jnp.abs(z)))
        lk = ls - z
        if masked:
            lk = jnp.where(past, lk, 0.0)
        suf = _dot2(lk, u)
        tot = jnp.zeros((tq, 1), F32)
        parts = [None] * nblk
        for b in reversed(range(nblk)):
            blk = slice(b * LANES, (b + 1) * LANES)
            parts[b] = ls[:, blk] + suf[:, blk] + tot
            tot = tot + jnp.sum(lk[:, blk], axis=1, keepdims=True)
        return jnp.concatenate(parts, axis=1), tot

    def weigh(j, pre, tot, c, acc, masked):
        vt = vb_ref[pl.ds(pl.multiple_of(j * tk, tk), tk), :]
        e = jnp.exp(pre + c)
        if masked:
            e = jnp.where(past, e, 0.0)
        return c + tot, acc + _dot(e.astype(BF16), vt)

    pre, tot = logits(i, True)
    c, acc = weigh(i, pre, tot, jnp.zeros((tq, 1), F32), jnp.zeros((tq, HEAD_DIM), F32), True)

    def body(jj, carry):
        pre, tot, c, acc = carry
        j = i - 1 - jj
        nxt = logits(jnp.maximum(j - 1, 0), False)
        c, acc = weigh(j, pre, tot, c, acc, False)
        return nxt + (c, acc)

    first = logits(jnp.maximum(i - 1, 0), False)
    carry = lax.fori_loop(0, i, body, first + (c, acc))
    o_ref[...] = carry[3].astype(BF16)


def _sb_attention(proj, tq=256, tk=256):
    b, s, _ = proj.shape
    return pl.pallas_call(
        functools.partial(_sb_kernel, tq=tq, tk=tk),
        grid=(b, N_HEADS, s // tq),
        in_specs=[pl.BlockSpec((None, tq, HEAD_DIM), lambda bi, h, i: (bi, i, COL_BQ + h)),
                  pl.BlockSpec((None, s, HEAD_DIM), lambda bi, h, i: (bi, 0, COL_BK + h)),
                  pl.BlockSpec((None, s, HEAD_DIM), lambda bi, h, i: (bi, 0, COL_BV + h))],
        out_specs=pl.BlockSpec((None, tq, HEAD_DIM), lambda bi, h, i: (bi, i, h)),
        out_shape=jax.ShapeDtypeStruct((b, s, GROUP), BF16),
        scratch_shapes=[pltpu.VMEM((s, HEAD_DIM), BF16), pltpu.VMEM((s, HEAD_DIM), BF16)],
        compiler_params=_params(3),
        name="sb_attention",
    )(proj, proj, proj)


def _fox_cum_kernel(cf_ref, bias_ref, col_ref, row_ref, *, chunk):
    s = cf_ref.shape[0]
    lf = jax.nn.log_sigmoid(cf_ref[...] + bias_ref[...])
    rr = lax.broadcasted_iota(jnp.int32, (chunk, chunk), 0)
    cc = lax.broadcasted_iota(jnp.int32, (chunk, chunk), 1)
    tri = jnp.where(rr >= cc, 1.0, 0.0).astype(BF16)
    carry = jnp.zeros((1, LANES), F32)
    for c in range(s // chunk):
        pre = _dot3_left(tri, lf[c * chunk:(c + 1) * chunk]) + carry
        col_ref[c * chunk:(c + 1) * chunk, :] = pre
        carry = pre[chunk - 1:chunk, :]
    row_ref[...] = jnp.transpose(col_ref[...])[:8, :]


def _fox_cum(proj, bias_row, chunk=256):
    b, s, _ = proj.shape
    return pl.pallas_call(
        functools.partial(_fox_cum_kernel, chunk=chunk),
        grid=(b,),
        in_specs=[pl.BlockSpec((None, s, LANES), lambda bi: (bi, 0, COL_CF)),
                  pl.BlockSpec((1, LANES), lambda bi: (0, 0))],
        out_specs=[pl.BlockSpec((None, s, LANES), lambda bi: (bi, 0, 0)),
                   pl.BlockSpec((None, 8, s), lambda bi: (bi, 0, 0))],
        out_shape=[jax.ShapeDtypeStruct((b, s, LANES), F32),
                   jax.ShapeDtypeStruct((b, 8, s), F32)],
        compiler_params=_params(1),
        name="fox_cum",
    )(proj, bias_row)


def _fox_kernel(q_ref, k_ref, v_ref, cq_ref, ck_ref, o_ref, kb_ref, vb_ref, *, tq, tk):
    h = pl.program_id(1)
    i = pl.program_id(2)

    @pl.when(i == 0)
    def _():
        kb_ref[...] = k_ref[...].astype(BF16)
        vb_ref[...] = v_ref[...].astype(BF16)

    r = tq // tk
    q = (q_ref[...] * SCALE_HEAD).astype(BF16)
    row = i * tq + lax.broadcasted_iota(jnp.int32, (tq, 1), 0)
    lane = lax.broadcasted_iota(jnp.int32, (1, LANES), 1)
    cq = jnp.sum(jnp.where(lane == h, cq_ref[...], 0.0), axis=1, keepdims=True)

    def score(j):
        kt = kb_ref[pl.ds(pl.multiple_of(j * tk, tk), tk), :]
        return _dot_nt(q, kt) + cq - ck_ref[pl.ds(j, 1), :]

    def value(j):
        return vb_ref[pl.ds(pl.multiple_of(j * tk, tk), tk), :]

    def causal(j, s):
        col = j * tk + lax.broadcasted_iota(jnp.int32, (1, tk), 1)
        return jnp.where(col <= row, s, NEG_INF)

    o_ref[...] = _flash_sweep(0, i * r, r, score, value, causal, tq).astype(BF16)


def _fox_attention(proj, cum_col, cum_row, tq=256, tk=256):
    b, s, _ = proj.shape
    ck = cum_row.reshape(b, 8, s // tk, tk)
    return pl.pallas_call(
        functools.partial(_fox_kernel, tq=tq, tk=tk),
        grid=(b, N_HEADS, s // tq),
        in_specs=[pl.BlockSpec((None, tq, HEAD_DIM), lambda bi, h, i: (bi, i, COL_CQ + h)),
                  pl.BlockSpec((None, s, HEAD_DIM), lambda bi, h, i: (bi, 0, COL_CK + h)),
                  pl.BlockSpec((None, s, HEAD_DIM), lambda bi, h, i: (bi, 0, COL_CV + h)),
                  pl.BlockSpec((None, tq, LANES), lambda bi, h, i: (bi, i, 0)),
                  pl.BlockSpec((None, None, s // tk, tk), lambda bi, h, i: (bi, h, 0, 0))],
        out_specs=pl.BlockSpec((None, tq, HEAD_DIM), lambda bi, h, i: (bi, i, h)),
        out_shape=jax.ShapeDtypeStruct((b, s, GROUP), BF16),
        scratch_shapes=[pltpu.VMEM((s, HEAD_DIM), BF16), pltpu.VMEM((s, HEAD_DIM), BF16)],
        compiler_params=_params(3),
        name="fox_attention",
    )(proj, proj, proj, cum_col, ck)


def _diff_kernel(q_ref, k_ref, v_ref, cq_ref, sq_ref, ck_ref, sk_ref, dl_ref, gn_ref, o_ref,
                 kb_ref, vb_ref, *, tq, tk):
    i = pl.program_id(2)
    half = DIFF_QK // 4 // 2

    @pl.when(i == 0)
    def _():
        kb_ref[...] = _rope(k_ref[...], ck_ref[...], sk_ref[...], half).astype(BF16)
        vb_ref[...] = v_ref[...].astype(BF16)

    r = tq // tk
    lane = lax.broadcasted_iota(jnp.int32, (1, LANES), 1)
    qr = _rope(q_ref[...], cq_ref[...], sq_ref[...], half) * SCALE_DIFF
    q = jnp.concatenate([jnp.where(lane < DIFF_QK, qr, 0.0),
                         jnp.where(lane >= DIFF_QK, qr, 0.0)], axis=0).astype(BF16)
    row1 = i * tq + lax.broadcasted_iota(jnp.int32, (tq, 1), 0)
    row = jnp.concatenate([row1, row1], axis=0)

    def score(j):
        return _dot_nt(q, kb_ref[pl.ds(pl.multiple_of(j * tk, tk), tk), :])

    def value(j):
        return vb_ref[pl.ds(pl.multiple_of(j * tk, tk), tk), :]

    def causal(j, s):
        col = j * tk + lax.broadcasted_iota(jnp.int32, (1, tk), 1)
        return jnp.where(col <= row, s, NEG_INF)

    o = _flash_sweep(0, i * r, r, score, value, causal, 2 * tq)
    dl = dl_ref[...]
    lam = (jnp.exp(jnp.sum(dl[0:1] * dl[1:2], axis=1, keepdims=True))
           - jnp.exp(jnp.sum(dl[2:3] * dl[3:4], axis=1, keepdims=True)) + dl[4:5, 0:1])
    od = o[:tq] - lam * o[tq:]
    o_ref[...] = (_rms(od, gn_ref[...]) * dl[5:6, 0:1]).astype(BF16)


def _diff_attention(proj, cos_t, sin_t, dl, gn, tq=256, tk=256):
    b, s, _ = proj.shape
    return pl.pallas_call(
        functools.partial(_diff_kernel, tq=tq, tk=tk),
        grid=(b, N_HEADS, s // tq),
        in_specs=[pl.BlockSpec((None, tq, HEAD_DIM), lambda bi, h, i: (bi, i, COL_DQ + h)),
                  pl.BlockSpec((None, s, HEAD_DIM), lambda bi, h, i: (bi, 0, COL_DK + h)),
                  pl.BlockSpec((None, s, HEAD_DIM), lambda bi, h, i: (bi, 0, COL_DV + h)),
                  pl.BlockSpec((tq, LANES), lambda bi, h, i: (i, 0)),
                  pl.BlockSpec((tq, LANES), lambda bi, h, i: (i, 0)),
                  pl.BlockSpec((s, LANES), lambda bi, h, i: (0, 0)),
                  pl.BlockSpec((s, LANES), lambda bi, h, i: (0, 0)),
                  pl.BlockSpec((8, LANES), lambda bi, h, i: (0, 0)),
                  pl.BlockSpec((1, LANES), lambda bi, h, i: (0, 0))],
        out_specs=pl.BlockSpec((None, tq, HEAD_DIM), lambda bi, h, i: (bi, i, h)),
        out_shape=jax.ShapeDtypeStruct((b, s, GROUP), BF16),
        scratch_shapes=[pltpu.VMEM((s, HEAD_DIM), BF16), pltpu.VMEM((s, HEAD_DIM), BF16)],
        compiler_params=_params(3),
        name="diff_attention",
    )(proj, proj, proj, cos_t, sin_t, cos_t, sin_t, dl, gn)


def _cmp_kernel(k2_ref, v2_ref, pek_ref, pev_ref, wk1_ref, wk2_ref, wv1_ref, wv2_ref,
                kc_ref, vc_ref):
    nb = k2_ref.shape[0]
    w = NSA_CMP_STRIDE * HEAD_DIM

    def compress(x2, pe, w1_ref, w2_ref):
        a = _dot((x2 + pe[:, :w]).astype(BF16), w1_ref[:w, :])
        bb = _dot((x2 + pe[:, w:]).astype(BF16), w1_ref[w:, :])
        hid = jax.nn.gelu(a + pltpu.roll(bb, nb - 1, 0), approximate=True)
        return _dot(hid.astype(BF16), w2_ref[...])

    kc_ref[...] = compress(k2_ref[...], pek_ref[...], wk1_ref, wk2_ref).astype(BF16)
    vc_ref[...] = compress(v2_ref[...], pev_ref[...], wv1_ref, wv2_ref).astype(BF16)


def _nsa_compress(k2, v2, pek, pev, wk1, wk2, wv1, wv2):
    b, nb, w = k2.shape
    hid = wk1.shape[1]
    full = lambda *shape: pl.BlockSpec(shape, lambda bi: (0,) * len(shape))
    return pl.pallas_call(
        _cmp_kernel,
        grid=(b,),
        in_specs=[pl.BlockSpec((None, nb, w), lambda bi: (bi, 0, 0)),
                  pl.BlockSpec((None, nb, w), lambda bi: (bi, 0, 0)),
                  full(1, 2 * w), full(1, 2 * w),
                  full(2 * w, hid), full(hid, HEAD_DIM), full(2 * w, hid), full(hid, HEAD_DIM)],
        out_specs=[pl.BlockSpec((None, nb, HEAD_DIM), lambda bi: (bi, 0, 0)),
                   pl.BlockSpec((None, nb, HEAD_DIM), lambda bi: (bi, 0, 0))],
        out_shape=[jax.ShapeDtypeStruct((b, nb, HEAD_DIM), BF16),
                   jax.ShapeDtypeStruct((b, nb, HEAD_DIM), BF16)],
        compiler_params=_params(1),
        name="nsa_compress",
    )(k2, v2, pek, pev, wk1, wk2, wv1, wv2)


def _nsa_kernel(q_ref, g_ref, kc_ref, vc_ref, ks_ref, vs_ref, kw_ref, vw_ref,
                cq_ref, sq_ref, ck_ref, sk_ref, o_ref,
                ksb_ref, vsb_ref, kwb_ref, vwb_ref, *, tq, tk, n_top):
    i = pl.program_id(1)
    half = HEAD_DIM // 4 // 2
    nb = kc_ref.shape[0]
    n_sel = NSA_SLC_BLOCK
    hq = N_HEADS * tq

    @pl.when(i == 0)
    def _():
        ksb_ref[...] = _rope(ks_ref[...], ck_ref[...], sk_ref[...], half).astype(BF16)
        kwb_ref[...] = _rope(kw_ref[...], ck_ref[...], sk_ref[...], half).astype(BF16)
        vsb_ref[...] = vs_ref[...].astype(BF16)
        vwb_ref[...] = vw_ref[...].astype(BF16)

    q = q_ref[...]
    heads = [q[:, h * HEAD_DIM:(h + 1) * HEAD_DIM] for h in range(N_HEADS)]
    cqt, sqt = cq_ref[...], sq_ref[...]
    q_raw = jnp.concatenate([x * SCALE_HEAD for x in heads], axis=0).astype(BF16)
    q_rot = jnp.concatenate([_rope(x, cqt, sqt, half) * SCALE_HEAD for x in heads],
                            axis=0).astype(BF16)
    row = i * tq + lax.broadcasted_iota(jnp.int32, (tq, 1), 0)
    rep = lambda a: jnp.concatenate([a] * N_HEADS, axis=0)

    cidx = lax.broadcasted_iota(jnp.int32, (1, nb), 1)
    c_valid = rep((cidx * NSA_CMP_STRIDE + (NSA_CMP_BLOCK - 1)) <= row)
    sc = jnp.where(c_valid, _dot_nt(q_raw, kc_ref[...]), NEG_INF)
    e = jnp.exp(sc - jnp.max(sc, axis=1, keepdims=True))
    p = jnp.where(c_valid, e * (1.0 / jnp.sum(e, axis=1, keepdims=True)), 0.0)
    o_cmp = _dot(p.astype(BF16), vc_ref[...])
    p_sum = p[0:tq]
    for h in range(1, N_HEADS):
        p_sum = p_sum + p[h * tq:(h + 1) * tq]
    oc = lax.broadcasted_iota(jnp.int32, (nb, LANES), 0)
    oj = lax.broadcasted_iota(jnp.int32, (nb, LANES), 1)
    per = NSA_SLC_BLOCK // NSA_CMP_STRIDE
    overlap = (jnp.where(oc // per == oj, 1.0, 0.0)
               + jnp.where((oc + 1) // per == oj, 1.0, 0.0)).astype(BF16)
    imp = _dot3(p_sum, overlap)

    imp_t = jnp.transpose(imp)[:n_sel, :]
    jb = lax.broadcasted_iota(jnp.int32, (n_sel, 1), 0)
    cur = (i * tq + lax.broadcasted_iota(jnp.int32, (1, tq), 1)) // NSA_SLC_BLOCK
    sel_valid = jb <= cur
    forced = sel_valid & ((jb == 0) | (jb >= cur - 1))
    score = jnp.where(forced, FORCE_SCORE, jnp.where(sel_valid, imp_t, NEG_INF))
    cnt = jnp.zeros((n_sel, tq), F32)
    for ii in range(n_sel):
        ri = score[ii:ii + 1, :]
        beats = (ri > score) | ((ri == score) & (jb > ii))
        cnt = cnt + jnp.where(beats, 1.0, 0.0)
    sel_t = jnp.where((cnt < n_top) & sel_valid, 1.0, 0.0)
    sel = jnp.transpose(jnp.concatenate([sel_t, jnp.zeros_like(sel_t)], axis=0)).astype(BF16)

    er = lax.broadcasted_iota(jnp.int32, (LANES, tk), 0)
    ec = lax.broadcasted_iota(jnp.int32, (LANES, tk), 1)

    def tile_rows(ref, j):
        return ref[pl.ds(pl.multiple_of(j * tk, tk), tk), :]

    def causal(j, s):
        col = j * tk + lax.broadcasted_iota(jnp.int32, (1, tk), 1)
        return jnp.where(rep(col <= row), s, NEG_INF)

    def slc_score(j):
        expand = jnp.where(er == (j * tk + ec) // NSA_SLC_BLOCK, 1.0, 0.0).astype(BF16)
        ok = _dot(sel, expand) > 0.5
        return jnp.where(rep(ok), _dot_nt(q_rot, tile_rows(ksb_ref, j)), NEG_INF)

    n_full = (i * tq) // tk
    n_diag = max(tq // tk, 1)
    o_slc = _flash_sweep(0, n_full, n_diag, slc_score, lambda j: tile_rows(vsb_ref, j),
                         causal, hq)

    def win_score(j):
        col = j * tk + lax.broadcasted_iota(jnp.int32, (1, tk), 1)
        return jnp.where(rep(col > row - NSA_WINDOW), _dot_nt(q_rot, tile_rows(kwb_ref, j)),
                         NEG_INF)

    j_lo = jnp.maximum(i * tq - NSA_WINDOW, 0) // tk
    o_win = _flash_sweep(j_lo, n_full, n_diag, win_score, lambda j: tile_rows(vwb_ref, j),
                         causal, hq)

    g = jax.nn.sigmoid(g_ref[...])
    for h in range(N_HEADS):
        rows = slice(h * tq, (h + 1) * tq)
        o_ref[:, h * HEAD_DIM:(h + 1) * HEAD_DIM] = (
            g[:, 3 * h:3 * h + 1] * o_cmp[rows] + g[:, 3 * h + 1:3 * h + 2] * o_slc[rows]
            + g[:, 3 * h + 2:3 * h + 3] * o_win[rows]).astype(BF16)


def _nsa_attention(proj, kc, vc, cos_t, sin_t, tq=128, tk=256):
    b, s, _ = proj.shape
    nb = kc.shape[1]
    n_top = min(NSA_TOPK, s // NSA_SLC_BLOCK)
    kv = lambda col: pl.BlockSpec((None, s, HEAD_DIM), lambda bi, i: (bi, 0, col))
    return pl.pallas_call(
        functools.partial(_nsa_kernel, tq=tq, tk=tk, n_top=n_top),
        grid=(b, s // tq),
        in_specs=[pl.BlockSpec((None, tq, GROUP), lambda bi, i: (bi, i, 0)),
                  pl.BlockSpec((None, tq, LANES), lambda bi, i: (bi, i, COL_AG)),
                  pl.BlockSpec((None, nb, HEAD_DIM), lambda bi, i: (bi, 0, 0)),
                  pl.BlockSpec((None, nb, HEAD_DIM), lambda bi, i: (bi, 0, 0)),
                  kv(COL_AKS), kv(COL_AVS), kv(COL_AKW), kv(COL_AVW),
                  pl.BlockSpec((tq, LANES), lambda bi, i: (i, 0)),
                  pl.BlockSpec((tq, LANES), lambda bi, i: (i, 0)),
                  pl.BlockSpec((s, LANES), lambda bi, i: (0, 0)),
                  pl.BlockSpec((s, LANES), lambda bi, i: (0, 0))],
        out_specs=pl.BlockSpec((None, tq, GROUP), lambda bi, i: (bi, i, 0)),
        out_shape=jax.ShapeDtypeStruct((b, s, GROUP), BF16),
        scratch_shapes=[pltpu.VMEM((s, HEAD_DIM), BF16)] * 4,
        compiler_params=_params(2),
        name="nsa_attention",
    )(proj, proj, kc, vc, proj, proj, proj, proj, cos_t, sin_t, cos_t, sin_t)


def _out_kernel(oa_ref, ob_ref, oc_ref, od_ref, w_ref, x_ref, g_ref, o_ref):
    acc = _dot(oa_ref[...], w_ref[0:GROUP, :])
    acc = acc + _dot(ob_ref[...], w_ref[GROUP:2 * GROUP, :])
    acc = acc + _dot(oc_ref[...], w_ref[2 * GROUP:3 * GROUP, :])
    acc = acc + _dot(od_ref[...], w_ref[3 * GROUP:4 * GROUP, :])
    o_ref[...] = x_ref[...] + _rms(acc, g_ref[...])


def _out_proj(oa, ob, oc, od, w, x2d, g, tm=512):
    t, d = x2d.shape
    mix = pl.BlockSpec((tm, GROUP), lambda i: (i, 0))
    return pl.pallas_call(
        _out_kernel,
        grid=(t // tm,),
        in_specs=[mix, mix, mix, mix,
                  pl.BlockSpec((4 * GROUP, d), lambda i: (0, 0)),
                  pl.BlockSpec((tm, d), lambda i: (i, 0)),
                  pl.BlockSpec((1, d), lambda i: (0, 0))],
        out_specs=pl.BlockSpec((tm, d), lambda i: (i, 0)),
        out_shape=jax.ShapeDtypeStruct((t, d), F32),
        compiler_params=_params(1),
        name="out_proj",
    )(oa, ob, oc, od, w, x2d, g)


HALO = 16


def _mlp_kernel(x_ref, xh_ref, gpre_ref, wg_ref, wu_ref, cwg_ref, cwu_ref, cbg_ref, cbu_ref,
                wd_ref, gpost_ref, o_ref, xn_ref, acc_ref, *, tm, tiles_per_seq):
    i = pl.program_id(0)
    f = pl.program_id(1)

    @pl.when(f == 0)
    def _():
        xn_ref[HALO:, :] = _rms(x_ref[...], gpre_ref[...]).astype(BF16)
        halo = _rms(xh_ref[...], gpre_ref[...])
        seq_start = (i % tiles_per_seq) == 0
        xn_ref[:HALO, :] = jnp.where(seq_start, 0.0, halo).astype(BF16)
        acc_ref[...] = jnp.zeros_like(acc_ref)

    xn = xn_ref[...]

    def conv(u, cw_ref, cb_ref):
        cw = cw_ref[...]
        out = cb_ref[...] + cw[0:1] * u[HALO - 2:HALO - 2 + tm]
        out = out + cw[1:2] * u[HALO - 1:HALO - 1 + tm]
        return out + cw[2:3] * u[HALO:HALO + tm]

    gate = conv(_dot(xn, wg_ref[...]), cwg_ref, cbg_ref)
    up = conv(_dot(xn, wu_ref[...]), cwu_ref, cbu_ref)
    hid = (jax.nn.gelu(gate, approximate=True) * up).astype(BF16)
    acc_ref[...] += _dot(hid, wd_ref[...])

    @pl.when(f == pl.num_programs(1) - 1)
    def _():
        o_ref[...] = x_ref[...] + _rms(acc_ref[...], gpost_ref[...])


def _mlp(x2d, gpre, w_up, conv_w, conv_b, w_down, gpost, seq, tm=512, tf=512):
    t, d = x2d.shape
    ff = w_down.shape[0]
    nf = ff // tf
    hb = tm // HALO
    return pl.pallas_call(
        functools.partial(_mlp_kernel, tm=tm, tiles_per_seq=seq // tm),
        grid=(t // tm, nf),
        in_specs=[pl.BlockSpec((tm, d), lambda i, f: (i, 0)),
                  pl.BlockSpec((HALO, d), lambda i, f: (jnp.maximum(i * hb - 1, 0), 0)),
                  pl.BlockSpec((1, d), lambda i, f: (0, 0)),
                  pl.BlockSpec((d, tf), lambda i, f: (0, f)),
                  pl.BlockSpec((d, tf), lambda i, f: (0, nf + f)),
                  pl.BlockSpec((3, tf), lambda i, f: (0, f)),
                  pl.BlockSpec((3, tf), lambda i, f: (0, nf + f)),
                  pl.BlockSpec((1, tf), lambda i, f: (0, f)),
                  pl.BlockSpec((1, tf), lambda i, f: (0, nf + f)),
                  pl.BlockSpec((tf, d), lambda i, f: (f, 0)),
                  pl.BlockSpec((1, d), lambda i, f: (0, 0))],
        out_specs=pl.BlockSpec((tm, d), lambda i, f: (i, 0)),
        out_shape=jax.ShapeDtypeStruct((t, d), F32),
        scratch_shapes=[pltpu.VMEM((tm + HALO, d), BF16), pltpu.VMEM((tm, d), F32)],
        compiler_params=_params(2),
        name="conv_geglu_mlp",
    )(x2d, x2d, gpre, w_up, w_up, conv_w, conv_w, conv_b, conv_b, w_down, gpost)


def _pad_w_in(w):
    parts = jnp.split(w, np.cumsum(_IN_SPLITS)[:-1].tolist(), axis=-1)
    out = []
    for p in parts:
        extra = (-p.shape[-1]) % LANES
        out.append(jnp.pad(p, ((0, 0), (0, 0), (0, extra))) if extra else p)
    return jnp.concatenate(out, axis=-1)


def _rope_lane_tables(seq, rot_dim, period):
    pos = jnp.arange(seq, dtype=F32)
    inv = ROPE_THETA ** (-jnp.arange(0, rot_dim, 2, dtype=F32) / rot_dim)
    ang = pos[:, None] * inv[None, :]
    cos, sin = jnp.cos(ang), jnp.sin(ang)
    rest = period - rot_dim
    c = jnp.concatenate([cos, cos, jnp.ones((seq, rest), F32)], axis=-1)
    s = jnp.concatenate([-sin, sin, jnp.zeros((seq, rest), F32)], axis=-1)
    reps = LANES // period
    return jnp.tile(c, (1, reps)), jnp.tile(s, (1, reps))


@jax.jit
def _forward(x, norm_mix_pre, norm_mix_post, norm_mlp_pre, norm_mlp_post, w_in, w_out,
             nsa_pe_k, nsa_pe_v, nsa_cmp_k1, nsa_cmp_k2, nsa_cmp_v1, nsa_cmp_v2,
             fox_forget_bias, diff_lambda, diff_norm, mlp_w_up, mlp_conv_w, mlp_conv_b,
             mlp_w_down):
    b, s, d = x.shape
    depth = w_in.shape[0]
    cos_h, sin_h = _rope_lane_tables(s, HEAD_DIM // 4, LANES)
    cos_d, sin_d = _rope_lane_tables(s, DIFF_QK // 4, DIFF_QK)
    w_in_b = _pad_w_in(w_in).astype(BF16)
    w_out_b = w_out.astype(BF16)
    w_up_b = mlp_w_up.astype(BF16)
    w_down_b = mlp_w_down.astype(BF16)
    k1_b, k2_b = nsa_cmp_k1.astype(BF16), nsa_cmp_k2.astype(BF16)
    v1_b, v2_b = nsa_cmp_v1.astype(BF16), nsa_cmp_v2.astype(BF16)
    fox_bias = jnp.pad(fox_forget_bias, ((0, 0), (0, LANES - N_HEADS)))
    row2 = lambda a: a.reshape(1, -1)
    cmp_w = NSA_CMP_STRIDE * HEAD_DIM

    xt = x.reshape(b * s, d)
    for l in range(depth):
        proj = _norm_matmul(xt, row2(norm_mix_pre[l]), w_in_b[l]).reshape(b, s, PROJ_BLOCKS * LANES)

        k2 = proj[:, :, COL_AKC * LANES:(COL_AKC + 1) * LANES].reshape(b, s // NSA_CMP_STRIDE, cmp_w)
        v2 = proj[:, :, COL_AVC * LANES:(COL_AVC + 1) * LANES].reshape(b, s // NSA_CMP_STRIDE, cmp_w)
        kc, vc = _nsa_compress(k2, v2, row2(nsa_pe_k[l]), row2(nsa_pe_v[l]),
                               k1_b[l], k2_b[l], v1_b[l], v2_b[l])
        o_a = _nsa_attention(proj, kc, vc, cos_h, sin_h)

        o_b = _sb_attention(proj)

        cum_col, cum_row = _fox_cum(proj, fox_bias[l:l + 1])
        o_c = _fox_attention(proj, cum_col, cum_row)

        lam_init = 0.8 - 0.6 * math.exp(-0.3 * l)
        dl = jnp.concatenate([
            jnp.pad(diff_lambda[l], ((0, 0), (0, LANES - DIFF_QK))),
            jnp.full((1, LANES), lam_init, F32),
            jnp.full((1, LANES), 1.0 - lam_init, F32),
            jnp.zeros((2, LANES), F32)], axis=0)
        o_d = _diff_attention(proj, cos_d, sin_d, dl, row2(diff_norm[l]))

        flat = lambda o: o.reshape(b * s, GROUP)
        xt = _out_proj(flat(o_a), flat(o_b), flat(o_c), flat(o_d), w_out_b[l], xt,
                       row2(norm_mix_post[l]))
        xt = _mlp(xt, row2(norm_mlp_pre[l]), w_up_b[l], mlp_conv_w[l], row2(mlp_conv_b[l]),
                  w_down_b[l], row2(norm_mlp_post[l]), s)
    return xt.reshape(b, s, d)


def kernel(x, norm_mix_pre, norm_mix_post, norm_mlp_pre, norm_mlp_post, w_in, w_out, nsa_pe_k, nsa_pe_v, nsa_cmp_k1, nsa_cmp_k2, nsa_cmp_v1, nsa_cmp_v2, fox_forget_bias, diff_lambda, diff_norm, mlp_w_up, mlp_conv_w, mlp_conv_b, mlp_w_down):
    return _forward(x, norm_mix_pre, norm_mix_post, norm_mlp_pre, norm_mlp_post, w_in, w_out,
                    nsa_pe_k, nsa_pe_v, nsa_cmp_k1, nsa_cmp_k2, nsa_cmp_v1, nsa_cmp_v2,
                    fox_forget_bias, diff_lambda, diff_norm, mlp_w_up, mlp_conv_w, mlp_conv_b,
                    mlp_w_down)
```

```python
import functools
import math

import jax
import jax.numpy as jnp
import numpy as np
from jax import lax
from jax.experimental import pallas as pl
from jax.experimental.pallas import tpu as pltpu

F32 = jnp.float32
BF16 = jnp.bfloat16

LANES = 128
HEAD_DIM = 128
N_HEADS = 4
GROUP = N_HEADS * HEAD_DIM
NSA_CMP_BLOCK = 32
NSA_CMP_STRIDE = 16
NSA_SLC_BLOCK = 64
NSA_TOPK = 16
NSA_WINDOW = 512
DIFF_QK = HEAD_DIM // 2
ROPE_HALF_HEAD = HEAD_DIM // 4 // 2
ROPE_HALF_DIFF = DIFF_QK // 4 // 2
EPS = 1e-6
NEG_INF = -1e30
FORCE_SCORE = 1e9
ROPE_THETA = 500000.0
SCALE_HEAD = HEAD_DIM ** -0.5
SCALE_DIFF = DIFF_QK ** -0.5

_IN_SPLITS = (GROUP, 128, 128, 128, 128, 128, 128, 3 * N_HEADS,
              GROUP, GROUP, GROUP, GROUP, GROUP, GROUP, N_HEADS, GROUP, GROUP, GROUP)
COL_AQ, COL_AQR, COL_AKS, COL_AKW, COL_AVS, COL_AVW = 0, 4, 8, 9, 10, 11
COL_BQ, COL_BK, COL_BV = 12, 16, 20
COL_CQ, COL_CK, COL_CV = 24, 28, 32
COL_DQ, COL_DK, COL_DV = 36, 40, 44
HEAD_BLOCKS = 48
AUX_KC, AUX_VC, AUX_G, AUX_F = 0, 1, 2, 3
AUX_BLOCKS = 4

PROJ_TN = 256
_TILES_SCALE = (0, 1, 6, 7, 12, 13)
_TILES_ROPE_HEAD = (2, 3, 4)
_TILES_ROPE_DIFF = (18, 19, 20, 21)
_ROPE_SET_STARTS = (4, 18, 20)

VMEM_LIMIT = 56 * 1024 * 1024


def _params(n_axes, vmem=VMEM_LIMIT):
    return pltpu.CompilerParams(dimension_semantics=("arbitrary",) * n_axes,
                                vmem_limit_bytes=vmem)


def _rms(x, g):
    ms = jnp.mean(x * x, axis=-1, keepdims=True)
    return x * lax.rsqrt(ms + EPS) * g


def _dot(a, b):
    return jnp.dot(a, b, preferred_element_type=F32)


def _dot_nt(a, b):
    return lax.dot_general(a, b, (((1,), (1,)), ((), ())), preferred_element_type=F32)


def _split3(x):
    hi = x.astype(BF16)
    r = x - hi.astype(F32)
    mid = r.astype(BF16)
    lo = (r - mid.astype(F32)).astype(BF16)
    return hi, mid, lo


def _dot3(x, m):
    hi, mid, lo = _split3(x)
    return _dot(hi, m) + _dot(mid, m) + _dot(lo, m)


def _dot2(x, m):
    hi = x.astype(BF16)
    lo = (x - hi.astype(F32)).astype(BF16)
    return _dot(hi, m) + _dot(lo, m)


def _dot3_left(m, x):
    hi, mid, lo = _split3(x)
    return _dot(m, hi) + _dot(m, mid) + _dot(m, lo)


def _rope(x, c, s, half):
    width = x.shape[1]
    lane = lax.broadcasted_iota(jnp.int32, (1, width), 1)
    first = (lane % (2 * half)) < half
    xs = jnp.where(first, pltpu.roll(x, width - half, 1), pltpu.roll(x, half, 1))
    return x * c + xs * s


def _lane_fold(x, op):
    out = x[:, :LANES]
    for b in range(1, x.shape[1] // LANES):
        out = op(out, x[:, b * LANES:(b + 1) * LANES])
    return out


def _row_max(s):
    return jnp.max(_lane_fold(s, jnp.maximum), axis=1, keepdims=True)


def _flash_step(s, vt, m, l, acc):
    m_new = jnp.maximum(m, _row_max(s))
    alpha = jnp.exp(m - m_new)
    p = jnp.exp(s - m_new)
    l = alpha * l + _lane_fold(p, jnp.add)
    acc = alpha * acc + _dot(p.astype(BF16), vt)
    return m_new, l, acc


def _flash_init(rows):
    return (jnp.full((rows, 1), NEG_INF, F32), jnp.zeros((rows, LANES), F32),
            jnp.zeros((rows, HEAD_DIM), F32))


def _flash_finish(state):
    _, l, acc = state
    return acc * (1.0 / jnp.sum(l, axis=1, keepdims=True))


def _flash_sweep(lo, n_full, n_diag, n_chain, score_fn, value_fn, mask_fn, rows):
    chains = range(n_chain)

    def body(j, carry):
        out = []
        for c in chains:
            s_next = score_fn(c, j + 1)
            out.append((s_next,) + _flash_step(carry[c][0], value_fn(c, j), *carry[c][1:]))
        return tuple(out)

    carry = lax.fori_loop(lo, n_full, body,
                          tuple((score_fn(c, lo),) + _flash_init(rows) for c in chains))
    outs = []
    for c in chains:
        s, state = carry[c][0], carry[c][1:]
        for d in range(n_diag):
            if d > 0:
                s = score_fn(c, n_full + d)
            state = _flash_step(mask_fn(n_full + d, s), value_fn(c, n_full + d), *state)
        outs.append(_flash_finish(state))
    return outs


def _flash_scratch(n_chain, rows, tk):
    stat = pltpu.VMEM((n_chain, rows, LANES), F32)
    return [pltpu.VMEM((n_chain, rows, tk), F32), pltpu.VMEM((n_chain, rows, tk), F32),
            pltpu.VMEM((n_chain, rows, tk), BF16), stat, stat, stat, stat]


def _flash_sweep_ref(lo, n_full, n_chain, score_fn, value_fn, mask_fn, rows, tk, bufs, rc=16):
    s_even, s_odd, p_buf, m_ref, l_ref, a_ref, acc_ref = bufs
    chains = range(n_chain)
    reps = tk // LANES
    m_ref[...] = jnp.full(m_ref.shape, NEG_INF, F32)
    l_ref[...] = jnp.zeros(l_ref.shape, F32)
    acc_ref[...] = jnp.zeros(acc_ref.shape, F32)

    def scores(s_buf, j):
        for c in chains:
            s_buf[c] = score_fn(c, j)

    def consume(s_buf, j, masked):
        for c in chains:
            for r0 in range(0, rows, rc):
                rr = slice(r0, r0 + rc)
                s = s_buf[c, rr, :]
                if masked:
                    s = mask_fn(j, s, r0)
                m_old = m_ref[c, rr, :]
                m_new = jnp.maximum(m_old, _row_max(s))
                alpha = jnp.exp(m_old - m_new)
                p = jnp.exp(s - jnp.concatenate([m_new] * reps, axis=1))
                l_ref[c, rr, :] = alpha * l_ref[c, rr, :] + _lane_fold(p, jnp.add)
                m_ref[c, rr, :] = m_new
                a_ref[c, rr, :] = alpha
                p_buf[c, rr, :] = p.astype(BF16)
            acc_ref[c] = a_ref[c] * acc_ref[c] + _dot(p_buf[c], value_fn(c, j))

    scores(s_even, lo)

    def pair(t, carry):
        j = lo + 2 * t
        scores(s_odd, j + 1)
        consume(s_even, j, False)
        scores(s_even, j + 2)
        consume(s_odd, j + 1, False)
        return carry

    n_pair = (n_full - lo) // 2
    lax.fori_loop(0, n_pair, pair, 0)
    j = lo + 2 * n_pair
    last_full = j < n_full

    @pl.when(last_full)
    def _():
        scores(s_odd, j + 1)

    consume(s_even, j, True)

    @pl.when(last_full)
    def _():
        consume(s_odd, j + 1, True)

    return [acc_ref[c] * (1.0 / jnp.sum(l_ref[c], axis=1, keepdims=True)) for c in chains]


def _head(ref, h, rows=None):
    cols = slice(h * HEAD_DIM, (h + 1) * HEAD_DIM)
    return ref[:, cols] if rows is None else ref[rows, cols]


def _key_rows(j, tk):
    return pl.ds(pl.multiple_of(j * tk, tk), tk)


def _tile_in(j, tiles):
    hit = j == tiles[0]
    for t in tiles[1:]:
        hit = hit | (j == t)
    return hit


def _head_proj_kernel(x_ref, g_ref, w_ref, c_ref, s_ref, o_ref, xn_ref):
    j = pl.program_id(1)

    @pl.when(j == 0)
    def _():
        xn_ref[...] = _rms(x_ref[...], g_ref[...]).astype(BF16)

    acc = _dot(xn_ref[...], w_ref[...])
    is_scale = _tile_in(j, _TILES_SCALE)
    is_rope_h = _tile_in(j, _TILES_ROPE_HEAD)
    is_rope_d = _tile_in(j, _TILES_ROPE_DIFF)

    @pl.when(is_scale)
    def _():
        o_ref[...] = (acc * SCALE_HEAD).astype(BF16)

    def rope_tile(half):
        reps = PROJ_TN // LANES
        c = jnp.concatenate([c_ref[...]] * reps, axis=1)
        s = jnp.concatenate([s_ref[...]] * reps, axis=1)
        o_ref[...] = _rope(acc, c, s, half).astype(BF16)

    pl.when(is_rope_h)(lambda: rope_tile(ROPE_HALF_HEAD))
    pl.when(is_rope_d)(lambda: rope_tile(ROPE_HALF_DIFF))

    @pl.when(jnp.logical_not(is_scale | is_rope_h | is_rope_d))
    def _():
        o_ref[...] = acc.astype(BF16)


def _rope_set(j):
    idx = 0
    for start in _ROPE_SET_STARTS:
        idx = idx + (j >= start).astype(jnp.int32)
    return idx


def _head_proj(x2d, g, w, cos_sets, sin_sets, seq, tm=512):
    t, d = x2d.shape
    n = w.shape[1]
    per_seq = seq // tm
    table = pl.BlockSpec((None, tm, LANES), lambda i, j: (_rope_set(j), i % per_seq, 0))
    return pl.pallas_call(
        _head_proj_kernel,
        grid=(t // tm, n // PROJ_TN),
        in_specs=[pl.BlockSpec((tm, d), lambda i, j: (i, 0)),
                  pl.BlockSpec((1, d), lambda i, j: (0, 0)),
                  pl.BlockSpec((d, PROJ_TN), lambda i, j: (0, j)),
                  table, table],
        out_specs=pl.BlockSpec((tm, PROJ_TN), lambda i, j: (i, j)),
        out_shape=jax.ShapeDtypeStruct((t, n), BF16),
        scratch_shapes=[pltpu.VMEM((tm, d), BF16)],
        compiler_params=_params(2),
        name="head_proj",
    )(x2d, g, w, cos_sets, sin_sets)


def _norm_matmul_kernel(x_ref, g_ref, w_ref, o_ref):
    o_ref[...] = _dot(_rms(x_ref[...], g_ref[...]).astype(BF16), w_ref[...])


def _aux_proj(x2d, g, w, tm=512):
    t, d = x2d.shape
    n = w.shape[1]
    return pl.pallas_call(
        _norm_matmul_kernel,
        grid=(t // tm,),
        in_specs=[pl.BlockSpec((tm, d), lambda i: (i, 0)),
                  pl.BlockSpec((1, d), lambda i: (0, 0)),
                  pl.BlockSpec((d, n), lambda i: (0, 0))],
        out_specs=pl.BlockSpec((tm, n), lambda i: (i, 0)),
        out_shape=jax.ShapeDtypeStruct((t, n), F32),
        compiler_params=_params(1),
        name="aux_proj",
    )(x2d, g, w)


def _sb_kernel(q_ref, k_ref, v_ref, o_ref, *, tq, tk, hp):
    assert tq == tk
    i = pl.program_id(2)
    nblk = tk // LANES
    row = i * tq + lax.broadcasted_iota(jnp.int32, (tq, 1), 0)
    col = i * tq + lax.broadcasted_iota(jnp.int32, (1, tk), 1)
    past = col < row
    jr = lax.broadcasted_iota(jnp.int32, (tk, tk), 0)
    jc = lax.broadcasted_iota(jnp.int32, (tk, tk), 1)
    u = jnp.where((jr // LANES == jc // LANES) & (jr > jc), 1.0, 0.0).astype(BF16)
    heads = range(hp)
    qs = [_head(q_ref, h) for h in heads]

    def logits(h, j, masked):
        z = _dot_nt(qs[h], _head(k_ref, h, _key_rows(j, tk)))
        ls = jnp.minimum(z, 0.0) - jnp.log1p(jnp.exp(-jnp.abs(z)))
        lk = ls - z
        if masked:
            lk = jnp.where(past, lk, 0.0)
        suf = _dot2(lk, u)
        tot = jnp.zeros((tq, 1), F32)
        parts = [None] * nblk
        for b in reversed(range(nblk)):
            blk = slice(b * LANES, (b + 1) * LANES)
            parts[b] = ls[:, blk] + suf[:, blk] + tot
            tot = tot + jnp.sum(lk[:, blk], axis=1, keepdims=True)
        return jnp.concatenate(parts, axis=1), tot

    def weigh(h, j, pre, tot, c, acc, masked):
        e = jnp.exp(pre + c)
        if masked:
            e = jnp.where(past, e, 0.0)
        return c + tot, acc + _dot(e.astype(BF16), _head(v_ref, h, _key_rows(j, tk)))

    state = []
    for h in heads:
        pre, tot = logits(h, i, True)
        c, acc = weigh(h, i, pre, tot, jnp.zeros((tq, 1), F32),
                       jnp.zeros((tq, HEAD_DIM), F32), True)
        state.append(logits(h, jnp.maximum(i - 1, 0), False) + (c, acc))

    def body(jj, carry):
        j = i - 1 - jj
        out = []
        for h in heads:
            pre, tot, c, acc = carry[h]
            nxt = logits(h, jnp.maximum(j - 1, 0), False)
            out.append(nxt + weigh(h, j, pre, tot, c, acc, False))
        return tuple(out)

    carry = lax.fori_loop(0, i, body, tuple(state))
    for h in heads:
        o_ref[:, h * HEAD_DIM:(h + 1) * HEAD_DIM] = carry[h][3].astype(BF16)


def _sb_attention(proj, tq=256, tk=256, hp=4):
    b, s, _ = proj.shape
    w = hp * HEAD_DIM
    return pl.pallas_call(
        functools.partial(_sb_kernel, tq=tq, tk=tk, hp=hp),
        grid=(b, N_HEADS // hp, s // tq),
        in_specs=[pl.BlockSpec((None, tq, w), lambda bi, hg, i: (bi, i, COL_BQ // hp + hg)),
                  pl.BlockSpec((None, s, w), lambda bi, hg, i: (bi, 0, COL_BK // hp + hg)),
                  pl.BlockSpec((None, s, w), lambda bi, hg, i: (bi, 0, COL_BV // hp + hg))],
        out_specs=pl.BlockSpec((None, tq, w), lambda bi, hg, i: (bi, i, hg)),
        out_shape=jax.ShapeDtypeStruct((b, s, GROUP), BF16),
        compiler_params=_params(3),
        name="sb_attention",
    )(proj, proj, proj)


def _fox_cum_kernel(cf_ref, bias_ref, col_ref, row_ref, *, chunk):
    s = cf_ref.shape[0]
    lf = jax.nn.log_sigmoid(cf_ref[...] + bias_ref[...])
    rr = lax.broadcasted_iota(jnp.int32, (chunk, chunk), 0)
    cc = lax.broadcasted_iota(jnp.int32, (chunk, chunk), 1)
    tri = jnp.where(rr >= cc, 1.0, 0.0).astype(BF16)
    carry = jnp.zeros((1, LANES), F32)
    for c in range(s // chunk):
        pre = _dot3_left(tri, lf[c * chunk:(c + 1) * chunk]) + carry
        col_ref[c * chunk:(c + 1) * chunk, :] = pre
        carry = pre[chunk - 1:chunk, :]
    row_ref[...] = jnp.transpose(col_ref[...])[:8, :]


def _fox_cum(aux, bias_row, chunk=256):
    b, s, _ = aux.shape
    return pl.pallas_call(
        functools.partial(_fox_cum_kernel, chunk=chunk),
        grid=(b,),
        in_specs=[pl.BlockSpec((None, s, LANES), lambda bi: (bi, 0, AUX_F)),
                  pl.BlockSpec((1, LANES), lambda bi: (0, 0))],
        out_specs=[pl.BlockSpec((None, s, LANES), lambda bi: (bi, 0, 0)),
                   pl.BlockSpec((None, 8, s), lambda bi: (bi, 0, 0))],
        out_shape=[jax.ShapeDtypeStruct((b, s, LANES), F32),
                   jax.ShapeDtypeStruct((b, 8, s), F32)],
        compiler_params=_params(1),
        name="fox_cum",
    )(aux, bias_row)


def _fox_kernel(q_ref, k_ref, v_ref, cq_ref, ck_ref, o_ref, *bufs, tq, tk, hp):
    assert tk >= tq
    hg = pl.program_id(1)
    i = pl.program_id(2)
    lane = lax.broadcasted_iota(jnp.int32, (1, LANES), 1)
    cum_q = cq_ref[...]
    qs = [_head(q_ref, h) for h in range(hp)]
    cqs = [jnp.sum(jnp.where(lane == hg * hp + h, cum_q, 0.0), axis=1, keepdims=True)
           for h in range(hp)]

    def score(h, j):
        return (_dot_nt(qs[h], _head(k_ref, h, _key_rows(j, tk))) + cqs[h]
                - ck_ref[h, pl.ds(j, 1), :])

    def value(h, j):
        return _head(v_ref, h, _key_rows(j, tk))

    def causal(j, s, r0):
        row = i * tq + r0 + lax.broadcasted_iota(jnp.int32, (s.shape[0], 1), 0)
        col = j * tk + lax.broadcasted_iota(jnp.int32, (1, tk), 1)
        return jnp.where(col <= row, s, NEG_INF)

    outs = _flash_sweep_ref(0, (i * tq) // tk, hp, score, value, causal, tq, tk, bufs)
    for h in range(hp):
        o_ref[:, h * HEAD_DIM:(h + 1) * HEAD_DIM] = outs[h].astype(BF16)


def _fox_attention(proj, cum_col, cum_row, tq=256, tk=512, hp=4):
    b, s, _ = proj.shape
    w = hp * HEAD_DIM
    ck = cum_row.reshape(b, 8, s // tk, tk)
    return pl.pallas_call(
        functools.partial(_fox_kernel, tq=tq, tk=tk, hp=hp),
        grid=(b, N_HEADS // hp, s // tq),
        in_specs=[pl.BlockSpec((None, tq, w), lambda bi, hg, i: (bi, i, COL_CQ // hp + hg)),
                  pl.BlockSpec((None, s, w), lambda bi, hg, i: (bi, 0, COL_CK // hp + hg)),
                  pl.BlockSpec((None, s, w), lambda bi, hg, i: (bi, 0, COL_CV // hp + hg)),
                  pl.BlockSpec((None, tq, LANES), lambda bi, hg, i: (bi, i, 0)),
                  pl.BlockSpec((None, hp, s // tk, tk), lambda bi, hg, i: (bi, hg, 0, 0))],
        out_specs=pl.BlockSpec((None, tq, w), lambda bi, hg, i: (bi, i, hg)),
        out_shape=jax.ShapeDtypeStruct((b, s, GROUP), BF16),
        scratch_shapes=_flash_scratch(hp, tq, tk),
        compiler_params=_params(3),
        name="fox_attention",
    )(proj, proj, proj, cum_col, ck)


def _diff_kernel(q_ref, k_ref, v_ref, dl_ref, gn_ref, o_ref, *bufs, tq, tk, hp):
    assert tk >= tq
    i = pl.program_id(2)
    lane = lax.broadcasted_iota(jnp.int32, (1, LANES), 1)
    zero = jnp.zeros((), BF16)
    qs = []
    for h in range(hp):
        qh = _head(q_ref, h)
        qs.append(jnp.concatenate([jnp.where(lane < DIFF_QK, qh, zero),
                                   jnp.where(lane >= DIFF_QK, qh, zero)], axis=0))

    def score(h, j):
        return _dot_nt(qs[h], _head(k_ref, h, _key_rows(j, tk)))

    def value(h, j):
        return _head(v_ref, h, _key_rows(j, tk))

    def causal(j, s, r0):
        row = i * tq + r0 % tq + lax.broadcasted_iota(jnp.int32, (s.shape[0], 1), 0)
        col = j * tk + lax.broadcasted_iota(jnp.int32, (1, tk), 1)
        return jnp.where(col <= row, s, NEG_INF)

    outs = _flash_sweep_ref(0, (i * tq) // tk, hp, score, value, causal, 2 * tq, tk, bufs,
                            rc=32)
    dl = dl_ref[...]
    lam = (jnp.exp(jnp.sum(dl[0:1] * dl[1:2], axis=1, keepdims=True))
           - jnp.exp(jnp.sum(dl[2:3] * dl[3:4], axis=1, keepdims=True)) + dl[4:5, 0:1])
    for h in range(hp):
        od = outs[h][:tq] - lam * outs[h][tq:]
        o_ref[:, h * HEAD_DIM:(h + 1) * HEAD_DIM] = (
            _rms(od, gn_ref[...]) * dl[5:6, 0:1]).astype(BF16)


def _diff_attention(proj, dl, gn, tq=256, tk=512, hp=4):
    b, s, _ = proj.shape
    w = hp * HEAD_DIM
    return pl.pallas_call(
        functools.partial(_diff_kernel, tq=tq, tk=tk, hp=hp),
        grid=(b, N_HEADS // hp, s // tq),
        in_specs=[pl.BlockSpec((None, tq, w), lambda bi, hg, i: (bi, i, COL_DQ // hp + hg)),
                  pl.BlockSpec((None, s, w), lambda bi, hg, i: (bi, 0, COL_DK // hp + hg)),
                  pl.BlockSpec((None, s, w), lambda bi, hg, i: (bi, 0, COL_DV // hp + hg)),
                  pl.BlockSpec((8, LANES), lambda bi, hg, i: (0, 0)),
                  pl.BlockSpec((1, LANES), lambda bi, hg, i: (0, 0))],
        out_specs=pl.BlockSpec((None, tq, w), lambda bi, hg, i: (bi, i, hg)),
        out_shape=jax.ShapeDtypeStruct((b, s, GROUP), BF16),
        scratch_shapes=_flash_scratch(hp, 2 * tq, tk),
        compiler_params=_params(3),
        name="diff_attention",
    )(proj, proj, proj, dl, gn)


def _cmp_kernel(k2_ref, v2_ref, pek_ref, pev_ref, wk1_ref, wk2_ref, wv1_ref, wv2_ref,
                kc_ref, vc_ref):
    nb = k2_ref.shape[0]
    w = NSA_CMP_STRIDE * HEAD_DIM

    def compress(x2, pe, w1_ref, w2_ref):
        a = _dot((x2 + pe[:, :w]).astype(BF16), w1_ref[:w, :])
        bb = _dot((x2 + pe[:, w:]).astype(BF16), w1_ref[w:, :])
        hid = jax.nn.gelu(a + pltpu.roll(bb, nb - 1, 0), approximate=True)
        return _dot(hid.astype(BF16), w2_ref[...])

    kc_ref[...] = compress(k2_ref[...], pek_ref[...], wk1_ref, wk2_ref).astype(BF16)
    vc_ref[...] = compress(v2_ref[...], pev_ref[...], wv1_ref, wv2_ref).astype(BF16)


def _nsa_compress(k2, v2, pek, pev, wk1, wk2, wv1, wv2):
    b, nb, w = k2.shape
    hid = wk1.shape[1]
    full = lambda *shape: pl.BlockSpec(shape, lambda bi: (0,) * len(shape))
    return pl.pallas_call(
        _cmp_kernel,
        grid=(b,),
        in_specs=[pl.BlockSpec((None, nb, w), lambda bi: (bi, 0, 0)),
                  pl.BlockSpec((None, nb, w), lambda bi: (bi, 0, 0)),
                  full(1, 2 * w), full(1, 2 * w),
                  full(2 * w, hid), full(hid, HEAD_DIM), full(2 * w, hid), full(hid, HEAD_DIM)],
        out_specs=[pl.BlockSpec((None, nb, HEAD_DIM), lambda bi: (bi, 0, 0)),
                   pl.BlockSpec((None, nb, HEAD_DIM), lambda bi: (bi, 0, 0))],
        out_shape=[jax.ShapeDtypeStruct((b, nb, HEAD_DIM), BF16),
                   jax.ShapeDtypeStruct((b, nb, HEAD_DIM), BF16)],
        compiler_params=_params(1),
        name="nsa_compress",
    )(k2, v2, pek, pev, wk1, wk2, wv1, wv2)


def _nsa_kernel(q_ref, qr_ref, g_ref, kc_ref, vc_ref, ks_ref, kw_ref, vs_ref, vw_ref, o_ref,
                *bufs, tq, tk, n_top):
    assert tk >= tq
    i = pl.program_id(1)
    nb = kc_ref.shape[0]
    n_sel = NSA_SLC_BLOCK
    hq = N_HEADS * tq

    q_raw = jnp.concatenate([_head(q_ref, h) for h in range(N_HEADS)], axis=0)
    q_rot = jnp.concatenate([_head(qr_ref, h) for h in range(N_HEADS)], axis=0)
    row = i * tq + lax.broadcasted_iota(jnp.int32, (tq, 1), 0)
    rep = lambda a: jnp.concatenate([a] * N_HEADS, axis=0)

    cidx = lax.broadcasted_iota(jnp.int32, (1, nb), 1)
    c_valid = rep((cidx * NSA_CMP_STRIDE + (NSA_CMP_BLOCK - 1)) <= row)
    sc = jnp.where(c_valid, _dot_nt(q_raw, kc_ref[...]), NEG_INF)
    e = jnp.exp(sc - jnp.max(sc, axis=1, keepdims=True))
    p = jnp.where(c_valid, e * (1.0 / jnp.sum(e, axis=1, keepdims=True)), 0.0)
    o_cmp = _dot(p.astype(BF16), vc_ref[...])
    p_sum = p[0:tq]
    for h in range(1, N_HEADS):
        p_sum = p_sum + p[h * tq:(h + 1) * tq]
    oc = lax.broadcasted_iota(jnp.int32, (nb, LANES), 0)
    oj = lax.broadcasted_iota(jnp.int32, (nb, LANES), 1)
    per = NSA_SLC_BLOCK // NSA_CMP_STRIDE
    overlap = (jnp.where(oc // per == oj, 1.0, 0.0)
               + jnp.where((oc + 1) // per == oj, 1.0, 0.0)).astype(BF16)
    imp = _dot3(p_sum, overlap)

    imp_t = jnp.transpose(imp)[:n_sel, :]
    jb = lax.broadcasted_iota(jnp.int32, (n_sel, 1), 0)
    cur = (i * tq + lax.broadcasted_iota(jnp.int32, (1, tq), 1)) // NSA_SLC_BLOCK
    sel_valid = jb <= cur
    forced = sel_valid & ((jb == 0) | (jb >= cur - 1))
    score = jnp.where(forced, FORCE_SCORE, jnp.where(sel_valid, imp_t, NEG_INF))
    cnt = jnp.zeros((n_sel, tq), F32)
    for ii in range(n_sel):
        ri = score[ii:ii + 1, :]
        beats = (ri > score) | ((ri == score) & (jb > ii))
        cnt = cnt + jnp.where(beats, 1.0, 0.0)
    sel_t = jnp.where((cnt < n_top) & sel_valid, 1.0, 0.0)
    sel = jnp.transpose(jnp.concatenate([sel_t, jnp.zeros_like(sel_t)], axis=0)).astype(BF16)

    er = lax.broadcasted_iota(jnp.int32, (LANES, tk), 0)
    ec = lax.broadcasted_iota(jnp.int32, (LANES, tk), 1)

    def causal(j, s, r0):
        rows = i * tq + r0 % tq + lax.broadcasted_iota(jnp.int32, (s.shape[0], 1), 0)
        col = j * tk + lax.broadcasted_iota(jnp.int32, (1, tk), 1)
        return jnp.where(col <= rows, s, NEG_INF)

    def slc_score(_, j):
        expand = jnp.where(er == (j * tk + ec) // NSA_SLC_BLOCK, 1.0, 0.0).astype(BF16)
        ok = _dot(sel, expand) > 0.5
        return jnp.where(rep(ok), _dot_nt(q_rot, ks_ref[_key_rows(j, tk), :]), NEG_INF)

    n_full = (i * tq) // tk
    o_slc, = _flash_sweep_ref(0, n_full, 1, slc_score,
                              lambda _, j: vs_ref[_key_rows(j, tk), :], causal, hq, tk, bufs,
                              rc=32)

    def win_score(_, j):
        col = j * tk + lax.broadcasted_iota(jnp.int32, (1, tk), 1)
        return jnp.where(rep(col > row - NSA_WINDOW),
                         _dot_nt(q_rot, kw_ref[_key_rows(j, tk), :]), NEG_INF)

    j_lo = (jnp.maximum(i * tq - NSA_WINDOW, 0) // (2 * tk)) * 2
    o_win, = _flash_sweep_ref(j_lo, n_full, 1, win_score,
                              lambda _, j: vw_ref[_key_rows(j, tk), :], causal, hq, tk, bufs,
                              rc=32)

    g = jax.nn.sigmoid(g_ref[...])
    for h in range(N_HEADS):
        rows = slice(h * tq, (h + 1) * tq)
        o_ref[:, h * HEAD_DIM:(h + 1) * HEAD_DIM] = (
            g[:, 3 * h:3 * h + 1] * o_cmp[rows] + g[:, 3 * h + 1:3 * h + 2] * o_slc[rows]
            + g[:, 3 * h + 2:3 * h + 3] * o_win[rows]).astype(BF16)


def _nsa_attention(proj, aux, kc, vc, tq=256, tk=512):
    b, s, _ = proj.shape
    nb = kc.shape[1]
    n_top = min(NSA_TOPK, s // NSA_SLC_BLOCK)
    kv = lambda col: pl.BlockSpec((None, s, HEAD_DIM), lambda bi, i: (bi, 0, col))
    return pl.pallas_call(
        functools.partial(_nsa_kernel, tq=tq, tk=tk, n_top=n_top),
        grid=(b, s // tq),
        in_specs=[pl.BlockSpec((None, tq, GROUP), lambda bi, i: (bi, i, COL_AQ // N_HEADS)),
                  pl.BlockSpec((None, tq, GROUP), lambda bi, i: (bi, i, COL_AQR // N_HEADS)),
                  pl.BlockSpec((None, tq, LANES), lambda bi, i: (bi, i, AUX_G)),
                  pl.BlockSpec((None, nb, HEAD_DIM), lambda bi, i: (bi, 0, 0)),
                  pl.BlockSpec((None, nb, HEAD_DIM), lambda bi, i: (bi, 0, 0)),
                  kv(COL_AKS), kv(COL_AKW), kv(COL_AVS), kv(COL_AVW)],
        out_specs=pl.BlockSpec((None, tq, GROUP), lambda bi, i: (bi, i, 0)),
        out_shape=jax.ShapeDtypeStruct((b, s, GROUP), BF16),
        scratch_shapes=_flash_scratch(1, N_HEADS * tq, tk),
        compiler_params=_params(2),
        name="nsa_attention",
    )(proj, proj, aux, kc, vc, proj, proj, proj, proj)


def _out_kernel(oa_ref, ob_ref, oc_ref, od_ref, w_ref, x_ref, g_ref, o_ref):
    acc = _dot(oa_ref[...], w_ref[0:GROUP, :])
    acc = acc + _dot(ob_ref[...], w_ref[GROUP:2 * GROUP, :])
    acc = acc + _dot(oc_ref[...], w_ref[2 * GROUP:3 * GROUP, :])
    acc = acc + _dot(od_ref[...], w_ref[3 * GROUP:4 * GROUP, :])
    o_ref[...] = x_ref[...] + _rms(acc, g_ref[...])


def _out_proj(oa, ob, oc, od, w, x2d, g, tm=512):
    t, d = x2d.shape
    mix = pl.BlockSpec((tm, GROUP), lambda i: (i, 0))
    return pl.pallas_call(
        _out_kernel,
        grid=(t // tm,),
        in_specs=[mix, mix, mix, mix,
                  pl.BlockSpec((4 * GROUP, d), lambda i: (0, 0)),
                  pl.BlockSpec((tm, d), lambda i: (i, 0)),
                  pl.BlockSpec((1, d), lambda i: (0, 0))],
        out_specs=pl.BlockSpec((tm, d), lambda i: (i, 0)),
        out_shape=jax.ShapeDtypeStruct((t, d), F32),
        compiler_params=_params(1),
        name="out_proj",
    )(oa, ob, oc, od, w, x2d, g)


HALO = 16


def _mlp_kernel(x_ref, xh_ref, gpre_ref, wg_ref, wu_ref, cwg_ref, cwu_ref, cbg_ref, cbu_ref,
                wd_ref, gpost_ref, o_ref, xn_ref, acc_ref, *, tm, tiles_per_seq):
    i = pl.program_id(0)
    f = pl.program_id(1)

    @pl.when(f == 0)
    def _():
        xn_ref[HALO:, :] = _rms(x_ref[...], gpre_ref[...]).astype(BF16)
        halo = _rms(xh_ref[...], gpre_ref[...])
        seq_start = (i % tiles_per_seq) == 0
        xn_ref[:HALO, :] = jnp.where(seq_start, 0.0, halo).astype(BF16)
        acc_ref[...] = jnp.zeros_like(acc_ref)

    xn = xn_ref[...]

    def conv(u, cw_ref, cb_ref):
        cw = cw_ref[...]
        out = cb_ref[...] + cw[0:1] * u[HALO - 2:HALO - 2 + tm]
        out = out + cw[1:2] * u[HALO - 1:HALO - 1 + tm]
        return out + cw[2:3] * u[HALO:HALO + tm]

    gate = conv(_dot(xn, wg_ref[...]), cwg_ref, cbg_ref)
    up = conv(_dot(xn, wu_ref[...]), cwu_ref, cbu_ref)
    hid = (jax.nn.gelu(gate, approximate=True) * up).astype(BF16)
    acc_ref[...] += _dot(hid, wd_ref[...])

    @pl.when(f == pl.num_programs(1) - 1)
    def _():
        o_ref[...] = x_ref[...] + _rms(acc_ref[...], gpost_ref[...])


def _mlp(x2d, gpre, w_up, conv_w, conv_b, w_down, gpost, seq, tm=512, tf=512):
    t, d = x2d.shape
    ff = w_down.shape[0]
    nf = ff // tf
    hb = tm // HALO
    return pl.pallas_call(
        functools.partial(_mlp_kernel, tm=tm, tiles_per_seq=seq // tm),
        grid=(t // tm, nf),
        in_specs=[pl.BlockSpec((tm, d), lambda i, f: (i, 0)),
                  pl.BlockSpec((HALO, d), lambda i, f: (jnp.maximum(i * hb - 1, 0), 0)),
                  pl.BlockSpec((1, d), lambda i, f: (0, 0)),
                  pl.BlockSpec((d, tf), lambda i, f: (0, f)),
                  pl.BlockSpec((d, tf), lambda i, f: (0, nf + f)),
                  pl.BlockSpec((3, tf), lambda i, f: (0, f)),
                  pl.BlockSpec((3, tf), lambda i, f: (0, nf + f)),
                  pl.BlockSpec((1, tf), lambda i, f: (0, f)),
                  pl.BlockSpec((1, tf), lambda i, f: (0, nf + f)),
                  pl.BlockSpec((tf, d), lambda i, f: (f, 0)),
                  pl.BlockSpec((1, d), lambda i, f: (0, 0))],
        out_specs=pl.BlockSpec((tm, d), lambda i, f: (i, 0)),
        out_shape=jax.ShapeDtypeStruct((t, d), F32),
        scratch_shapes=[pltpu.VMEM((tm + HALO, d), BF16), pltpu.VMEM((tm, d), F32)],
        compiler_params=_params(2),
        name="conv_geglu_mlp",
    )(x2d, x2d, gpre, w_up, w_up, conv_w, conv_w, conv_b, conv_b, w_down, gpost)


def _split_w_in(w):
    (a_q, a_kc, a_vc, a_ks, a_vs, a_kw, a_vw, a_g, b_q, b_k, b_v,
     c_q, c_k, c_v, c_f, d_q, d_k, d_v) = jnp.split(w, np.cumsum(_IN_SPLITS)[:-1].tolist(), axis=-1)
    pad = lambda p: jnp.pad(p, ((0, 0), (0, 0), (0, LANES - p.shape[-1])))
    heads = jnp.concatenate([a_q, a_q, a_ks, a_kw, a_vs, a_vw, b_q, b_k, b_v,
                             c_q, c_k, c_v, d_q, d_k, d_v], axis=-1)
    aux = jnp.concatenate([a_kc, a_vc, pad(a_g), pad(c_f)], axis=-1)
    return heads.astype(BF16), aux.astype(BF16)


def _rope_lane_tables(seq, rot_dim, period):
    pos = jnp.arange(seq, dtype=F32)
    inv = ROPE_THETA ** (-jnp.arange(0, rot_dim, 2, dtype=F32) / rot_dim)
    ang = pos[:, None] * inv[None, :]
    cos, sin = jnp.cos(ang), jnp.sin(ang)
    rest = period - rot_dim
    c = jnp.concatenate([cos, cos, jnp.ones((seq, rest), F32)], axis=-1)
    s = jnp.concatenate([-sin, sin, jnp.zeros((seq, rest), F32)], axis=-1)
    reps = LANES // period
    return jnp.tile(c, (1, reps)), jnp.tile(s, (1, reps))


@jax.jit
def _forward(x, norm_mix_pre, norm_mix_post, norm_mlp_pre, norm_mlp_post, w_in, w_out,
             nsa_pe_k, nsa_pe_v, nsa_cmp_k1, nsa_cmp_k2, nsa_cmp_v1, nsa_cmp_v2,
             fox_forget_bias, diff_lambda, diff_norm, mlp_w_up, mlp_conv_w, mlp_conv_b,
             mlp_w_down):
    b, s, d = x.shape
    depth = w_in.shape[0]
    cos_h, sin_h = _rope_lane_tables(s, HEAD_DIM // 4, LANES)
    cos_d, sin_d = _rope_lane_tables(s, DIFF_QK // 4, DIFF_QK)
    cos_sets = jnp.stack([cos_h * SCALE_HEAD, cos_h, cos_d * SCALE_DIFF, cos_d])
    sin_sets = jnp.stack([sin_h * SCALE_HEAD, sin_h, sin_d * SCALE_DIFF, sin_d])
    w_heads, w_aux = _split_w_in(w_in)
    w_out_b = w_out.astype(BF16)
    w_up_b = mlp_w_up.astype(BF16)
    w_down_b = mlp_w_down.astype(BF16)
    k1_b, k2_b = nsa_cmp_k1.astype(BF16), nsa_cmp_k2.astype(BF16)
    v1_b, v2_b = nsa_cmp_v1.astype(BF16), nsa_cmp_v2.astype(BF16)
    fox_bias = jnp.pad(fox_forget_bias, ((0, 0), (0, LANES - N_HEADS)))
    row2 = lambda a: a.reshape(1, -1)
    cmp_w = NSA_CMP_STRIDE * HEAD_DIM

    xt = x.reshape(b * s, d)
    for l in range(depth):
        g_pre = row2(norm_mix_pre[l])
        proj = _head_proj(xt, g_pre, w_heads[l], cos_sets, sin_sets, s)
        proj = proj.reshape(b, s, HEAD_BLOCKS * LANES)
        aux = _aux_proj(xt, g_pre, w_aux[l]).reshape(b, s, AUX_BLOCKS * LANES)

        k2 = aux[:, :, AUX_KC * LANES:(AUX_KC + 1) * LANES].reshape(b, s // NSA_CMP_STRIDE, cmp_w)
        v2 = aux[:, :, AUX_VC * LANES:(AUX_VC + 1) * LANES].reshape(b, s // NSA_CMP_STRIDE, cmp_w)
        kc, vc = _nsa_compress(k2, v2, row2(nsa_pe_k[l]), row2(nsa_pe_v[l]),
                               k1_b[l], k2_b[l], v1_b[l], v2_b[l])
        o_a = _nsa_attention(proj, aux, kc, vc)

        o_b = _sb_attention(proj)

        cum_col, cum_row = _fox_cum(aux, fox_bias[l:l + 1])
        o_c = _fox_attention(proj, cum_col, cum_row)

        lam_init = 0.8 - 0.6 * math.exp(-0.3 * l)
        dl = jnp.concatenate([
            jnp.pad(diff_lambda[l], ((0, 0), (0, LANES - DIFF_QK))),
            jnp.full((1, LANES), lam_init, F32),
            jnp.full((1, LANES), 1.0 - lam_init, F32),
            jnp.zeros((2, LANES), F32)], axis=0)
        o_d = _diff_attention(proj, dl, row2(diff_norm[l]))

        flat = lambda o: o.reshape(b * s, GROUP)
        xt = _out_proj(flat(o_a), flat(o_b), flat(o_c), flat(o_d), w_out_b[l], xt,
                       row2(norm_mix_post[l]))
        xt = _mlp(xt, row2(norm_mlp_pre[l]), w_up_b[l], mlp_conv_w[l], row2(mlp_conv_b[l]),
                  w_down_b[l], row2(norm_mlp_post[l]), s)
    return xt.reshape(b, s, d)


def kernel(x, norm_mix_pre, norm_mix_post, norm_mlp_pre, norm_mlp_post, w_in, w_out, nsa_pe_k, nsa_pe_v, nsa_cmp_k1, nsa_cmp_k2, nsa_cmp_v1, nsa_cmp_v2, fox_forget_bias, diff_lambda, diff_norm, mlp_w_up, mlp_conv_w, mlp_conv_b, mlp_w_down):
    return _forward(x, norm_mix_pre, norm_mix_post, norm_mlp_pre, norm_mlp_post, w_in, w_out,
                    nsa_pe_k, nsa_pe_v, nsa_cmp_k1, nsa_cmp_k2, nsa_cmp_v1, nsa_cmp_v2,
                    fox_forget_bias, diff_lambda, diff_norm, mlp_w_up, mlp_conv_w, mlp_conv_b,
                    mlp_w_down)
```

```python
import functools
import math

import jax
import jax.numpy as jnp
import numpy as np
from jax import lax
from jax.experimental import pallas as pl
from jax.experimental.pallas import tpu as pltpu

F32 = jnp.float32
BF16 = jnp.bfloat16

LANES = 128
HEAD_DIM = 128
N_HEADS = 4
GROUP = N_HEADS * HEAD_DIM
NSA_CMP_BLOCK = 32
NSA_CMP_STRIDE = 16
NSA_SLC_BLOCK = 64
NSA_TOPK = 16
NSA_WINDOW = 512
DIFF_QK = HEAD_DIM // 2
ROPE_HALF_HEAD = HEAD_DIM // 4 // 2
ROPE_HALF_DIFF = DIFF_QK // 4 // 2
EPS = 1e-6
NEG_INF = -1e30
FORCE_SCORE = 1e9
ROPE_THETA = 500000.0
SCALE_HEAD = HEAD_DIM ** -0.5
SCALE_DIFF = DIFF_QK ** -0.5

_IN_SPLITS = (GROUP, 128, 128, 128, 128, 128, 128, 3 * N_HEADS,
              GROUP, GROUP, GROUP, GROUP, GROUP, GROUP, N_HEADS, GROUP, GROUP, GROUP)
COL_AQ, COL_AQR, COL_AKS, COL_AKW, COL_AVS, COL_AVW = 0, 4, 8, 9, 10, 11
COL_BQ, COL_BK, COL_BV = 12, 16, 20
COL_CQ, COL_CK, COL_CV = 24, 28, 32
COL_DQ, COL_DK, COL_DV = 36, 40, 44
HEAD_BLOCKS = 48
AUX_KC, AUX_VC, AUX_G, AUX_F = 0, 1, 2, 3
AUX_BLOCKS = 4

PROJ_TN = GROUP
_TILES_SCALE = (0, 3, 6)
_TILE_ROPE_Q = 1
_TILE_ROPE_KV = 2
_TILES_ROPE_DIFF = (9, 10)
_ROPE_SET_STARTS = (2, 9, 10)

VMEM_LIMIT = 56 * 1024 * 1024


def _params(n_axes, vmem=VMEM_LIMIT):
    return pltpu.CompilerParams(dimension_semantics=("arbitrary",) * n_axes,
                                vmem_limit_bytes=vmem)


def _rms(x, g):
    ms = jnp.mean(x * x, axis=-1, keepdims=True)
    return x * lax.rsqrt(ms + EPS) * g


def _dot(a, b):
    return jnp.dot(a, b, preferred_element_type=F32)


def _dot_nt(a, b):
    return lax.dot_general(a, b, (((1,), (1,)), ((), ())), preferred_element_type=F32)


def _split3(x):
    hi = x.astype(BF16)
    r = x - hi.astype(F32)
    mid = r.astype(BF16)
    lo = (r - mid.astype(F32)).astype(BF16)
    return hi, mid, lo


def _dot3(x, m):
    hi, mid, lo = _split3(x)
    return _dot(hi, m) + _dot(mid, m) + _dot(lo, m)


def _dot2(x, m):
    hi = x.astype(BF16)
    lo = (x - hi.astype(F32)).astype(BF16)
    return _dot(hi, m) + _dot(lo, m)


def _dot3_left(m, x):
    hi, mid, lo = _split3(x)
    return _dot(m, hi) + _dot(m, mid) + _dot(m, lo)


def _rope(x, c, s, half):
    width = x.shape[1]
    lane = lax.broadcasted_iota(jnp.int32, (1, width), 1)
    first = (lane % (2 * half)) < half
    xs = jnp.where(first, pltpu.roll(x, width - half, 1), pltpu.roll(x, half, 1))
    return x * c + xs * s


def _lane_fold(x, op):
    out = x[:, :LANES]
    for b in range(1, x.shape[1] // LANES):
        out = op(out, x[:, b * LANES:(b + 1) * LANES])
    return out


def _row_max(s):
    return jnp.max(_lane_fold(s, jnp.maximum), axis=1, keepdims=True)


def _flash_step(s, vt, m, l, acc):
    m_new = jnp.maximum(m, _row_max(s))
    alpha = jnp.exp(m - m_new)
    p = jnp.exp(s - m_new)
    l = alpha * l + _lane_fold(p, jnp.add)
    acc = alpha * acc + _dot(p.astype(BF16), vt)
    return m_new, l, acc


def _flash_init(rows):
    return (jnp.full((rows, 1), NEG_INF, F32), jnp.zeros((rows, LANES), F32),
            jnp.zeros((rows, HEAD_DIM), F32))


def _flash_finish(state):
    _, l, acc = state
    return acc * (1.0 / jnp.sum(l, axis=1, keepdims=True))


def _flash_sweep(lo, n_full, n_diag, n_chain, score_fn, value_fn, mask_fn, rows):
    chains = range(n_chain)

    def body(j, carry):
        out = []
        for c in chains:
            s_next = score_fn(c, j + 1)
            out.append((s_next,) + _flash_step(carry[c][0], value_fn(c, j), *carry[c][1:]))
        return tuple(out)

    carry = lax.fori_loop(lo, n_full, body,
                          tuple((score_fn(c, lo),) + _flash_init(rows) for c in chains))
    outs = []
    for c in chains:
        s, state = carry[c][0], carry[c][1:]
        for d in range(n_diag):
            if d > 0:
                s = score_fn(c, n_full + d)
            state = _flash_step(mask_fn(n_full + d, s), value_fn(c, n_full + d), *state)
        outs.append(_flash_finish(state))
    return outs


def _flash_scratch(n_chain, rows, tk):
    stat = pltpu.VMEM((n_chain, rows, LANES), F32)
    return [pltpu.VMEM((n_chain, rows, tk), F32), pltpu.VMEM((n_chain, rows, tk), F32),
            pltpu.VMEM((n_chain, rows, tk), BF16), stat, stat, stat, stat]


def _flash_sweep_ref(lo, n_full, n_chain, score_fn, value_fn, mask_fn, rows, tk, bufs, rc=16):
    s_even, s_odd, p_buf, m_ref, l_ref, a_ref, acc_ref = bufs
    chains = range(n_chain)
    reps = tk // LANES
    m_ref[...] = jnp.full(m_ref.shape, NEG_INF, F32)
    l_ref[...] = jnp.zeros(l_ref.shape, F32)
    acc_ref[...] = jnp.zeros(acc_ref.shape, F32)

    def scores(s_buf, j):
        for c in chains:
            s_buf[c] = score_fn(c, j)

    def consume(s_buf, j, masked):
        for c in chains:
            for r0 in range(0, rows, rc):
                rr = slice(r0, r0 + rc)
                s = s_buf[c, rr, :]
                if masked:
                    s = mask_fn(j, s, r0)
                m_old = m_ref[c, rr, :]
                m_new = jnp.maximum(m_old, _row_max(s))
                alpha = jnp.exp(m_old - m_new)
                p = jnp.exp(s - jnp.concatenate([m_new] * reps, axis=1))
                l_ref[c, rr, :] = alpha * l_ref[c, rr, :] + _lane_fold(p, jnp.add)
                m_ref[c, rr, :] = m_new
                a_ref[c, rr, :] = alpha
                p_buf[c, rr, :] = p.astype(BF16)
            acc_ref[c] = a_ref[c] * acc_ref[c] + _dot(p_buf[c], value_fn(c, j))

    scores(s_even, lo)

    def pair(t, carry):
        j = lo + 2 * t
        scores(s_odd, j + 1)
        consume(s_even, j, False)
        scores(s_even, j + 2)
        consume(s_odd, j + 1, False)
        return carry

    n_pair = (n_full - lo) // 2
    lax.fori_loop(0, n_pair, pair, 0)
    j = lo + 2 * n_pair
    last_full = j < n_full

    @pl.when(last_full)
    def _():
        scores(s_odd, j + 1)

    consume(s_even, j, True)

    @pl.when(last_full)
    def _():
        consume(s_odd, j + 1, True)

    return [acc_ref[c] * (1.0 / jnp.sum(l_ref[c], axis=1, keepdims=True)) for c in chains]


def _head(ref, h, rows=None):
    cols = slice(h * HEAD_DIM, (h + 1) * HEAD_DIM)
    return ref[:, cols] if rows is None else ref[rows, cols]


def _key_rows(j, tk):
    return pl.ds(pl.multiple_of(j * tk, tk), tk)


def _tile_in(j, tiles):
    hit = j == tiles[0]
    for t in tiles[1:]:
        hit = hit | (j == t)
    return hit


def _head_proj_kernel(x_ref, g_ref, w_ref, c_ref, s_ref, o_ref, xn_ref):
    j = pl.program_id(1)

    @pl.when(j == 0)
    def _():
        xn_ref[...] = _rms(x_ref[...], g_ref[...]).astype(BF16)

    acc = _dot(xn_ref[...], w_ref[...])
    is_scale = _tile_in(j, _TILES_SCALE)
    is_rope_d = _tile_in(j, _TILES_ROPE_DIFF)
    is_rope = (j == _TILE_ROPE_Q) | (j == _TILE_ROPE_KV) | is_rope_d

    @pl.when(is_scale)
    def _():
        o_ref[...] = (acc * SCALE_HEAD).astype(BF16)

    def rope_tile(half, n_rot):
        n_id = PROJ_TN // LANES - n_rot
        ones = jnp.ones(c_ref.shape, F32)
        c = jnp.concatenate([c_ref[...]] * n_rot + [ones] * n_id, axis=1)
        s = jnp.concatenate([s_ref[...]] * n_rot + [0.0 * ones] * n_id, axis=1)
        o_ref[...] = _rope(acc, c, s, half).astype(BF16)

    pl.when(j == _TILE_ROPE_Q)(lambda: rope_tile(ROPE_HALF_HEAD, N_HEADS))
    pl.when(j == _TILE_ROPE_KV)(lambda: rope_tile(ROPE_HALF_HEAD, 2))
    pl.when(is_rope_d)(lambda: rope_tile(ROPE_HALF_DIFF, N_HEADS))

    @pl.when(jnp.logical_not(is_scale | is_rope))
    def _():
        o_ref[...] = acc.astype(BF16)


def _rope_set(j):
    idx = 0
    for start in _ROPE_SET_STARTS:
        idx = idx + (j >= start).astype(jnp.int32)
    return idx


def _head_proj(x2d, g, w, cos_sets, sin_sets, seq, tm=1024):
    t, d = x2d.shape
    n = w.shape[1]
    per_seq = seq // tm
    table = pl.BlockSpec((None, tm, LANES), lambda i, j: (_rope_set(j), i % per_seq, 0))
    return pl.pallas_call(
        _head_proj_kernel,
        grid=(t // tm, n // PROJ_TN),
        in_specs=[pl.BlockSpec((tm, d), lambda i, j: (i, 0)),
                  pl.BlockSpec((1, d), lambda i, j: (0, 0)),
                  pl.BlockSpec((d, PROJ_TN), lambda i, j: (0, j)),
                  table, table],
        out_specs=pl.BlockSpec((tm, PROJ_TN), lambda i, j: (i, j)),
        out_shape=jax.ShapeDtypeStruct((t, n), BF16),
        scratch_shapes=[pltpu.VMEM((tm, d), BF16)],
        compiler_params=_params(2),
        name="head_proj",
    )(x2d, g, w, cos_sets, sin_sets)


def _norm_matmul_kernel(x_ref, g_ref, w_ref, o_ref):
    o_ref[...] = _dot(_rms(x_ref[...], g_ref[...]).astype(BF16), w_ref[...])


def _aux_proj(x2d, g, w, tm=512):
    t, d = x2d.shape
    n = w.shape[1]
    return pl.pallas_call(
        _norm_matmul_kernel,
        grid=(t // tm,),
        in_specs=[pl.BlockSpec((tm, d), lambda i: (i, 0)),
                  pl.BlockSpec((1, d), lambda i: (0, 0)),
                  pl.BlockSpec((d, n), lambda i: (0, 0))],
        out_specs=pl.BlockSpec((tm, n), lambda i: (i, 0)),
        out_shape=jax.ShapeDtypeStruct((t, n), F32),
        compiler_params=_params(1),
        name="aux_proj",
    )(x2d, g, w)


def _sb_kernel(q_ref, k_ref, v_ref, o_ref,
               ls_e, ls_o, hl_e, hl_o, tot_e, tot_o, suf_ref, e_ref, c_ref, acc_ref,
               *, tq, tk, hp, rc):
    assert tq == tk
    i = pl.program_id(2)
    reps = tk // LANES
    heads = range(hp)
    jr = lax.broadcasted_iota(jnp.int32, (tk, tk), 0)
    jc = lax.broadcasted_iota(jnp.int32, (tk, tk), 1)
    u = jnp.where(jr > jc, 1.0, 0.0).astype(BF16)
    qs = [_head(q_ref, h) for h in heads]
    c_ref[...] = jnp.zeros(c_ref.shape, F32)
    acc_ref[...] = jnp.zeros(acc_ref.shape, F32)

    lo_base = hp * tq

    def prepare(bufs, j, diagonal):
        ls_ref, hl_ref, tot_ref = bufs
        for h in heads:
            z = _dot_nt(qs[h], _head(k_ref, h, _key_rows(j, tk)))
            for r0 in range(0, tq, rc):
                rr = slice(r0, r0 + rc)
                zc = z[rr]
                ls = jnp.minimum(zc, 0.0) - jnp.log(1.0 + jnp.exp(-jnp.abs(zc)))
                lk = ls - zc
                if diagonal:
                    row = r0 + lax.broadcasted_iota(jnp.int32, (rc, 1), 0)
                    col = lax.broadcasted_iota(jnp.int32, (1, tk), 1)
                    past = col < row
                    lk = jnp.where(past, lk, 0.0)
                    ls = jnp.where(past, ls, NEG_INF)
                hi = lk.astype(BF16)
                hl_ref[h * tq + r0:h * tq + r0 + rc, :] = hi
                hl_ref[lo_base + h * tq + r0:lo_base + h * tq + r0 + rc, :] = (
                    lk - hi.astype(F32)).astype(BF16)
                ls_ref[h, rr, :] = ls
                tot = jnp.sum(_lane_fold(lk, jnp.add), axis=1, keepdims=True)
                tot_ref[h, rr, :] = jnp.broadcast_to(tot, (rc, LANES))

    def weigh(bufs, j):
        ls_ref, hl_ref, tot_ref = bufs
        suf_ref[...] = _dot(hl_ref[...], u)
        for h in heads:
            for r0 in range(0, tq, rc):
                rr = slice(r0, r0 + rc)
                c = c_ref[h, rr, :]
                suf = (suf_ref[h * tq + r0:h * tq + r0 + rc, :]
                       + suf_ref[lo_base + h * tq + r0:lo_base + h * tq + r0 + rc, :])
                pre = ls_ref[h, rr, :] + suf + jnp.concatenate([c] * reps, axis=1)
                e_ref[h, rr, :] = jnp.exp(pre).astype(BF16)
                c_ref[h, rr, :] = c + tot_ref[h, rr, :]
            acc_ref[h] += _dot(e_ref[h], _head(v_ref, h, _key_rows(j, tk)))

    even = (ls_e, hl_e, tot_e)
    odd = (ls_o, hl_o, tot_o)
    prepare(even, i, True)

    def pair(p, carry):
        j = i - 2 * p
        prepare(odd, j - 1, False)
        weigh(even, j)
        prepare(even, j - 2, False)
        weigh(odd, j - 1)
        return carry

    n_pair = i // 2
    lax.fori_loop(0, n_pair, pair, 0)
    j = i - 2 * n_pair
    one_more = j > 0

    @pl.when(one_more)
    def _():
        prepare(odd, j - 1, False)

    weigh(even, j)

    @pl.when(one_more)
    def _():
        weigh(odd, j - 1)

    for h in heads:
        o_ref[:, h * HEAD_DIM:(h + 1) * HEAD_DIM] = acc_ref[h].astype(BF16)


def _sb_attention(proj, tq=256, tk=256, hp=4, rc=16):
    b, s, _ = proj.shape
    w = hp * HEAD_DIM
    wide_f32 = pltpu.VMEM((hp, tq, tk), F32)
    wide_bf16 = pltpu.VMEM((hp, tq, tk), BF16)
    stat = pltpu.VMEM((hp, tq, LANES), F32)
    return pl.pallas_call(
        functools.partial(_sb_kernel, tq=tq, tk=tk, hp=hp, rc=rc),
        grid=(b, N_HEADS // hp, s // tq),
        in_specs=[pl.BlockSpec((None, tq, w), lambda bi, hg, i: (bi, i, COL_BQ // hp + hg)),
                  pl.BlockSpec((None, s, w), lambda bi, hg, i: (bi, 0, COL_BK // hp + hg)),
                  pl.BlockSpec((None, s, w), lambda bi, hg, i: (bi, 0, COL_BV // hp + hg))],
        out_specs=pl.BlockSpec((None, tq, w), lambda bi, hg, i: (bi, i, hg)),
        out_shape=jax.ShapeDtypeStruct((b, s, GROUP), BF16),
        scratch_shapes=[wide_f32, wide_f32,
                        pltpu.VMEM((2 * hp * tq, tk), BF16), pltpu.VMEM((2 * hp * tq, tk), BF16),
                        stat, stat, pltpu.VMEM((2 * hp * tq, tk), F32), wide_bf16, stat, stat],
        compiler_params=_params(3),
        name="sb_attention",
    )(proj, proj, proj)


def _fox_cum_kernel(cf_ref, bias_ref, col_ref, row_ref, *, chunk):
    s = cf_ref.shape[0]
    lf = jax.nn.log_sigmoid(cf_ref[...] + bias_ref[...])
    rr = lax.broadcasted_iota(jnp.int32, (chunk, chunk), 0)
    cc = lax.broadcasted_iota(jnp.int32, (chunk, chunk), 1)
    tri = jnp.where(rr >= cc, 1.0, 0.0).astype(BF16)
    carry = jnp.zeros((1, LANES), F32)
    for c in range(s // chunk):
        pre = _dot3_left(tri, lf[c * chunk:(c + 1) * chunk]) + carry
        col_ref[c * chunk:(c + 1) * chunk, :] = pre
        carry = pre[chunk - 1:chunk, :]
    row_ref[...] = jnp.transpose(col_ref[...])[:8, :]


def _fox_cum(aux, bias_row, chunk=256):
    b, s, _ = aux.shape
    return pl.pallas_call(
        functools.partial(_fox_cum_kernel, chunk=chunk),
        grid=(b,),
        in_specs=[pl.BlockSpec((None, s, LANES), lambda bi: (bi, 0, AUX_F)),
                  pl.BlockSpec((1, LANES), lambda bi: (0, 0))],
        out_specs=[pl.BlockSpec((None, s, LANES), lambda bi: (bi, 0, 0)),
                   pl.BlockSpec((None, 8, s), lambda bi: (bi, 0, 0))],
        out_shape=[jax.ShapeDtypeStruct((b, s, LANES), F32),
                   jax.ShapeDtypeStruct((b, 8, s), F32)],
        compiler_params=_params(1),
        name="fox_cum",
    )(aux, bias_row)


def _fox_kernel(q_ref, k_ref, v_ref, cq_ref, ck_ref, o_ref, *bufs, tq, tk, hp):
    assert tk >= tq
    hg = pl.program_id(1)
    i = pl.program_id(2)
    lane = lax.broadcasted_iota(jnp.int32, (1, LANES), 1)
    cum_q = cq_ref[...]
    qs = [_head(q_ref, h) for h in range(hp)]
    cqs = [jnp.sum(jnp.where(lane == hg * hp + h, cum_q, 0.0), axis=1, keepdims=True)
           for h in range(hp)]

    def score(h, j):
        return (_dot_nt(qs[h], _head(k_ref, h, _key_rows(j, tk))) + cqs[h]
                - ck_ref[h, pl.ds(j, 1), :])

    def value(h, j):
        return _head(v_ref, h, _key_rows(j, tk))

    def causal(j, s, r0):
        row = i * tq + r0 + lax.broadcasted_iota(jnp.int32, (s.shape[0], 1), 0)
        col = j * tk + lax.broadcasted_iota(jnp.int32, (1, tk), 1)
        return jnp.where(col <= row, s, NEG_INF)

    outs = _flash_sweep_ref(0, (i * tq) // tk, hp, score, value, causal, tq, tk, bufs)
    for h in range(hp):
        o_ref[:, h * HEAD_DIM:(h + 1) * HEAD_DIM] = outs[h].astype(BF16)


def _fox_attention(proj, cum_col, cum_row, tq=256, tk=512, hp=4):
    b, s, _ = proj.shape
    w = hp * HEAD_DIM
    ck = cum_row.reshape(b, 8, s // tk, tk)
    return pl.pallas_call(
        functools.partial(_fox_kernel, tq=tq, tk=tk, hp=hp),
        grid=(b, N_HEADS // hp, s // tq),
        in_specs=[pl.BlockSpec((None, tq, w), lambda bi, hg, i: (bi, i, COL_CQ // hp + hg)),
                  pl.BlockSpec((None, s, w), lambda bi, hg, i: (bi, 0, COL_CK // hp + hg)),
                  pl.BlockSpec((None, s, w), lambda bi, hg, i: (bi, 0, COL_CV // hp + hg)),
                  pl.BlockSpec((None, tq, LANES), lambda bi, hg, i: (bi, i, 0)),
                  pl.BlockSpec((None, hp, s // tk, tk), lambda bi, hg, i: (bi, hg, 0, 0))],
        out_specs=pl.BlockSpec((None, tq, w), lambda bi, hg, i: (bi, i, hg)),
        out_shape=jax.ShapeDtypeStruct((b, s, GROUP), BF16),
        scratch_shapes=_flash_scratch(hp, tq, tk),
        compiler_params=_params(3),
        name="fox_attention",
    )(proj, proj, proj, cum_col, ck)


def _diff_kernel(q_ref, k_ref, v_ref, dl_ref, gn_ref, o_ref, *bufs, tq, tk, hp):
    assert tk >= tq
    i = pl.program_id(2)
    lane = lax.broadcasted_iota(jnp.int32, (1, LANES), 1)
    zero = jnp.zeros((), BF16)
    qs = []
    for h in range(hp):
        qh = _head(q_ref, h)
        qs.append(jnp.concatenate([jnp.where(lane < DIFF_QK, qh, zero),
                                   jnp.where(lane >= DIFF_QK, qh, zero)], axis=0))

    def score(h, j):
        return _dot_nt(qs[h], _head(k_ref, h, _key_rows(j, tk)))

    def value(h, j):
        return _head(v_ref, h, _key_rows(j, tk))

    def causal(j, s, r0):
        row = i * tq + r0 % tq + lax.broadcasted_iota(jnp.int32, (s.shape[0], 1), 0)
        col = j * tk + lax.broadcasted_iota(jnp.int32, (1, tk), 1)
        return jnp.where(col <= row, s, NEG_INF)

    outs = _flash_sweep_ref(0, (i * tq) // tk, hp, score, value, causal, 2 * tq, tk, bufs,
                            rc=32)
    dl = dl_ref[...]
    lam = (jnp.exp(jnp.sum(dl[0:1] * dl[1:2], axis=1, keepdims=True))
           - jnp.exp(jnp.sum(dl[2:3] * dl[3:4], axis=1, keepdims=True)) + dl[4:5, 0:1])
    for h in range(hp):
        od = outs[h][:tq] - lam * outs[h][tq:]
        o_ref[:, h * HEAD_DIM:(h + 1) * HEAD_DIM] = (
            _rms(od, gn_ref[...]) * dl[5:6, 0:1]).astype(BF16)


def _diff_attention(proj, dl, gn, tq=256, tk=512, hp=4):
    b, s, _ = proj.shape
    w = hp * HEAD_DIM
    return pl.pallas_call(
        functools.partial(_diff_kernel, tq=tq, tk=tk, hp=hp),
        grid=(b, N_HEADS // hp, s // tq),
        in_specs=[pl.BlockSpec((None, tq, w), lambda bi, hg, i: (bi, i, COL_DQ // hp + hg)),
                  pl.BlockSpec((None, s, w), lambda bi, hg, i: (bi, 0, COL_DK // hp + hg)),
                  pl.BlockSpec((None, s, w), lambda bi, hg, i: (bi, 0, COL_DV // hp + hg)),
                  pl.BlockSpec((8, LANES), lambda bi, hg, i: (0, 0)),
                  pl.BlockSpec((1, LANES), lambda bi, hg, i: (0, 0))],
        out_specs=pl.BlockSpec((None, tq, w), lambda bi, hg, i: (bi, i, hg)),
        out_shape=jax.ShapeDtypeStruct((b, s, GROUP), BF16),
        scratch_shapes=_flash_scratch(hp, 2 * tq, tk),
        compiler_params=_params(3),
        name="diff_attention",
    )(proj, proj, proj, dl, gn)


def _cmp_kernel(k2_ref, v2_ref, pek_ref, pev_ref, wk1_ref, wk2_ref, wv1_ref, wv2_ref,
                kc_ref, vc_ref):
    nb = k2_ref.shape[0]
    w = NSA_CMP_STRIDE * HEAD_DIM

    def compress(x2, pe, w1_ref, w2_ref):
        a = _dot((x2 + pe[:, :w]).astype(BF16), w1_ref[:w, :])
        bb = _dot((x2 + pe[:, w:]).astype(BF16), w1_ref[w:, :])
        hid = jax.nn.gelu(a + pltpu.roll(bb, nb - 1, 0), approximate=True)
        return _dot(hid.astype(BF16), w2_ref[...])

    kc_ref[...] = compress(k2_ref[...], pek_ref[...], wk1_ref, wk2_ref).astype(BF16)
    vc_ref[...] = compress(v2_ref[...], pev_ref[...], wv1_ref, wv2_ref).astype(BF16)


def _nsa_compress(k2, v2, pek, pev, wk1, wk2, wv1, wv2):
    b, nb, w = k2.shape
    hid = wk1.shape[1]
    full = lambda *shape: pl.BlockSpec(shape, lambda bi: (0,) * len(shape))
    return pl.pallas_call(
        _cmp_kernel,
        grid=(b,),
        in_specs=[pl.BlockSpec((None, nb, w), lambda bi: (bi, 0, 0)),
                  pl.BlockSpec((None, nb, w), lambda bi: (bi, 0, 0)),
                  full(1, 2 * w), full(1, 2 * w),
                  full(2 * w, hid), full(hid, HEAD_DIM), full(2 * w, hid), full(hid, HEAD_DIM)],
        out_specs=[pl.BlockSpec((None, nb, HEAD_DIM), lambda bi: (bi, 0, 0)),
                   pl.BlockSpec((None, nb, HEAD_DIM), lambda bi: (bi, 0, 0))],
        out_shape=[jax.ShapeDtypeStruct((b, nb, HEAD_DIM), BF16),
                   jax.ShapeDtypeStruct((b, nb, HEAD_DIM), BF16)],
        compiler_params=_params(1),
        name="nsa_compress",
    )(k2, v2, pek, pev, wk1, wk2, wv1, wv2)


def _nsa_kernel(q_ref, qr_ref, g_ref, kc_ref, vc_ref, ks_ref, kw_ref, vs_ref, vw_ref, o_ref,
                *bufs, tq, tk, n_top):
    assert tk >= tq
    i = pl.program_id(1)
    nb = kc_ref.shape[0]
    n_sel = NSA_SLC_BLOCK
    hq = N_HEADS * tq

    q_raw = jnp.concatenate([_head(q_ref, h) for h in range(N_HEADS)], axis=0)
    q_rot = jnp.concatenate([_head(qr_ref, h) for h in range(N_HEADS)], axis=0)
    row = i * tq + lax.broadcasted_iota(jnp.int32, (tq, 1), 0)
    rep = lambda a: jnp.concatenate([a] * N_HEADS, axis=0)

    cidx = lax.broadcasted_iota(jnp.int32, (1, nb), 1)
    c_valid = rep((cidx * NSA_CMP_STRIDE + (NSA_CMP_BLOCK - 1)) <= row)
    sc = jnp.where(c_valid, _dot_nt(q_raw, kc_ref[...]), NEG_INF)
    e = jnp.exp(sc - jnp.max(sc, axis=1, keepdims=True))
    p = jnp.where(c_valid, e * (1.0 / jnp.sum(e, axis=1, keepdims=True)), 0.0)
    o_cmp = _dot(p.astype(BF16), vc_ref[...])
    p_sum = p[0:tq]
    for h in range(1, N_HEADS):
        p_sum = p_sum + p[h * tq:(h + 1) * tq]
    oc = lax.broadcasted_iota(jnp.int32, (nb, LANES), 0)
    oj = lax.broadcasted_iota(jnp.int32, (nb, LANES), 1)
    per = NSA_SLC_BLOCK // NSA_CMP_STRIDE
    overlap = (jnp.where(oc // per == oj, 1.0, 0.0)
               + jnp.where((oc + 1) // per == oj, 1.0, 0.0)).astype(BF16)
    imp = _dot3(p_sum, overlap)

    imp_t = jnp.transpose(imp)[:n_sel, :]
    jb = lax.broadcasted_iota(jnp.int32, (n_sel, 1), 0)
    cur = (i * tq + lax.broadcasted_iota(jnp.int32, (1, tq), 1)) // NSA_SLC_BLOCK
    sel_valid = jb <= cur
    forced = sel_valid & ((jb == 0) | (jb >= cur - 1))
    score = jnp.where(forced, FORCE_SCORE, jnp.where(sel_valid, imp_t, NEG_INF))
    cnt = jnp.zeros((n_sel, tq), F32)
    for ii in range(n_sel):
        ri = score[ii:ii + 1, :]
        beats = (ri > score) | ((ri == score) & (jb > ii))
        cnt = cnt + jnp.where(beats, 1.0, 0.0)
    sel_t = jnp.where((cnt < n_top) & sel_valid, 1.0, 0.0)
    sel = jnp.transpose(jnp.concatenate([sel_t, jnp.zeros_like(sel_t)], axis=0)).astype(BF16)

    er = lax.broadcasted_iota(jnp.int32, (LANES, tk), 0)
    ec = lax.broadcasted_iota(jnp.int32, (LANES, tk), 1)

    def causal(j, s, r0):
        rows = i * tq + r0 % tq + lax.broadcasted_iota(jnp.int32, (s.shape[0], 1), 0)
        col = j * tk + lax.broadcasted_iota(jnp.int32, (1, tk), 1)
        return jnp.where(col <= rows, s, NEG_INF)

    def slc_score(_, j):
        expand = jnp.where(er == (j * tk + ec) // NSA_SLC_BLOCK, 1.0, 0.0).astype(BF16)
        ok = _dot(sel, expand) > 0.5
        return jnp.where(rep(ok), _dot_nt(q_rot, ks_ref[_key_rows(j, tk), :]), NEG_INF)

    n_full = (i * tq) // tk
    o_slc, = _flash_sweep_ref(0, n_full, 1, slc_score,
                              lambda _, j: vs_ref[_key_rows(j, tk), :], causal, hq, tk, bufs,
                              rc=32)

    def win_score(_, j):
        col = j * tk + lax.broadcasted_iota(jnp.int32, (1, tk), 1)
        return jnp.where(rep(col > row - NSA_WINDOW),
                         _dot_nt(q_rot, kw_ref[_key_rows(j, tk), :]), NEG_INF)

    j_lo = (jnp.maximum(i * tq - NSA_WINDOW, 0) // (2 * tk)) * 2
    o_win, = _flash_sweep_ref(j_lo, n_full, 1, win_score,
                              lambda _, j: vw_ref[_key_rows(j, tk), :], causal, hq, tk, bufs,
                              rc=32)

    g = jax.nn.sigmoid(g_ref[...])
    for h in range(N_HEADS):
        rows = slice(h * tq, (h + 1) * tq)
        o_ref[:, h * HEAD_DIM:(h + 1) * HEAD_DIM] = (
            g[:, 3 * h:3 * h + 1] * o_cmp[rows] + g[:, 3 * h + 1:3 * h + 2] * o_slc[rows]
            + g[:, 3 * h + 2:3 * h + 3] * o_win[rows]).astype(BF16)


def _nsa_attention(proj, aux, kc, vc, tq=256, tk=512):
    b, s, _ = proj.shape
    nb = kc.shape[1]
    n_top = min(NSA_TOPK, s // NSA_SLC_BLOCK)
    kv = lambda col: pl.BlockSpec((None, s, HEAD_DIM), lambda bi, i: (bi, 0, col))
    return pl.pallas_call(
        functools.partial(_nsa_kernel, tq=tq, tk=tk, n_top=n_top),
        grid=(b, s // tq),
        in_specs=[pl.BlockSpec((None, tq, GROUP), lambda bi, i: (bi, i, COL_AQ // N_HEADS)),
                  pl.BlockSpec((None, tq, GROUP), lambda bi, i: (bi, i, COL_AQR // N_HEADS)),
                  pl.BlockSpec((None, tq, LANES), lambda bi, i: (bi, i, AUX_G)),
                  pl.BlockSpec((None, nb, HEAD_DIM), lambda bi, i: (bi, 0, 0)),
                  pl.BlockSpec((None, nb, HEAD_DIM), lambda bi, i: (bi, 0, 0)),
                  kv(COL_AKS), kv(COL_AKW), kv(COL_AVS), kv(COL_AVW)],
        out_specs=pl.BlockSpec((None, tq, GROUP), lambda bi, i: (bi, i, 0)),
        out_shape=jax.ShapeDtypeStruct((b, s, GROUP), BF16),
        scratch_shapes=_flash_scratch(1, N_HEADS * tq, tk),
        compiler_params=_params(2),
        name="nsa_attention",
    )(proj, proj, aux, kc, vc, proj, proj, proj, proj)


def _out_kernel(oa_ref, ob_ref, oc_ref, od_ref, w_ref, x_ref, g_ref, o_ref):
    acc = _dot(oa_ref[...], w_ref[0:GROUP, :])
    acc = acc + _dot(ob_ref[...], w_ref[GROUP:2 * GROUP, :])
    acc = acc + _dot(oc_ref[...], w_ref[2 * GROUP:3 * GROUP, :])
    acc = acc + _dot(od_ref[...], w_ref[3 * GROUP:4 * GROUP, :])
    o_ref[...] = x_ref[...] + _rms(acc, g_ref[...])


def _out_proj(oa, ob, oc, od, w, x2d, g, tm=512):
    t, d = x2d.shape
    mix = pl.BlockSpec((tm, GROUP), lambda i: (i, 0))
    return pl.pallas_call(
        _out_kernel,
        grid=(t // tm,),
        in_specs=[mix, mix, mix, mix,
                  pl.BlockSpec((4 * GROUP, d), lambda i: (0, 0)),
                  pl.BlockSpec((tm, d), lambda i: (i, 0)),
                  pl.BlockSpec((1, d), lambda i: (0, 0))],
        out_specs=pl.BlockSpec((tm, d), lambda i: (i, 0)),
        out_shape=jax.ShapeDtypeStruct((t, d), F32),
        compiler_params=_params(1),
        name="out_proj",
    )(oa, ob, oc, od, w, x2d, g)


HALO = 16


def _mlp_kernel(x_ref, xh_ref, gpre_ref, wg_ref, wu_ref, cwg_ref, cwu_ref, cbg_ref, cbu_ref,
                wd_ref, gpost_ref, o_ref, xn_ref, acc_ref, *, tm, tiles_per_seq):
    i = pl.program_id(0)
    f = pl.program_id(1)

    @pl.when(f == 0)
    def _():
        xn_ref[HALO:, :] = _rms(x_ref[...], gpre_ref[...]).astype(BF16)
        halo = _rms(xh_ref[...], gpre_ref[...])
        seq_start = (i % tiles_per_seq) == 0
        xn_ref[:HALO, :] = jnp.where(seq_start, 0.0, halo).astype(BF16)
        acc_ref[...] = jnp.zeros_like(acc_ref)

    xn = xn_ref[...]

    def conv(u, cw_ref, cb_ref):
        cw = cw_ref[...]
        out = cb_ref[...] + cw[0:1] * u[HALO - 2:HALO - 2 + tm]
        out = out + cw[1:2] * u[HALO - 1:HALO - 1 + tm]
        return out + cw[2:3] * u[HALO:HALO + tm]

    gate = conv(_dot(xn, wg_ref[...]), cwg_ref, cbg_ref)
    up = conv(_dot(xn, wu_ref[...]), cwu_ref, cbu_ref)
    hid = (jax.nn.gelu(gate, approximate=True) * up).astype(BF16)
    acc_ref[...] += _dot(hid, wd_ref[...])

    @pl.when(f == pl.num_programs(1) - 1)
    def _():
        o_ref[...] = x_ref[...] + _rms(acc_ref[...], gpost_ref[...])


def _mlp(x2d, gpre, w_up, conv_w, conv_b, w_down, gpost, seq, tm=1024, tf=512):
    t, d = x2d.shape
    ff = w_down.shape[0]
    nf = ff // tf
    hb = tm // HALO
    once = pl.Buffered(1)
    return pl.pallas_call(
        functools.partial(_mlp_kernel, tm=tm, tiles_per_seq=seq // tm),
        grid=(t // tm, nf),
        in_specs=[pl.BlockSpec((tm, d), lambda i, f: (i, 0), pipeline_mode=once),
                  pl.BlockSpec((HALO, d), lambda i, f: (jnp.maximum(i * hb - 1, 0), 0)),
                  pl.BlockSpec((1, d), lambda i, f: (0, 0)),
                  pl.BlockSpec((d, tf), lambda i, f: (0, f)),
                  pl.BlockSpec((d, tf), lambda i, f: (0, nf + f)),
                  pl.BlockSpec((3, tf), lambda i, f: (0, f)),
                  pl.BlockSpec((3, tf), lambda i, f: (0, nf + f)),
                  pl.BlockSpec((1, tf), lambda i, f: (0, f)),
                  pl.BlockSpec((1, tf), lambda i, f: (0, nf + f)),
                  pl.BlockSpec((tf, d), lambda i, f: (f, 0)),
                  pl.BlockSpec((1, d), lambda i, f: (0, 0))],
        out_specs=pl.BlockSpec((tm, d), lambda i, f: (i, 0), pipeline_mode=once),
        out_shape=jax.ShapeDtypeStruct((t, d), F32),
        scratch_shapes=[pltpu.VMEM((tm + HALO, d), BF16), pltpu.VMEM((tm, d), F32)],
        compiler_params=_params(2),
        name="conv_geglu_mlp",
    )(x2d, x2d, gpre, w_up, w_up, conv_w, conv_w, conv_b, conv_b, w_down, gpost)


def _split_w_in(w):
    (a_q, a_kc, a_vc, a_ks, a_vs, a_kw, a_vw, a_g, b_q, b_k, b_v,
     c_q, c_k, c_v, c_f, d_q, d_k, d_v) = jnp.split(w.astype(BF16),
                                                    np.cumsum(_IN_SPLITS)[:-1].tolist(), axis=-1)
    pad = lambda p: jnp.pad(p, ((0, 0), (0, 0), (0, LANES - p.shape[-1])))
    heads = jnp.concatenate([a_q, a_q, a_ks, a_kw, a_vs, a_vw, b_q, b_k, b_v,
                             c_q, c_k, c_v, d_q, d_k, d_v], axis=-1)
    aux = jnp.concatenate([a_kc, a_vc, pad(a_g), pad(c_f)], axis=-1)
    return heads.astype(BF16), aux.astype(BF16)


def _rope_lane_tables(seq, rot_dim, period):
    pos = jnp.arange(seq, dtype=F32)
    inv = ROPE_THETA ** (-jnp.arange(0, rot_dim, 2, dtype=F32) / rot_dim)
    ang = pos[:, None] * inv[None, :]
    cos, sin = jnp.cos(ang), jnp.sin(ang)
    rest = period - rot_dim
    c = jnp.concatenate([cos, cos, jnp.ones((seq, rest), F32)], axis=-1)
    s = jnp.concatenate([-sin, sin, jnp.zeros((seq, rest), F32)], axis=-1)
    reps = LANES // period
    return jnp.tile(c, (1, reps)), jnp.tile(s, (1, reps))


@jax.jit
def _forward(x, norm_mix_pre, norm_mix_post, norm_mlp_pre, norm_mlp_post, w_in, w_out,
             nsa_pe_k, nsa_pe_v, nsa_cmp_k1, nsa_cmp_k2, nsa_cmp_v1, nsa_cmp_v2,
             fox_forget_bias, diff_lambda, diff_norm, mlp_w_up, mlp_conv_w, mlp_conv_b,
             mlp_w_down):
    b, s, d = x.shape
    depth = w_in.shape[0]
    cos_h, sin_h = _rope_lane_tables(s, HEAD_DIM // 4, LANES)
    cos_d, sin_d = _rope_lane_tables(s, DIFF_QK // 4, DIFF_QK)
    cos_sets = jnp.stack([cos_h * SCALE_HEAD, cos_h, cos_d * SCALE_DIFF, cos_d])
    sin_sets = jnp.stack([sin_h * SCALE_HEAD, sin_h, sin_d * SCALE_DIFF, sin_d])
    w_heads, w_aux = _split_w_in(w_in)
    w_out_b = w_out.astype(BF16)
    w_up_b = mlp_w_up.astype(BF16)
    w_down_b = mlp_w_down.astype(BF16)
    k1_b, k2_b = nsa_cmp_k1.astype(BF16), nsa_cmp_k2.astype(BF16)
    v1_b, v2_b = nsa_cmp_v1.astype(BF16), nsa_cmp_v2.astype(BF16)
    fox_bias = jnp.pad(fox_forget_bias, ((0, 0), (0, LANES - N_HEADS)))
    row2 = lambda a: a.reshape(1, -1)
    cmp_w = NSA_CMP_STRIDE * HEAD_DIM

    xt = x.reshape(b * s, d)
    for l in range(depth):
        g_pre = row2(norm_mix_pre[l])
        proj = _head_proj(xt, g_pre, w_heads[l], cos_sets, sin_sets, s, tm=min(1024, s))
        proj = proj.reshape(b, s, HEAD_BLOCKS * LANES)
        aux = _aux_proj(xt, g_pre, w_aux[l]).reshape(b, s, AUX_BLOCKS * LANES)

        k2 = aux[:, :, AUX_KC * LANES:(AUX_KC + 1) * LANES].reshape(b, s // NSA_CMP_STRIDE, cmp_w)
        v2 = aux[:, :, AUX_VC * LANES:(AUX_VC + 1) * LANES].reshape(b, s // NSA_CMP_STRIDE, cmp_w)
        kc, vc = _nsa_compress(k2, v2, row2(nsa_pe_k[l]), row2(nsa_pe_v[l]),
                               k1_b[l], k2_b[l], v1_b[l], v2_b[l])
        o_a = _nsa_attention(proj, aux, kc, vc)

        o_b = _sb_attention(proj)

        cum_col, cum_row = _fox_cum(aux, fox_bias[l:l + 1])
        o_c = _fox_attention(proj, cum_col, cum_row)

        lam_init = 0.8 - 0.6 * math.exp(-0.3 * l)
        dl = jnp.concatenate([
            jnp.pad(diff_lambda[l], ((0, 0), (0, LANES - DIFF_QK))),
            jnp.full((1, LANES), lam_init, F32),
            jnp.full((1, LANES), 1.0 - lam_init, F32),
            jnp.zeros((2, LANES), F32)], axis=0)
        o_d = _diff_attention(proj, dl, row2(diff_norm[l]))

        flat = lambda o: o.reshape(b * s, GROUP)
        xt = _out_proj(flat(o_a), flat(o_b), flat(o_c), flat(o_d), w_out_b[l], xt,
                       row2(norm_mix_post[l]))
        xt = _mlp(xt, row2(norm_mlp_pre[l]), w_up_b[l], mlp_conv_w[l], row2(mlp_conv_b[l]),
                  w_down_b[l], row2(norm_mlp_post[l]), s, tm=min(1024, s))
    return xt.reshape(b, s, d)


def kernel(x, norm_mix_pre, norm_mix_post, norm_mlp_pre, norm_mlp_post, w_in, w_out, nsa_pe_k, nsa_pe_v, nsa_cmp_k1, nsa_cmp_k2, nsa_cmp_v1, nsa_cmp_v2, fox_forget_bias, diff_lambda, diff_norm, mlp_w_up, mlp_conv_w, mlp_conv_b, mlp_w_down):
    return _forward(x, norm_mix_pre, norm_mix_post, norm_mlp_pre, norm_mlp_post, w_in, w_out,
                    nsa_pe_k, nsa_pe_v, nsa_cmp_k1, nsa_cmp_k2, nsa_cmp_v1, nsa_cmp_v2,
                    fox_forget_bias, diff_lambda, diff_norm, mlp_w_up, mlp_conv_w, mlp_conv_b,
                    mlp_w_down)
```

```python
import functools
import math

import jax
import jax.numpy as jnp
import numpy as np
from jax import lax
from jax.experimental import pallas as pl
from jax.experimental.pallas import tpu as pltpu

F32 = jnp.float32
BF16 = jnp.bfloat16

LANES = 128
HEAD_DIM = 128
N_HEADS = 4
GROUP = N_HEADS * HEAD_DIM
NSA_CMP_BLOCK = 32
NSA_CMP_STRIDE = 16
NSA_SLC_BLOCK = 64
NSA_TOPK = 16
NSA_WINDOW = 512
DIFF_QK = HEAD_DIM // 2
ROPE_HALF_HEAD = HEAD_DIM // 4 // 2
ROPE_HALF_DIFF = DIFF_QK // 4 // 2
EPS = 1e-6
NEG_INF = -1e30
FORCE_SCORE = 1e9
ROPE_THETA = 500000.0
LOG2E = math.log2(math.e)
SCALE_HEAD = HEAD_DIM ** -0.5 * LOG2E
SCALE_DIFF = DIFF_QK ** -0.5 * LOG2E

_IN_SPLITS = (GROUP, 128, 128, 128, 128, 128, 128, 3 * N_HEADS,
              GROUP, GROUP, GROUP, GROUP, GROUP, GROUP, N_HEADS, GROUP, GROUP, GROUP)
COL_AQ, COL_AQR, COL_AKS, COL_AVS, COL_AKW, COL_AVW = 0, 4, 8, 9, 10, 11
COL_BQ, COL_BK, COL_BV = 12, 16, 20
COL_CQ, COL_CK, COL_CV = 24, 28, 32
COL_DQ, COL_DK, COL_DV = 36, 40, 44
HEAD_BLOCKS = 48
AUX_KC, AUX_VC, AUX_G, AUX_F = 0, 1, 2, 3
AUX_BLOCKS = 4

PROJ_TN = GROUP
_TILES_SCALE = (0, 3, 6)
_TILE_ROPE_Q = 1
_TILE_ROPE_KV = 2
_TILES_ROPE_DIFF = (9, 10)
_ROPE_SET_STARTS = (2, 9, 10)

VMEM_LIMIT = 56 * 1024 * 1024


def _params(n_axes, vmem=VMEM_LIMIT):
    return pltpu.CompilerParams(dimension_semantics=("arbitrary",) * n_axes,
                                vmem_limit_bytes=vmem)


def _rms(x, g):
    ms = jnp.mean(x * x, axis=-1, keepdims=True)
    return x * lax.rsqrt(ms + EPS) * g


def _dot(a, b):
    return jnp.dot(a, b, preferred_element_type=F32)


def _dot_nt(a, b):
    return lax.dot_general(a, b, (((1,), (1,)), ((), ())), preferred_element_type=F32)


def _split3(x):
    hi = x.astype(BF16)
    r = x - hi.astype(F32)
    mid = r.astype(BF16)
    lo = (r - mid.astype(F32)).astype(BF16)
    return hi, mid, lo


def _dot3(x, m):
    hi, mid, lo = _split3(x)
    return _dot(hi, m) + _dot(mid, m) + _dot(lo, m)


def _dot2(x, m):
    hi = x.astype(BF16)
    lo = (x - hi.astype(F32)).astype(BF16)
    return _dot(hi, m) + _dot(lo, m)


def _dot3_left(m, x):
    hi, mid, lo = _split3(x)
    return _dot(m, hi) + _dot(m, mid) + _dot(m, lo)


def _rope(x, c, s, half):
    width = x.shape[1]
    lane = lax.broadcasted_iota(jnp.int32, (1, width), 1)
    first = (lane % (2 * half)) < half
    xs = jnp.where(first, pltpu.roll(x, width - half, 1), pltpu.roll(x, half, 1))
    return x * c + xs * s


def _lane_fold(x, op):
    out = x[:, :LANES]
    for b in range(1, x.shape[1] // LANES):
        out = op(out, x[:, b * LANES:(b + 1) * LANES])
    return out


def _row_max(s):
    return jnp.max(_lane_fold(s, jnp.maximum), axis=1, keepdims=True)


def _flash_scratch(n_chain, rows, tk):
    stat = pltpu.VMEM((n_chain, rows, LANES), F32)
    return [pltpu.VMEM((n_chain, rows, tk), F32), pltpu.VMEM((n_chain, rows, tk), F32),
            pltpu.VMEM((n_chain, rows, tk), BF16), stat, stat,
            pltpu.VMEM((n_chain, rows, 2 * LANES), F32)]


def _flash_sweep_ref(lo, n_full, n_chain, score_fn, value_fn, mask_fn, rows, tk, bufs, rc=16):
    s_even, s_odd, p_buf, m_ref, a_ref, acc_ref = bufs
    chains = range(n_chain)
    reps = tk // LANES
    ones = jnp.ones((tk, LANES), BF16)
    m_ref[...] = jnp.full(m_ref.shape, NEG_INF, F32)
    acc_ref[...] = jnp.zeros(acc_ref.shape, F32)

    def scores(s_buf, j):
        for c in chains:
            s_buf[c] = score_fn(c, j)

    def consume(s_buf, j, masked):
        for c in chains:
            for r0 in range(0, rows, rc):
                rr = slice(r0, r0 + rc)
                s = s_buf[c, rr, :]
                if masked:
                    s = mask_fn(j, s, r0)
                m_old = m_ref[c, rr, :]
                m_new = jnp.maximum(m_old, _row_max(s))
                m_ref[c, rr, :] = m_new
                a_ref[c, rr, :] = jnp.exp2(m_old - m_new)
                shifted = s - jnp.concatenate([m_new] * reps, axis=1)
                p_buf[c, rr, :] = jnp.exp2(shifted.astype(BF16))
            v_ones = jnp.concatenate([value_fn(c, j), ones], axis=1)
            alpha = a_ref[c]
            acc_ref[c] = (jnp.concatenate([alpha, alpha], axis=1) * acc_ref[c]
                          + _dot(p_buf[c], v_ones))

    scores(s_even, lo)

    def pair(t, carry):
        j = lo + 2 * t
        scores(s_odd, j + 1)
        consume(s_even, j, False)
        scores(s_even, j + 2)
        consume(s_odd, j + 1, False)
        return carry

    n_pair = (n_full - lo) // 2
    lax.fori_loop(0, n_pair, pair, 0)
    j = lo + 2 * n_pair
    last_full = j < n_full

    @pl.when(last_full)
    def _():
        scores(s_odd, j + 1)

    consume(s_even, j, True)

    @pl.when(last_full)
    def _():
        consume(s_odd, j + 1, True)

    return [acc_ref[c, :, :LANES] * (1.0 / acc_ref[c, :, LANES:]) for c in chains]


def _head(ref, h, rows=None):
    cols = slice(h * HEAD_DIM, (h + 1) * HEAD_DIM)
    return ref[:, cols] if rows is None else ref[rows, cols]


def _key_rows(j, tk):
    return pl.ds(pl.multiple_of(j * tk, tk), tk)


def _tile_in(j, tiles):
    hit = j == tiles[0]
    for t in tiles[1:]:
        hit = hit | (j == t)
    return hit


def _head_proj_kernel(x_ref, g_ref, w_ref, c_ref, s_ref, o_ref, xn_ref):
    j = pl.program_id(1)

    @pl.when(j == 0)
    def _():
        xn_ref[...] = _rms(x_ref[...], g_ref[...]).astype(BF16)

    acc = _dot(xn_ref[...], w_ref[...])
    is_scale = _tile_in(j, _TILES_SCALE)
    is_rope_d = _tile_in(j, _TILES_ROPE_DIFF)
    is_rope = (j == _TILE_ROPE_Q) | (j == _TILE_ROPE_KV) | is_rope_d

    @pl.when(is_scale)
    def _():
        o_ref[...] = (acc * SCALE_HEAD).astype(BF16)

    def rope_tile(half, rotated):
        ones = jnp.ones(c_ref.shape, F32)
        c = jnp.concatenate([c_ref[...] if r else ones for r in rotated], axis=1)
        s = jnp.concatenate([s_ref[...] if r else 0.0 * ones for r in rotated], axis=1)
        o_ref[...] = _rope(acc, c, s, half).astype(BF16)

    every = (True,) * N_HEADS
    pl.when(j == _TILE_ROPE_Q)(lambda: rope_tile(ROPE_HALF_HEAD, every))
    pl.when(j == _TILE_ROPE_KV)(lambda: rope_tile(ROPE_HALF_HEAD, (True, False, True, False)))
    pl.when(is_rope_d)(lambda: rope_tile(ROPE_HALF_DIFF, every))

    @pl.when(jnp.logical_not(is_scale | is_rope))
    def _():
        o_ref[...] = acc.astype(BF16)


def _rope_set(j):
    idx = 0
    for start in _ROPE_SET_STARTS:
        idx = idx + (j >= start).astype(jnp.int32)
    return idx


def _head_proj(x2d, g, w, cos_sets, sin_sets, seq, tm=1024):
    t, d = x2d.shape
    n = w.shape[1]
    per_seq = seq // tm
    table = pl.BlockSpec((None, tm, LANES), lambda i, j: (_rope_set(j), i % per_seq, 0))
    return pl.pallas_call(
        _head_proj_kernel,
        grid=(t // tm, n // PROJ_TN),
        in_specs=[pl.BlockSpec((tm, d), lambda i, j: (i, 0)),
                  pl.BlockSpec((1, d), lambda i, j: (0, 0)),
                  pl.BlockSpec((d, PROJ_TN), lambda i, j: (0, j)),
                  table, table],
        out_specs=pl.BlockSpec((tm, PROJ_TN), lambda i, j: (i, j)),
        out_shape=jax.ShapeDtypeStruct((t, n), BF16),
        scratch_shapes=[pltpu.VMEM((tm, d), BF16)],
        compiler_params=_params(2),
        name="head_proj",
    )(x2d, g, w, cos_sets, sin_sets)


def _norm_matmul_kernel(x_ref, g_ref, w_ref, o_ref):
    o_ref[...] = _dot(_rms(x_ref[...], g_ref[...]).astype(BF16), w_ref[...])


def _aux_proj(x2d, g, w, tm=512):
    t, d = x2d.shape
    n = w.shape[1]
    return pl.pallas_call(
        _norm_matmul_kernel,
        grid=(t // tm,),
        in_specs=[pl.BlockSpec((tm, d), lambda i: (i, 0)),
                  pl.BlockSpec((1, d), lambda i: (0, 0)),
                  pl.BlockSpec((d, n), lambda i: (0, 0))],
        out_specs=pl.BlockSpec((tm, n), lambda i: (i, 0)),
        out_shape=jax.ShapeDtypeStruct((t, n), F32),
        compiler_params=_params(1),
        name="aux_proj",
    )(x2d, g, w)


def _sb_kernel(q_ref, k_ref, v_ref, o_ref,
               ls_e, ls_o, hl_e, hl_o, tot_e, tot_o, suf_ref, e_ref, c_ref, acc_ref,
               *, tq, tk, hp, rc):
    assert tq == tk
    i = pl.program_id(2)
    reps = tk // LANES
    heads = range(hp)
    jr = lax.broadcasted_iota(jnp.int32, (tk, tk), 0)
    jc = lax.broadcasted_iota(jnp.int32, (tk, tk), 1)
    u = jnp.where(jr > jc, 1.0, 0.0).astype(BF16)
    qs = [_head(q_ref, h) for h in heads]
    c_ref[...] = jnp.zeros(c_ref.shape, F32)
    acc_ref[...] = jnp.zeros(acc_ref.shape, F32)

    lo_base = hp * tq

    def prepare(bufs, j, diagonal):
        ls_ref, hl_ref, tot_ref = bufs
        for h in heads:
            z = _dot_nt(qs[h], _head(k_ref, h, _key_rows(j, tk)))
            for r0 in range(0, tq, rc):
                rr = slice(r0, r0 + rc)
                zc = z[rr]
                ls = jnp.minimum(zc, 0.0) - jnp.log2(1.0 + jnp.exp2(-jnp.abs(zc)))
                lk = ls - zc
                if diagonal:
                    row = r0 + lax.broadcasted_iota(jnp.int32, (rc, 1), 0)
                    col = lax.broadcasted_iota(jnp.int32, (1, tk), 1)
                    past = col < row
                    lk = jnp.where(past, lk, 0.0)
                    ls = jnp.where(past, ls, NEG_INF)
                hi = lk.astype(BF16)
                hl_ref[h * tq + r0:h * tq + r0 + rc, :] = hi
                hl_ref[lo_base + h * tq + r0:lo_base + h * tq + r0 + rc, :] = (
                    lk - hi.astype(F32)).astype(BF16)
                ls_ref[h, rr, :] = ls
                tot = jnp.sum(_lane_fold(lk, jnp.add), axis=1, keepdims=True)
                tot_ref[h, rr, :] = jnp.broadcast_to(tot, (rc, LANES))

    def weigh(bufs, j):
        ls_ref, hl_ref, tot_ref = bufs
        suf_ref[...] = _dot(hl_ref[...], u)
        for h in heads:
            for r0 in range(0, tq, rc):
                rr = slice(r0, r0 + rc)
                c = c_ref[h, rr, :]
                suf = (suf_ref[h * tq + r0:h * tq + r0 + rc, :]
                       + suf_ref[lo_base + h * tq + r0:lo_base + h * tq + r0 + rc, :])
                pre = ls_ref[h, rr, :] + suf + jnp.concatenate([c] * reps, axis=1)
                e_ref[h, rr, :] = jnp.exp2(pre).astype(BF16)
                c_ref[h, rr, :] = c + tot_ref[h, rr, :]
            acc_ref[h] += _dot(e_ref[h], _head(v_ref, h, _key_rows(j, tk)))

    even = (ls_e, hl_e, tot_e)
    odd = (ls_o, hl_o, tot_o)
    prepare(even, i, True)

    def pair(p, carry):
        j = i - 2 * p
        prepare(odd, j - 1, False)
        weigh(even, j)
        prepare(even, j - 2, False)
        weigh(odd, j - 1)
        return carry

    n_pair = i // 2
    lax.fori_loop(0, n_pair, pair, 0)
    j = i - 2 * n_pair
    one_more = j > 0

    @pl.when(one_more)
    def _():
        prepare(odd, j - 1, False)

    weigh(even, j)

    @pl.when(one_more)
    def _():
        weigh(odd, j - 1)

    for h in heads:
        o_ref[:, h * HEAD_DIM:(h + 1) * HEAD_DIM] = acc_ref[h].astype(BF16)


def _sb_attention(proj, tq=256, tk=256, hp=4, rc=16):
    b, s, _ = proj.shape
    w = hp * HEAD_DIM
    wide_f32 = pltpu.VMEM((hp, tq, tk), F32)
    wide_bf16 = pltpu.VMEM((hp, tq, tk), BF16)
    stat = pltpu.VMEM((hp, tq, LANES), F32)
    return pl.pallas_call(
        functools.partial(_sb_kernel, tq=tq, tk=tk, hp=hp, rc=rc),
        grid=(b, N_HEADS // hp, s // tq),
        in_specs=[pl.BlockSpec((None, tq, w), lambda bi, hg, i: (bi, i, COL_BQ // hp + hg)),
                  pl.BlockSpec((None, s, w), lambda bi, hg, i: (bi, 0, COL_BK // hp + hg)),
                  pl.BlockSpec((None, s, w), lambda bi, hg, i: (bi, 0, COL_BV // hp + hg))],
        out_specs=pl.BlockSpec((None, tq, w), lambda bi, hg, i: (bi, i, hg)),
        out_shape=jax.ShapeDtypeStruct((b, s, GROUP), BF16),
        scratch_shapes=[wide_f32, wide_f32,
                        pltpu.VMEM((2 * hp * tq, tk), BF16), pltpu.VMEM((2 * hp * tq, tk), BF16),
                        stat, stat, pltpu.VMEM((2 * hp * tq, tk), F32), wide_bf16, stat, stat],
        compiler_params=_params(3),
        name="sb_attention",
    )(proj, proj, proj)


def _fox_cum_kernel(cf_ref, bias_ref, col_ref, row_ref, *, chunk):
    s = cf_ref.shape[0]
    lf = jax.nn.log_sigmoid(cf_ref[...] + bias_ref[...]) * LOG2E
    rr = lax.broadcasted_iota(jnp.int32, (chunk, chunk), 0)
    cc = lax.broadcasted_iota(jnp.int32, (chunk, chunk), 1)
    tri = jnp.where(rr >= cc, 1.0, 0.0).astype(BF16)
    carry = jnp.zeros((1, LANES), F32)
    for c in range(s // chunk):
        pre = _dot3_left(tri, lf[c * chunk:(c + 1) * chunk]) + carry
        col_ref[c * chunk:(c + 1) * chunk, :] = pre
        carry = pre[chunk - 1:chunk, :]
    row_ref[...] = jnp.transpose(col_ref[...])[:8, :]


def _fox_cum(aux, bias_row, chunk=256):
    b, s, _ = aux.shape
    return pl.pallas_call(
        functools.partial(_fox_cum_kernel, chunk=chunk),
        grid=(b,),
        in_specs=[pl.BlockSpec((None, s, LANES), lambda bi: (bi, 0, AUX_F)),
                  pl.BlockSpec((1, LANES), lambda bi: (0, 0))],
        out_specs=[pl.BlockSpec((None, s, LANES), lambda bi: (bi, 0, 0)),
                   pl.BlockSpec((None, 8, s), lambda bi: (bi, 0, 0))],
        out_shape=[jax.ShapeDtypeStruct((b, s, LANES), F32),
                   jax.ShapeDtypeStruct((b, 8, s), F32)],
        compiler_params=_params(1),
        name="fox_cum",
    )(aux, bias_row)


def _fox_kernel(q_ref, k_ref, v_ref, cq_ref, ck_ref, o_ref, *bufs, tq, tk, hp):
    assert tk >= tq
    hg = pl.program_id(1)
    i = pl.program_id(2)
    lane = lax.broadcasted_iota(jnp.int32, (1, LANES), 1)
    cum_q = cq_ref[...]
    qs = [_head(q_ref, h) for h in range(hp)]
    cqs = [jnp.sum(jnp.where(lane == hg * hp + h, cum_q, 0.0), axis=1, keepdims=True)
           for h in range(hp)]

    def score(h, j):
        return (_dot_nt(qs[h], _head(k_ref, h, _key_rows(j, tk))) + cqs[h]
                - ck_ref[h, pl.ds(j, 1), :])

    def value(h, j):
        return _head(v_ref, h, _key_rows(j, tk))

    def causal(j, s, r0):
        row = i * tq + r0 + lax.broadcasted_iota(jnp.int32, (s.shape[0], 1), 0)
        col = j * tk + lax.broadcasted_iota(jnp.int32, (1, tk), 1)
        return jnp.where(col <= row, s, NEG_INF)

    outs = _flash_sweep_ref(0, (i * tq) // tk, hp, score, value, causal, tq, tk, bufs)
    for h in range(hp):
        o_ref[:, h * HEAD_DIM:(h + 1) * HEAD_DIM] = outs[h].astype(BF16)


def _fox_attention(proj, cum_col, cum_row, tq=256, tk=512, hp=4):
    b, s, _ = proj.shape
    w = hp * HEAD_DIM
    ck = cum_row.reshape(b, 8, s // tk, tk)
    return pl.pallas_call(
        functools.partial(_fox_kernel, tq=tq, tk=tk, hp=hp),
        grid=(b, N_HEADS // hp, s // tq),
        in_specs=[pl.BlockSpec((None, tq, w), lambda bi, hg, i: (bi, i, COL_CQ // hp + hg)),
                  pl.BlockSpec((None, s, w), lambda bi, hg, i: (bi, 0, COL_CK // hp + hg)),
                  pl.BlockSpec((None, s, w), lambda bi, hg, i: (bi, 0, COL_CV // hp + hg)),
                  pl.BlockSpec((None, tq, LANES), lambda bi, hg, i: (bi, i, 0)),
                  pl.BlockSpec((None, hp, s // tk, tk), lambda bi, hg, i: (bi, hg, 0, 0))],
        out_specs=pl.BlockSpec((None, tq, w), lambda bi, hg, i: (bi, i, hg)),
        out_shape=jax.ShapeDtypeStruct((b, s, GROUP), BF16),
        scratch_shapes=_flash_scratch(hp, tq, tk),
        compiler_params=_params(3),
        name="fox_attention",
    )(proj, proj, proj, cum_col, ck)


def _diff_kernel(q_ref, k_ref, v_ref, dl_ref, gn_ref, o_ref, *bufs, tq, tk, hp):
    assert tk >= tq
    i = pl.program_id(2)
    lane = lax.broadcasted_iota(jnp.int32, (1, LANES), 1)
    zero = jnp.zeros((), BF16)
    qs = []
    for h in range(hp):
        qh = _head(q_ref, h)
        qs.append(jnp.concatenate([jnp.where(lane < DIFF_QK, qh, zero),
                                   jnp.where(lane >= DIFF_QK, qh, zero)], axis=0))

    def score(h, j):
        return _dot_nt(qs[h], _head(k_ref, h, _key_rows(j, tk)))

    def value(h, j):
        return _head(v_ref, h, _key_rows(j, tk))

    def causal(j, s, r0):
        row = i * tq + r0 % tq + lax.broadcasted_iota(jnp.int32, (s.shape[0], 1), 0)
        col = j * tk + lax.broadcasted_iota(jnp.int32, (1, tk), 1)
        return jnp.where(col <= row, s, NEG_INF)

    outs = _flash_sweep_ref(0, (i * tq) // tk, hp, score, value, causal, 2 * tq, tk, bufs,
                            rc=16)
    dl = dl_ref[...]
    lam = (jnp.exp(jnp.sum(dl[0:1] * dl[1:2], axis=1, keepdims=True))
           - jnp.exp(jnp.sum(dl[2:3] * dl[3:4], axis=1, keepdims=True)) + dl[4:5, 0:1])
    for h in range(hp):
        od = outs[h][:tq] - lam * outs[h][tq:]
        o_ref[:, h * HEAD_DIM:(h + 1) * HEAD_DIM] = (
            _rms(od, gn_ref[...]) * dl[5:6, 0:1]).astype(BF16)


def _diff_attention(proj, dl, gn, tq=256, tk=512, hp=4):
    b, s, _ = proj.shape
    w = hp * HEAD_DIM
    return pl.pallas_call(
        functools.partial(_diff_kernel, tq=tq, tk=tk, hp=hp),
        grid=(b, N_HEADS // hp, s // tq),
        in_specs=[pl.BlockSpec((None, tq, w), lambda bi, hg, i: (bi, i, COL_DQ // hp + hg)),
                  pl.BlockSpec((None, s, w), lambda bi, hg, i: (bi, 0, COL_DK // hp + hg)),
                  pl.BlockSpec((None, s, w), lambda bi, hg, i: (bi, 0, COL_DV // hp + hg)),
                  pl.BlockSpec((8, LANES), lambda bi, hg, i: (0, 0)),
                  pl.BlockSpec((1, LANES), lambda bi, hg, i: (0, 0))],
        out_specs=pl.BlockSpec((None, tq, w), lambda bi, hg, i: (bi, i, hg)),
        out_shape=jax.ShapeDtypeStruct((b, s, GROUP), BF16),
        scratch_shapes=_flash_scratch(hp, 2 * tq, tk),
        compiler_params=_params(3),
        name="diff_attention",
    )(proj, proj, proj, dl, gn)


def _cmp_kernel(k2_ref, v2_ref, pek_ref, pev_ref, wk1_ref, wk2_ref, wv1_ref, wv2_ref,
                kc_ref, vc_ref):
    nb = k2_ref.shape[0]
    w = NSA_CMP_STRIDE * HEAD_DIM

    def compress(x2, pe, w1_ref, w2_ref):
        a = _dot((x2 + pe[:, :w]).astype(BF16), w1_ref[:w, :])
        bb = _dot((x2 + pe[:, w:]).astype(BF16), w1_ref[w:, :])
        hid = jax.nn.gelu(a + pltpu.roll(bb, nb - 1, 0), approximate=True)
        return _dot(hid.astype(BF16), w2_ref[...])

    kc_ref[...] = compress(k2_ref[...], pek_ref[...], wk1_ref, wk2_ref).astype(BF16)
    vc_ref[...] = compress(v2_ref[...], pev_ref[...], wv1_ref, wv2_ref).astype(BF16)


def _nsa_compress(k2, v2, pek, pev, wk1, wk2, wv1, wv2):
    b, nb, w = k2.shape
    hid = wk1.shape[1]
    full = lambda *shape: pl.BlockSpec(shape, lambda bi: (0,) * len(shape))
    return pl.pallas_call(
        _cmp_kernel,
        grid=(b,),
        in_specs=[pl.BlockSpec((None, nb, w), lambda bi: (bi, 0, 0)),
                  pl.BlockSpec((None, nb, w), lambda bi: (bi, 0, 0)),
                  full(1, 2 * w), full(1, 2 * w),
                  full(2 * w, hid), full(hid, HEAD_DIM), full(2 * w, hid), full(hid, HEAD_DIM)],
        out_specs=[pl.BlockSpec((None, nb, HEAD_DIM), lambda bi: (bi, 0, 0)),
                   pl.BlockSpec((None, nb, HEAD_DIM), lambda bi: (bi, 0, 0))],
        out_shape=[jax.ShapeDtypeStruct((b, nb, HEAD_DIM), BF16),
                   jax.ShapeDtypeStruct((b, nb, HEAD_DIM), BF16)],
        compiler_params=_params(1),
        name="nsa_compress",
    )(k2, v2, pek, pev, wk1, wk2, wv1, wv2)


def _nsa_kernel(q_ref, qr_ref, g_ref, kc_ref, vc_ref, ks_ref, kw_ref, vs_ref, vw_ref, o_ref,
                *bufs, tq, tk, n_top):
    assert tk >= tq
    i = pl.program_id(1)
    nb = kc_ref.shape[0]
    n_sel = NSA_SLC_BLOCK
    hq = N_HEADS * tq

    q_raw = jnp.concatenate([_head(q_ref, h) for h in range(N_HEADS)], axis=0)
    q_rot = jnp.concatenate([_head(qr_ref, h) for h in range(N_HEADS)], axis=0)
    row = i * tq + lax.broadcasted_iota(jnp.int32, (tq, 1), 0)
    rep = lambda a: jnp.concatenate([a] * N_HEADS, axis=0)

    cidx = lax.broadcasted_iota(jnp.int32, (1, nb), 1)
    c_valid = rep((cidx * NSA_CMP_STRIDE + (NSA_CMP_BLOCK - 1)) <= row)
    sc = jnp.where(c_valid, _dot_nt(q_raw, kc_ref[...]), NEG_INF)
    e = jnp.exp2(sc - jnp.max(sc, axis=1, keepdims=True))
    p = jnp.where(c_valid, e * (1.0 / jnp.sum(e, axis=1, keepdims=True)), 0.0)
    o_cmp = _dot(p.astype(BF16), vc_ref[...])
    p_sum = p[0:tq]
    for h in range(1, N_HEADS):
        p_sum = p_sum + p[h * tq:(h + 1) * tq]
    oc = lax.broadcasted_iota(jnp.int32, (nb, LANES), 0)
    oj = lax.broadcasted_iota(jnp.int32, (nb, LANES), 1)
    per = NSA_SLC_BLOCK // NSA_CMP_STRIDE
    overlap = (jnp.where(oc // per == oj, 1.0, 0.0)
               + jnp.where((oc + 1) // per == oj, 1.0, 0.0)).astype(BF16)
    imp = _dot3(p_sum, overlap)

    imp_t = jnp.transpose(imp)[:n_sel, :]
    jb = lax.broadcasted_iota(jnp.int32, (n_sel, 1), 0)
    cur = (i * tq + lax.broadcasted_iota(jnp.int32, (1, tq), 1)) // NSA_SLC_BLOCK
    sel_valid = jb <= cur
    forced = sel_valid & ((jb == 0) | (jb >= cur - 1))
    score = jnp.where(forced, FORCE_SCORE, jnp.where(sel_valid, imp_t, NEG_INF))
    cnt = jnp.zeros((n_sel, tq), F32)
    for ii in range(n_sel):
        ri = score[ii:ii + 1, :]
        beats = (ri > score) | ((ri == score) & (jb > ii))
        cnt = cnt + jnp.where(beats, 1.0, 0.0)
    sel_t = jnp.where((cnt < n_top) & sel_valid, 1.0, 0.0)
    sel = jnp.transpose(jnp.concatenate([sel_t, jnp.zeros_like(sel_t)], axis=0)).astype(BF16)

    er = lax.broadcasted_iota(jnp.int32, (LANES, tk), 0)
    ec = lax.broadcasted_iota(jnp.int32, (LANES, tk), 1)

    def causal(j, s, r0):
        rows = i * tq + r0 % tq + lax.broadcasted_iota(jnp.int32, (s.shape[0], 1), 0)
        col = j * tk + lax.broadcasted_iota(jnp.int32, (1, tk), 1)
        return jnp.where(col <= rows, s, NEG_INF)

    def slc_score(_, j):
        expand = jnp.where(er == (j * tk + ec) // NSA_SLC_BLOCK, 1.0, 0.0).astype(BF16)
        ok = _dot(sel, expand) > 0.5
        return jnp.where(rep(ok), _dot_nt(q_rot, ks_ref[_key_rows(j, tk), :]), NEG_INF)

    n_full = (i * tq) // tk
    o_slc, = _flash_sweep_ref(0, n_full, 1, slc_score,
                              lambda _, j: vs_ref[_key_rows(j, tk), :], causal, hq, tk, bufs,
                              rc=16)

    def win_score(_, j):
        col = j * tk + lax.broadcasted_iota(jnp.int32, (1, tk), 1)
        return jnp.where(rep(col > row - NSA_WINDOW),
                         _dot_nt(q_rot, kw_ref[_key_rows(j, tk), :]), NEG_INF)

    j_lo = (jnp.maximum(i * tq - NSA_WINDOW, 0) // (2 * tk)) * 2
    o_win, = _flash_sweep_ref(j_lo, n_full, 1, win_score,
                              lambda _, j: vw_ref[_key_rows(j, tk), :], causal, hq, tk, bufs,
                              rc=16)

    g = jax.nn.sigmoid(g_ref[...])
    for h in range(N_HEADS):
        rows = slice(h * tq, (h + 1) * tq)
        o_ref[:, h * HEAD_DIM:(h + 1) * HEAD_DIM] = (
            g[:, 3 * h:3 * h + 1] * o_cmp[rows] + g[:, 3 * h + 1:3 * h + 2] * o_slc[rows]
            + g[:, 3 * h + 2:3 * h + 3] * o_win[rows]).astype(BF16)


def _nsa_attention(proj, aux, kc, vc, tq=256, tk=512):
    b, s, _ = proj.shape
    nb = kc.shape[1]
    n_top = min(NSA_TOPK, s // NSA_SLC_BLOCK)
    kv = lambda col: pl.BlockSpec((None, s, HEAD_DIM), lambda bi, i: (bi, 0, col))
    return pl.pallas_call(
        functools.partial(_nsa_kernel, tq=tq, tk=tk, n_top=n_top),
        grid=(b, s // tq),
        in_specs=[pl.BlockSpec((None, tq, GROUP), lambda bi, i: (bi, i, COL_AQ // N_HEADS)),
                  pl.BlockSpec((None, tq, GROUP), lambda bi, i: (bi, i, COL_AQR // N_HEADS)),
                  pl.BlockSpec((None, tq, LANES), lambda bi, i: (bi, i, AUX_G)),
                  pl.BlockSpec((None, nb, HEAD_DIM), lambda bi, i: (bi, 0, 0)),
                  pl.BlockSpec((None, nb, HEAD_DIM), lambda bi, i: (bi, 0, 0)),
                  kv(COL_AKS), kv(COL_AKW), kv(COL_AVS), kv(COL_AVW)],
        out_specs=pl.BlockSpec((None, tq, GROUP), lambda bi, i: (bi, i, 0)),
        out_shape=jax.ShapeDtypeStruct((b, s, GROUP), BF16),
        scratch_shapes=_flash_scratch(1, N_HEADS * tq, tk),
        compiler_params=_params(2),
        name="nsa_attention",
    )(proj, proj, aux, kc, vc, proj, proj, proj, proj)


def _out_kernel(oa_ref, ob_ref, oc_ref, od_ref, w_ref, x_ref, g_ref, o_ref):
    acc = _dot(oa_ref[...], w_ref[0:GROUP, :])
    acc = acc + _dot(ob_ref[...], w_ref[GROUP:2 * GROUP, :])
    acc = acc + _dot(oc_ref[...], w_ref[2 * GROUP:3 * GROUP, :])
    acc = acc + _dot(od_ref[...], w_ref[3 * GROUP:4 * GROUP, :])
    o_ref[...] = x_ref[...] + _rms(acc, g_ref[...])


def _out_proj(oa, ob, oc, od, w, x2d, g, tm=512):
    t, d = x2d.shape
    mix = pl.BlockSpec((tm, GROUP), lambda i: (i, 0))
    return pl.pallas_call(
        _out_kernel,
        grid=(t // tm,),
        in_specs=[mix, mix, mix, mix,
                  pl.BlockSpec((4 * GROUP, d), lambda i: (0, 0)),
                  pl.BlockSpec((tm, d), lambda i: (i, 0)),
                  pl.BlockSpec((1, d), lambda i: (0, 0))],
        out_specs=pl.BlockSpec((tm, d), lambda i: (i, 0)),
        out_shape=jax.ShapeDtypeStruct((t, d), F32),
        compiler_params=_params(1),
        name="out_proj",
    )(oa, ob, oc, od, w, x2d, g)


HALO = 16


def _mlp_kernel(x_ref, xh_ref, gpre_ref, wg_ref, wu_ref, cwg_ref, cwu_ref, cbg_ref, cbu_ref,
                wd_ref, gpost_ref, o_ref, xn_ref, acc_ref, *, tm, tiles_per_seq):
    i = pl.program_id(0)
    f = pl.program_id(1)

    @pl.when(f == 0)
    def _():
        xn_ref[HALO:, :] = _rms(x_ref[...], gpre_ref[...]).astype(BF16)
        halo = _rms(xh_ref[...], gpre_ref[...])
        seq_start = (i % tiles_per_seq) == 0
        xn_ref[:HALO, :] = jnp.where(seq_start, 0.0, halo).astype(BF16)
        acc_ref[...] = jnp.zeros_like(acc_ref)

    xn = xn_ref[...]

    def conv(u, cw_ref, cb_ref):
        cw = cw_ref[...]
        out = cb_ref[...] + cw[0:1] * u[HALO - 2:HALO - 2 + tm]
        out = out + cw[1:2] * u[HALO - 1:HALO - 1 + tm]
        return out + cw[2:3] * u[HALO:HALO + tm]

    gate = conv(_dot(xn, wg_ref[...]), cwg_ref, cbg_ref)
    up = conv(_dot(xn, wu_ref[...]), cwu_ref, cbu_ref)
    hid = (jax.nn.gelu(gate, approximate=True) * up).astype(BF16)
    acc_ref[...] += _dot(hid, wd_ref[...])

    @pl.when(f == pl.num_programs(1) - 1)
    def _():
        o_ref[...] = x_ref[...] + _rms(acc_ref[...], gpost_ref[...])


def _mlp(x2d, gpre, w_up, conv_w, conv_b, w_down, gpost, seq, tm=1024, tf=512):
    t, d = x2d.shape
    ff = w_down.shape[0]
    nf = ff // tf
    hb = tm // HALO
    once = pl.Buffered(1)
    return pl.pallas_call(
        functools.partial(_mlp_kernel, tm=tm, tiles_per_seq=seq // tm),
        grid=(t // tm, nf),
        in_specs=[pl.BlockSpec((tm, d), lambda i, f: (i, 0), pipeline_mode=once),
                  pl.BlockSpec((HALO, d), lambda i, f: (jnp.maximum(i * hb - 1, 0), 0)),
                  pl.BlockSpec((1, d), lambda i, f: (0, 0)),
                  pl.BlockSpec((d, tf), lambda i, f: (0, f)),
                  pl.BlockSpec((d, tf), lambda i, f: (0, nf + f)),
                  pl.BlockSpec((3, tf), lambda i, f: (0, f)),
                  pl.BlockSpec((3, tf), lambda i, f: (0, nf + f)),
                  pl.BlockSpec((1, tf), lambda i, f: (0, f)),
                  pl.BlockSpec((1, tf), lambda i, f: (0, nf + f)),
                  pl.BlockSpec((tf, d), lambda i, f: (f, 0)),
                  pl.BlockSpec((1, d), lambda i, f: (0, 0))],
        out_specs=pl.BlockSpec((tm, d), lambda i, f: (i, 0), pipeline_mode=once),
        out_shape=jax.ShapeDtypeStruct((t, d), F32),
        scratch_shapes=[pltpu.VMEM((tm + HALO, d), BF16), pltpu.VMEM((tm, d), F32)],
        compiler_params=_params(2),
        name="conv_geglu_mlp",
    )(x2d, x2d, gpre, w_up, w_up, conv_w, conv_w, conv_b, conv_b, w_down, gpost)


def _split_w_in(w):
    off = np.concatenate([[0], np.cumsum(_IN_SPLITS)]).tolist()
    cols = lambda first, last: w[:, :, off[first]:off[last + 1]].astype(BF16)
    a_q = cols(0, 0)
    heads = jnp.concatenate([a_q, a_q, cols(3, 6), cols(8, 13), cols(15, 17)], axis=-1)
    pad = lambda p: jnp.pad(p, ((0, 0), (0, 0), (0, LANES - p.shape[-1])))
    aux = jnp.concatenate([cols(1, 2), pad(cols(7, 7)), pad(cols(14, 14))], axis=-1)
    return heads, aux


def _rope_lane_tables(seq, rot_dim, period):
    pos = jnp.arange(seq, dtype=F32)
    inv = ROPE_THETA ** (-jnp.arange(0, rot_dim, 2, dtype=F32) / rot_dim)
    ang = pos[:, None] * inv[None, :]
    cos, sin = jnp.cos(ang), jnp.sin(ang)
    rest = period - rot_dim
    c = jnp.concatenate([cos, cos, jnp.ones((seq, rest), F32)], axis=-1)
    s = jnp.concatenate([-sin, sin, jnp.zeros((seq, rest), F32)], axis=-1)
    reps = LANES // period
    return jnp.tile(c, (1, reps)), jnp.tile(s, (1, reps))


@jax.jit
def _forward(x, norm_mix_pre, norm_mix_post, norm_mlp_pre, norm_mlp_post, w_in, w_out,
             nsa_pe_k, nsa_pe_v, nsa_cmp_k1, nsa_cmp_k2, nsa_cmp_v1, nsa_cmp_v2,
             fox_forget_bias, diff_lambda, diff_norm, mlp_w_up, mlp_conv_w, mlp_conv_b,
             mlp_w_down):
    b, s, d = x.shape
    depth = w_in.shape[0]
    cos_h, sin_h = _rope_lane_tables(s, HEAD_DIM // 4, LANES)
    cos_d, sin_d = _rope_lane_tables(s, DIFF_QK // 4, DIFF_QK)
    cos_sets = jnp.stack([cos_h * SCALE_HEAD, cos_h, cos_d * SCALE_DIFF, cos_d])
    sin_sets = jnp.stack([sin_h * SCALE_HEAD, sin_h, sin_d * SCALE_DIFF, sin_d])
    w_heads, w_aux = _split_w_in(w_in)
    w_out_b = w_out.astype(BF16)
    w_up_b = mlp_w_up.astype(BF16)
    w_down_b = mlp_w_down.astype(BF16)
    k1_b, k2_b = nsa_cmp_k1.astype(BF16), nsa_cmp_k2.astype(BF16)
    v1_b, v2_b = nsa_cmp_v1.astype(BF16), nsa_cmp_v2.astype(BF16)
    fox_bias = jnp.pad(fox_forget_bias, ((0, 0), (0, LANES - N_HEADS)))
    row2 = lambda a: a.reshape(1, -1)
    cmp_w = NSA_CMP_STRIDE * HEAD_DIM

    xt = x.reshape(b * s, d)
    for l in range(depth):
        g_pre = row2(norm_mix_pre[l])
        proj = _head_proj(xt, g_pre, w_heads[l], cos_sets, sin_sets, s, tm=min(1024, s))
        proj = proj.reshape(b, s, HEAD_BLOCKS * LANES)
        aux = _aux_proj(xt, g_pre, w_aux[l]).reshape(b, s, AUX_BLOCKS * LANES)

        k2 = aux[:, :, AUX_KC * LANES:(AUX_KC + 1) * LANES].reshape(b, s // NSA_CMP_STRIDE, cmp_w)
        v2 = aux[:, :, AUX_VC * LANES:(AUX_VC + 1) * LANES].reshape(b, s // NSA_CMP_STRIDE, cmp_w)
        kc, vc = _nsa_compress(k2, v2, row2(nsa_pe_k[l]), row2(nsa_pe_v[l]),
                               k1_b[l], k2_b[l], v1_b[l], v2_b[l])
        o_a = _nsa_attention(proj, aux, kc, vc)

        o_b = _sb_attention(proj)

        cum_col, cum_row = _fox_cum(aux, fox_bias[l:l + 1])
        o_c = _fox_attention(proj, cum_col, cum_row)

        lam_init = 0.8 - 0.6 * math.exp(-0.3 * l)
        dl = jnp.concatenate([
            jnp.pad(diff_lambda[l], ((0, 0), (0, LANES - DIFF_QK))),
            jnp.full((1, LANES), lam_init, F32),
            jnp.full((1, LANES), 1.0 - lam_init, F32),
            jnp.zeros((2, LANES), F32)], axis=0)
        o_d = _diff_attention(proj, dl, row2(diff_norm[l]))

        flat = lambda o: o.reshape(b * s, GROUP)
        xt = _out_proj(flat(o_a), flat(o_b), flat(o_c), flat(o_d), w_out_b[l], xt,
                       row2(norm_mix_post[l]))
        xt = _mlp(xt, row2(norm_mlp_pre[l]), w_up_b[l], mlp_conv_w[l], row2(mlp_conv_b[l]),
                  w_down_b[l], row2(norm_mlp_post[l]), s, tm=min(1024, s))
    return xt.reshape(b, s, d)


def kernel(x, norm_mix_pre, norm_mix_post, norm_mlp_pre, norm_mlp_post, w_in, w_out, nsa_pe_k, nsa_pe_v, nsa_cmp_k1, nsa_cmp_k2, nsa_cmp_v1, nsa_cmp_v2, fox_forget_bias, diff_lambda, diff_norm, mlp_w_up, mlp_conv_w, mlp_conv_b, mlp_w_down):
    return _forward(x, norm_mix_pre, norm_mix_post, norm_mlp_pre, norm_mlp_post, w_in, w_out,
                    nsa_pe_k, nsa_pe_v, nsa_cmp_k1, nsa_cmp_k2, nsa_cmp_v1, nsa_cmp_v2,
                    fox_forget_bias, diff_lambda, diff_norm, mlp_w_up, mlp_conv_w, mlp_conv_b,
                    mlp_w_down)
```

```python
import functools
import math

import jax
import jax.numpy as jnp
import numpy as np
from jax import lax
from jax.experimental import pallas as pl
from jax.experimental.pallas import tpu as pltpu

F32 = jnp.float32
BF16 = jnp.bfloat16

LANES = 128
HEAD_DIM = 128
N_HEADS = 4
GROUP = N_HEADS * HEAD_DIM
NSA_CMP_BLOCK = 32
NSA_CMP_STRIDE = 16
NSA_SLC_BLOCK = 64
NSA_TOPK = 16
NSA_WINDOW = 512
DIFF_QK = HEAD_DIM // 2
ROPE_HALF_HEAD = HEAD_DIM // 4 // 2
ROPE_HALF_DIFF = DIFF_QK // 4 // 2
EPS = 1e-6
NEG_INF = -1e30
FORCE_SCORE = 1e9
ROPE_THETA = 500000.0
LOG2E = math.log2(math.e)
SCALE_HEAD = HEAD_DIM ** -0.5 * LOG2E
SCALE_DIFF = DIFF_QK ** -0.5 * LOG2E

_IN_SPLITS = (GROUP, 128, 128, 128, 128, 128, 128, 3 * N_HEADS,
              GROUP, GROUP, GROUP, GROUP, GROUP, GROUP, N_HEADS, GROUP, GROUP, GROUP)
COL_AQ, COL_AQR, COL_AKS, COL_AVS, COL_AKW, COL_AVW = 0, 4, 8, 9, 10, 11
COL_BQ, COL_BK, COL_BV = 12, 16, 20
COL_CQ, COL_CK, COL_CV = 24, 28, 32
COL_DQ, COL_DK, COL_DV = 36, 40, 44
HEAD_BLOCKS = 48
AUX_KC, AUX_VC, AUX_G, AUX_F = 0, 1, 2, 3
AUX_BLOCKS = 4

PROJ_TN = GROUP
_TILES_SCALE = (0, 3, 6)
_TILE_ROPE_Q = 1
_TILE_ROPE_KV = 2
_TILES_ROPE_DIFF = (9, 10)
_ROPE_SET_STARTS = (2, 9, 10)

VMEM_LIMIT = 56 * 1024 * 1024


def _params(n_axes, vmem=VMEM_LIMIT):
    return pltpu.CompilerParams(dimension_semantics=("arbitrary",) * n_axes,
                                vmem_limit_bytes=vmem)


def _rms(x, g):
    ms = jnp.mean(x * x, axis=-1, keepdims=True)
    return x * lax.rsqrt(ms + EPS) * g


def _dot(a, b):
    return jnp.dot(a, b, preferred_element_type=F32)


def _dot_nt(a, b):
    return lax.dot_general(a, b, (((1,), (1,)), ((), ())), preferred_element_type=F32)


def _split3(x):
    hi = x.astype(BF16)
    r = x - hi.astype(F32)
    mid = r.astype(BF16)
    lo = (r - mid.astype(F32)).astype(BF16)
    return hi, mid, lo


def _dot3(x, m):
    hi, mid, lo = _split3(x)
    return _dot(hi, m) + _dot(mid, m) + _dot(lo, m)


def _dot2(x, m):
    hi = x.astype(BF16)
    lo = (x - hi.astype(F32)).astype(BF16)
    return _dot(hi, m) + _dot(lo, m)


def _dot3_left(m, x):
    hi, mid, lo = _split3(x)
    return _dot(m, hi) + _dot(m, mid) + _dot(m, lo)


def _rope(x, c, s, half):
    width = x.shape[1]
    lane = lax.broadcasted_iota(jnp.int32, (1, width), 1)
    first = (lane % (2 * half)) < half
    xs = jnp.where(first, pltpu.roll(x, width - half, 1), pltpu.roll(x, half, 1))
    return x * c + xs * s


def _lane_fold(x, op):
    out = x[:, :LANES]
    for b in range(1, x.shape[1] // LANES):
        out = op(out, x[:, b * LANES:(b + 1) * LANES])
    return out


def _row_max(s):
    return jnp.max(_lane_fold(s, jnp.maximum), axis=1, keepdims=True)


def _flash_scratch(n_chain, rows, tk):
    stat = pltpu.VMEM((n_chain, rows, LANES), F32)
    return [pltpu.VMEM((n_chain, rows, tk), F32), pltpu.VMEM((n_chain, rows, tk), F32),
            pltpu.VMEM((n_chain, rows, tk), BF16), stat, stat,
            pltpu.VMEM((n_chain, rows, 2 * LANES), F32)]


def _flash_sweep_ref(lo, n_full, n_chain, score_fn, value_fn, mask_fn, rows, tk, bufs, rc=16):
    s_even, s_odd, p_buf, m_ref, a_ref, acc_ref = bufs
    chains = range(n_chain)
    reps = tk // LANES
    ones = jnp.ones((tk, LANES), BF16)
    m_ref[...] = jnp.full(m_ref.shape, NEG_INF, F32)
    acc_ref[...] = jnp.zeros(acc_ref.shape, F32)

    def scores(s_buf, j):
        for c in chains:
            s_buf[c] = score_fn(c, j)

    def consume(s_buf, j, masked):
        for c in chains:
            for r0 in range(0, rows, rc):
                rr = slice(r0, r0 + rc)
                s = s_buf[c, rr, :]
                if masked:
                    s = mask_fn(j, s, r0)
                m_old = m_ref[c, rr, :]
                m_new = jnp.maximum(m_old, _row_max(s))
                m_ref[c, rr, :] = m_new
                a_ref[c, rr, :] = jnp.exp2(m_old - m_new)
                shifted = s - jnp.concatenate([m_new] * reps, axis=1)
                p_buf[c, rr, :] = jnp.exp2(shifted.astype(BF16))
            v_ones = jnp.concatenate([value_fn(c, j), ones], axis=1)
            alpha = a_ref[c]
            acc_ref[c] = (jnp.concatenate([alpha, alpha], axis=1) * acc_ref[c]
                          + _dot(p_buf[c], v_ones))

    scores(s_even, lo)

    def pair(t, carry):
        j = lo + 2 * t
        scores(s_odd, j + 1)
        consume(s_even, j, False)
        scores(s_even, j + 2)
        consume(s_odd, j + 1, False)
        return carry

    n_pair = (n_full - lo) // 2
    lax.fori_loop(0, n_pair, pair, 0)
    j = lo + 2 * n_pair
    last_full = j < n_full

    @pl.when(last_full)
    def _():
        scores(s_odd, j + 1)

    consume(s_even, j, True)

    @pl.when(last_full)
    def _():
        consume(s_odd, j + 1, True)

    return [acc_ref[c, :, :LANES] * (1.0 / acc_ref[c, :, LANES:]) for c in chains]


def _head(ref, h, rows=None):
    cols = slice(h * HEAD_DIM, (h + 1) * HEAD_DIM)
    return ref[:, cols] if rows is None else ref[rows, cols]


def _key_rows(j, tk):
    return pl.ds(pl.multiple_of(j * tk, tk), tk)


def _tile_in(j, tiles):
    hit = j == tiles[0]
    for t in tiles[1:]:
        hit = hit | (j == t)
    return hit


def _head_proj_kernel(x_ref, g_ref, w_ref, c_ref, s_ref, o_ref, xn_ref):
    j = pl.program_id(1)

    @pl.when(j == 0)
    def _():
        xn_ref[...] = _rms(x_ref[...], g_ref[...]).astype(BF16)

    acc = _dot(xn_ref[...], w_ref[...])
    is_scale = _tile_in(j, _TILES_SCALE)
    is_rope_d = _tile_in(j, _TILES_ROPE_DIFF)
    is_rope = (j == _TILE_ROPE_Q) | (j == _TILE_ROPE_KV) | is_rope_d

    @pl.when(is_scale)
    def _():
        o_ref[...] = (acc * SCALE_HEAD).astype(BF16)

    def rope_tile(half, rotated):
        ones = jnp.ones(c_ref.shape, F32)
        c = jnp.concatenate([c_ref[...] if r else ones for r in rotated], axis=1)
        s = jnp.concatenate([s_ref[...] if r else 0.0 * ones for r in rotated], axis=1)
        o_ref[...] = _rope(acc, c, s, half).astype(BF16)

    every = (True,) * N_HEADS
    pl.when(j == _TILE_ROPE_Q)(lambda: rope_tile(ROPE_HALF_HEAD, every))
    pl.when(j == _TILE_ROPE_KV)(lambda: rope_tile(ROPE_HALF_HEAD, (True, False, True, False)))
    pl.when(is_rope_d)(lambda: rope_tile(ROPE_HALF_DIFF, every))

    @pl.when(jnp.logical_not(is_scale | is_rope))
    def _():
        o_ref[...] = acc.astype(BF16)


def _rope_set(j):
    idx = 0
    for start in _ROPE_SET_STARTS:
        idx = idx + (j >= start).astype(jnp.int32)
    return idx


def _head_proj(x2d, g, w, layer, cos_sets, sin_sets, seq, tm=1024):
    t, d = x2d.shape
    n = w.shape[2]
    per_seq = seq // tm
    table = pl.BlockSpec((None, tm, LANES), lambda i, j: (_rope_set(j), i % per_seq, 0))
    return pl.pallas_call(
        _head_proj_kernel,
        grid=(t // tm, n // PROJ_TN),
        in_specs=[pl.BlockSpec((tm, d), lambda i, j: (i, 0)),
                  pl.BlockSpec((1, d), lambda i, j: (0, 0)),
                  pl.BlockSpec((None, d, PROJ_TN), lambda i, j: (layer, 0, j)),
                  table, table],
        out_specs=pl.BlockSpec((tm, PROJ_TN), lambda i, j: (i, j)),
        out_shape=jax.ShapeDtypeStruct((t, n), BF16),
        scratch_shapes=[pltpu.VMEM((tm, d), BF16)],
        compiler_params=_params(2),
        name="head_proj",
    )(x2d, g, w, cos_sets, sin_sets)


def _norm_matmul_kernel(x_ref, g_ref, w_ref, o_ref):
    o_ref[...] = _dot(_rms(x_ref[...], g_ref[...]).astype(BF16), w_ref[...])


def _aux_proj(x2d, g, w, layer, tm=512):
    t, d = x2d.shape
    n = w.shape[2]
    return pl.pallas_call(
        _norm_matmul_kernel,
        grid=(t // tm,),
        in_specs=[pl.BlockSpec((tm, d), lambda i: (i, 0)),
                  pl.BlockSpec((1, d), lambda i: (0, 0)),
                  pl.BlockSpec((None, d, n), lambda i: (layer, 0, 0))],
        out_specs=pl.BlockSpec((tm, n), lambda i: (i, 0)),
        out_shape=jax.ShapeDtypeStruct((t, n), F32),
        compiler_params=_params(1),
        name="aux_proj",
    )(x2d, g, w)


def _sb_kernel(q_ref, k_ref, v_ref, o_ref,
               ls_e, ls_o, hl_e, hl_o, tot_e, tot_o, suf_ref, e_ref, c_ref, acc_ref,
               *, tq, tk, hp, rc):
    assert tq == tk
    i = pl.program_id(2)
    reps = tk // LANES
    heads = range(hp)
    jr = lax.broadcasted_iota(jnp.int32, (tk, tk), 0)
    jc = lax.broadcasted_iota(jnp.int32, (tk, tk), 1)
    u = jnp.where(jr > jc, 1.0, 0.0).astype(BF16)
    qs = [_head(q_ref, h) for h in heads]
    c_ref[...] = jnp.zeros(c_ref.shape, F32)
    acc_ref[...] = jnp.zeros(acc_ref.shape, F32)

    lo_base = hp * tq

    def prepare(bufs, j, diagonal):
        ls_ref, hl_ref, tot_ref = bufs
        for h in heads:
            z = _dot_nt(qs[h], _head(k_ref, h, _key_rows(j, tk)))
            for r0 in range(0, tq, rc):
                rr = slice(r0, r0 + rc)
                zc = z[rr]
                ls = jnp.minimum(zc, 0.0) - jnp.log2(1.0 + jnp.exp2(-jnp.abs(zc)))
                lk = ls - zc
                if diagonal:
                    row = r0 + lax.broadcasted_iota(jnp.int32, (rc, 1), 0)
                    col = lax.broadcasted_iota(jnp.int32, (1, tk), 1)
                    past = col < row
                    lk = jnp.where(past, lk, 0.0)
                    ls = jnp.where(past, ls, NEG_INF)
                hi = lk.astype(BF16)
                hl_ref[h * tq + r0:h * tq + r0 + rc, :] = hi
                hl_ref[lo_base + h * tq + r0:lo_base + h * tq + r0 + rc, :] = (
                    lk - hi.astype(F32)).astype(BF16)
                ls_ref[h, rr, :] = ls
                tot = jnp.sum(_lane_fold(lk, jnp.add), axis=1, keepdims=True)
                tot_ref[h, rr, :] = jnp.broadcast_to(tot, (rc, LANES))

    def weigh(bufs, j):
        ls_ref, hl_ref, tot_ref = bufs
        suf_ref[...] = _dot(hl_ref[...], u)
        for h in heads:
            for r0 in range(0, tq, rc):
                rr = slice(r0, r0 + rc)
                c = c_ref[h, rr, :]
                suf = (suf_ref[h * tq + r0:h * tq + r0 + rc, :]
                       + suf_ref[lo_base + h * tq + r0:lo_base + h * tq + r0 + rc, :])
                pre = ls_ref[h, rr, :] + suf + jnp.concatenate([c] * reps, axis=1)
                e_ref[h, rr, :] = jnp.exp2(pre).astype(BF16)
                c_ref[h, rr, :] = c + tot_ref[h, rr, :]
            acc_ref[h] += _dot(e_ref[h], _head(v_ref, h, _key_rows(j, tk)))

    even = (ls_e, hl_e, tot_e)
    odd = (ls_o, hl_o, tot_o)
    prepare(even, i, True)

    def pair(p, carry):
        j = i - 2 * p
        prepare(odd, j - 1, False)
        weigh(even, j)
        prepare(even, j - 2, False)
        weigh(odd, j - 1)
        return carry

    n_pair = i // 2
    lax.fori_loop(0, n_pair, pair, 0)
    j = i - 2 * n_pair
    one_more = j > 0

    @pl.when(one_more)
    def _():
        prepare(odd, j - 1, False)

    weigh(even, j)

    @pl.when(one_more)
    def _():
        weigh(odd, j - 1)

    for h in heads:
        o_ref[:, h * HEAD_DIM:(h + 1) * HEAD_DIM] = acc_ref[h].astype(BF16)


def _sb_attention(proj, tq=256, tk=256, hp=4, rc=16):
    b, s, _ = proj.shape
    w = hp * HEAD_DIM
    wide_f32 = pltpu.VMEM((hp, tq, tk), F32)
    wide_bf16 = pltpu.VMEM((hp, tq, tk), BF16)
    stat = pltpu.VMEM((hp, tq, LANES), F32)
    return pl.pallas_call(
        functools.partial(_sb_kernel, tq=tq, tk=tk, hp=hp, rc=rc),
        grid=(b, N_HEADS // hp, s // tq),
        in_specs=[pl.BlockSpec((None, tq, w), lambda bi, hg, i: (bi, i, COL_BQ // hp + hg)),
                  pl.BlockSpec((None, s, w), lambda bi, hg, i: (bi, 0, COL_BK // hp + hg)),
                  pl.BlockSpec((None, s, w), lambda bi, hg, i: (bi, 0, COL_BV // hp + hg))],
        out_specs=pl.BlockSpec((None, tq, w), lambda bi, hg, i: (bi, i, hg)),
        out_shape=jax.ShapeDtypeStruct((b, s, GROUP), BF16),
        scratch_shapes=[wide_f32, wide_f32,
                        pltpu.VMEM((2 * hp * tq, tk), BF16), pltpu.VMEM((2 * hp * tq, tk), BF16),
                        stat, stat, pltpu.VMEM((2 * hp * tq, tk), F32), wide_bf16, stat, stat],
        compiler_params=_params(3),
        name="sb_attention",
    )(proj, proj, proj)


def _fox_cum_kernel(cf_ref, bias_ref, col_ref, row_ref, *, chunk):
    s = cf_ref.shape[0]
    lf = jax.nn.log_sigmoid(cf_ref[...] + bias_ref[...]) * LOG2E
    rr = lax.broadcasted_iota(jnp.int32, (chunk, chunk), 0)
    cc = lax.broadcasted_iota(jnp.int32, (chunk, chunk), 1)
    tri = jnp.where(rr >= cc, 1.0, 0.0).astype(BF16)
    carry = jnp.zeros((1, LANES), F32)
    for c in range(s // chunk):
        pre = _dot3_left(tri, lf[c * chunk:(c + 1) * chunk]) + carry
        col_ref[c * chunk:(c + 1) * chunk, :] = pre
        carry = pre[chunk - 1:chunk, :]
    row_ref[...] = jnp.transpose(col_ref[...])[:8, :]


def _fox_cum(aux, bias_row, chunk=256):
    b, s, _ = aux.shape
    return pl.pallas_call(
        functools.partial(_fox_cum_kernel, chunk=chunk),
        grid=(b,),
        in_specs=[pl.BlockSpec((None, s, LANES), lambda bi: (bi, 0, AUX_F)),
                  pl.BlockSpec((1, LANES), lambda bi: (0, 0))],
        out_specs=[pl.BlockSpec((None, s, LANES), lambda bi: (bi, 0, 0)),
                   pl.BlockSpec((None, 8, s), lambda bi: (bi, 0, 0))],
        out_shape=[jax.ShapeDtypeStruct((b, s, LANES), F32),
                   jax.ShapeDtypeStruct((b, 8, s), F32)],
        compiler_params=_params(1),
        name="fox_cum",
    )(aux, bias_row)


def _fox_kernel(q_ref, k_ref, v_ref, cq_ref, ck_ref, o_ref, *bufs, tq, tk, hp):
    assert tk >= tq
    hg = pl.program_id(1)
    i = pl.program_id(2)
    lane = lax.broadcasted_iota(jnp.int32, (1, LANES), 1)
    cum_q = cq_ref[...]
    qs = [_head(q_ref, h) for h in range(hp)]
    cqs = [jnp.sum(jnp.where(lane == hg * hp + h, cum_q, 0.0), axis=1, keepdims=True)
           for h in range(hp)]

    def score(h, j):
        return (_dot_nt(qs[h], _head(k_ref, h, _key_rows(j, tk))) + cqs[h]
                - ck_ref[h, pl.ds(j, 1), :])

    def value(h, j):
        return _head(v_ref, h, _key_rows(j, tk))

    def causal(j, s, r0):
        row = i * tq + r0 + lax.broadcasted_iota(jnp.int32, (s.shape[0], 1), 0)
        col = j * tk + lax.broadcasted_iota(jnp.int32, (1, tk), 1)
        return jnp.where(col <= row, s, NEG_INF)

    outs = _flash_sweep_ref(0, (i * tq) // tk, hp, score, value, causal, tq, tk, bufs)
    for h in range(hp):
        o_ref[:, h * HEAD_DIM:(h + 1) * HEAD_DIM] = outs[h].astype(BF16)


def _fox_attention(proj, cum_col, cum_row, tq=256, tk=512, hp=4):
    b, s, _ = proj.shape
    w = hp * HEAD_DIM
    ck = cum_row.reshape(b, 8, s // tk, tk)
    return pl.pallas_call(
        functools.partial(_fox_kernel, tq=tq, tk=tk, hp=hp),
        grid=(b, N_HEADS // hp, s // tq),
        in_specs=[pl.BlockSpec((None, tq, w), lambda bi, hg, i: (bi, i, COL_CQ // hp + hg)),
                  pl.BlockSpec((None, s, w), lambda bi, hg, i: (bi, 0, COL_CK // hp + hg)),
                  pl.BlockSpec((None, s, w), lambda bi, hg, i: (bi, 0, COL_CV // hp + hg)),
                  pl.BlockSpec((None, tq, LANES), lambda bi, hg, i: (bi, i, 0)),
                  pl.BlockSpec((None, hp, s // tk, tk), lambda bi, hg, i: (bi, hg, 0, 0))],
        out_specs=pl.BlockSpec((None, tq, w), lambda bi, hg, i: (bi, i, hg)),
        out_shape=jax.ShapeDtypeStruct((b, s, GROUP), BF16),
        scratch_shapes=_flash_scratch(hp, tq, tk),
        compiler_params=_params(3),
        name="fox_attention",
    )(proj, proj, proj, cum_col, ck)


def _diff_kernel(q_ref, k_ref, v_ref, dl_ref, gn_ref, o_ref, *bufs, tq, tk, hp):
    assert tk >= tq
    i = pl.program_id(2)
    lane = lax.broadcasted_iota(jnp.int32, (1, LANES), 1)
    zero = jnp.zeros((), BF16)
    qs = []
    for h in range(hp):
        qh = _head(q_ref, h)
        qs.append(jnp.concatenate([jnp.where(lane < DIFF_QK, qh, zero),
                                   jnp.where(lane >= DIFF_QK, qh, zero)], axis=0))

    def score(h, j):
        return _dot_nt(qs[h], _head(k_ref, h, _key_rows(j, tk)))

    def value(h, j):
        return _head(v_ref, h, _key_rows(j, tk))

    def causal(j, s, r0):
        row = i * tq + r0 % tq + lax.broadcasted_iota(jnp.int32, (s.shape[0], 1), 0)
        col = j * tk + lax.broadcasted_iota(jnp.int32, (1, tk), 1)
        return jnp.where(col <= row, s, NEG_INF)

    outs = _flash_sweep_ref(0, (i * tq) // tk, hp, score, value, causal, 2 * tq, tk, bufs,
                            rc=16)
    dl = dl_ref[...]
    lam = (jnp.exp(jnp.sum(dl[0:1] * dl[1:2], axis=1, keepdims=True))
           - jnp.exp(jnp.sum(dl[2:3] * dl[3:4], axis=1, keepdims=True)) + dl[4:5, 0:1])
    for h in range(hp):
        od = outs[h][:tq] - lam * outs[h][tq:]
        o_ref[:, h * HEAD_DIM:(h + 1) * HEAD_DIM] = (
            _rms(od, gn_ref[...]) * dl[5:6, 0:1]).astype(BF16)


def _diff_attention(proj, dl, gn, tq=256, tk=512, hp=4):
    b, s, _ = proj.shape
    w = hp * HEAD_DIM
    return pl.pallas_call(
        functools.partial(_diff_kernel, tq=tq, tk=tk, hp=hp),
        grid=(b, N_HEADS // hp, s // tq),
        in_specs=[pl.BlockSpec((None, tq, w), lambda bi, hg, i: (bi, i, COL_DQ // hp + hg)),
                  pl.BlockSpec((None, s, w), lambda bi, hg, i: (bi, 0, COL_DK // hp + hg)),
                  pl.BlockSpec((None, s, w), lambda bi, hg, i: (bi, 0, COL_DV // hp + hg)),
                  pl.BlockSpec((8, LANES), lambda bi, hg, i: (0, 0)),
                  pl.BlockSpec((1, LANES), lambda bi, hg, i: (0, 0))],
        out_specs=pl.BlockSpec((None, tq, w), lambda bi, hg, i: (bi, i, hg)),
        out_shape=jax.ShapeDtypeStruct((b, s, GROUP), BF16),
        scratch_shapes=_flash_scratch(hp, 2 * tq, tk),
        compiler_params=_params(3),
        name="diff_attention",
    )(proj, proj, proj, dl, gn)


def _cmp_kernel(k2_ref, v2_ref, pek_ref, pev_ref, wk1_ref, wk2_ref, wv1_ref, wv2_ref,
                kc_ref, vc_ref):
    nb = k2_ref.shape[0]
    w = NSA_CMP_STRIDE * HEAD_DIM

    def compress(x2, pe, w1_ref, w2_ref):
        a = _dot((x2 + pe[:, :w]).astype(BF16), w1_ref[:w, :])
        bb = _dot((x2 + pe[:, w:]).astype(BF16), w1_ref[w:, :])
        hid = jax.nn.gelu(a + pltpu.roll(bb, nb - 1, 0), approximate=True)
        return _dot(hid.astype(BF16), w2_ref[...])

    kc_ref[...] = compress(k2_ref[...], pek_ref[...], wk1_ref, wk2_ref).astype(BF16)
    vc_ref[...] = compress(v2_ref[...], pev_ref[...], wv1_ref, wv2_ref).astype(BF16)


def _nsa_compress(k2, v2, pek, pev, wk1, wk2, wv1, wv2):
    b, nb, w = k2.shape
    hid = wk1.shape[1]
    full = lambda *shape: pl.BlockSpec(shape, lambda bi: (0,) * len(shape))
    return pl.pallas_call(
        _cmp_kernel,
        grid=(b,),
        in_specs=[pl.BlockSpec((None, nb, w), lambda bi: (bi, 0, 0)),
                  pl.BlockSpec((None, nb, w), lambda bi: (bi, 0, 0)),
                  full(1, 2 * w), full(1, 2 * w),
                  full(2 * w, hid), full(hid, HEAD_DIM), full(2 * w, hid), full(hid, HEAD_DIM)],
        out_specs=[pl.BlockSpec((None, nb, HEAD_DIM), lambda bi: (bi, 0, 0)),
                   pl.BlockSpec((None, nb, HEAD_DIM), lambda bi: (bi, 0, 0))],
        out_shape=[jax.ShapeDtypeStruct((b, nb, HEAD_DIM), BF16),
                   jax.ShapeDtypeStruct((b, nb, HEAD_DIM), BF16)],
        compiler_params=_params(1),
        name="nsa_compress",
    )(k2, v2, pek, pev, wk1, wk2, wv1, wv2)


def _nsa_kernel(q_ref, qr_ref, g_ref, kc_ref, vc_ref, ks_ref, kw_ref, vs_ref, vw_ref, o_ref,
                *bufs, tq, tk, n_top):
    assert tk >= tq
    i = pl.program_id(1)
    nb = kc_ref.shape[0]
    n_sel = NSA_SLC_BLOCK
    hq = N_HEADS * tq

    q_raw = jnp.concatenate([_head(q_ref, h) for h in range(N_HEADS)], axis=0)
    q_rot = jnp.concatenate([_head(qr_ref, h) for h in range(N_HEADS)], axis=0)
    row = i * tq + lax.broadcasted_iota(jnp.int32, (tq, 1), 0)
    rep = lambda a: jnp.concatenate([a] * N_HEADS, axis=0)

    cidx = lax.broadcasted_iota(jnp.int32, (1, nb), 1)
    c_valid = rep((cidx * NSA_CMP_STRIDE + (NSA_CMP_BLOCK - 1)) <= row)
    sc = jnp.where(c_valid, _dot_nt(q_raw, kc_ref[...]), NEG_INF)
    e = jnp.exp2(sc - jnp.max(sc, axis=1, keepdims=True))
    p = jnp.where(c_valid, e * (1.0 / jnp.sum(e, axis=1, keepdims=True)), 0.0)
    o_cmp = _dot(p.astype(BF16), vc_ref[...])
    p_sum = p[0:tq]
    for h in range(1, N_HEADS):
        p_sum = p_sum + p[h * tq:(h + 1) * tq]
    oc = lax.broadcasted_iota(jnp.int32, (nb, LANES), 0)
    oj = lax.broadcasted_iota(jnp.int32, (nb, LANES), 1)
    per = NSA_SLC_BLOCK // NSA_CMP_STRIDE
    overlap = (jnp.where(oc // per == oj, 1.0, 0.0)
               + jnp.where((oc + 1) // per == oj, 1.0, 0.0)).astype(BF16)
    imp = _dot3(p_sum, overlap)

    imp_t = jnp.transpose(imp)[:n_sel, :]
    jb = lax.broadcasted_iota(jnp.int32, (n_sel, 1), 0)
    cur = (i * tq + lax.broadcasted_iota(jnp.int32, (1, tq), 1)) // NSA_SLC_BLOCK
    sel_valid = jb <= cur
    forced = sel_valid & ((jb == 0) | (jb >= cur - 1))
    score = jnp.where(forced, FORCE_SCORE, jnp.where(sel_valid, imp_t, NEG_INF))
    cnt = jnp.zeros((n_sel, tq), F32)
    for ii in range(n_sel):
        ri = score[ii:ii + 1, :]
        beats = (ri > score) | ((ri == score) & (jb > ii))
        cnt = cnt + jnp.where(beats, 1.0, 0.0)
    sel_t = jnp.where((cnt < n_top) & sel_valid, 1.0, 0.0)
    sel = jnp.transpose(jnp.concatenate([sel_t, jnp.zeros_like(sel_t)], axis=0)).astype(BF16)

    er = lax.broadcasted_iota(jnp.int32, (LANES, tk), 0)
    ec = lax.broadcasted_iota(jnp.int32, (LANES, tk), 1)

    def causal(j, s, r0):
        rows = i * tq + r0 % tq + lax.broadcasted_iota(jnp.int32, (s.shape[0], 1), 0)
        col = j * tk + lax.broadcasted_iota(jnp.int32, (1, tk), 1)
        return jnp.where(col <= rows, s, NEG_INF)

    def slc_score(_, j):
        expand = jnp.where(er == (j * tk + ec) // NSA_SLC_BLOCK, 1.0, 0.0).astype(BF16)
        ok = _dot(sel, expand) > 0.5
        return jnp.where(rep(ok), _dot_nt(q_rot, ks_ref[_key_rows(j, tk), :]), NEG_INF)

    n_full = (i * tq) // tk
    o_slc, = _flash_sweep_ref(0, n_full, 1, slc_score,
                              lambda _, j: vs_ref[_key_rows(j, tk), :], causal, hq, tk, bufs,
                              rc=16)

    def win_score(_, j):
        col = j * tk + lax.broadcasted_iota(jnp.int32, (1, tk), 1)
        return jnp.where(rep(col > row - NSA_WINDOW),
                         _dot_nt(q_rot, kw_ref[_key_rows(j, tk), :]), NEG_INF)

    j_lo = (jnp.maximum(i * tq - NSA_WINDOW, 0) // (2 * tk)) * 2
    o_win, = _flash_sweep_ref(j_lo, n_full, 1, win_score,
                              lambda _, j: vw_ref[_key_rows(j, tk), :], causal, hq, tk, bufs,
                              rc=16)

    g = jax.nn.sigmoid(g_ref[...])
    for h in range(N_HEADS):
        rows = slice(h * tq, (h + 1) * tq)
        o_ref[:, h * HEAD_DIM:(h + 1) * HEAD_DIM] = (
            g[:, 3 * h:3 * h + 1] * o_cmp[rows] + g[:, 3 * h + 1:3 * h + 2] * o_slc[rows]
            + g[:, 3 * h + 2:3 * h + 3] * o_win[rows]).astype(BF16)


def _nsa_attention(proj, aux, kc, vc, tq=256, tk=512):
    b, s, _ = proj.shape
    nb = kc.shape[1]
    n_top = min(NSA_TOPK, s // NSA_SLC_BLOCK)
    kv = lambda col: pl.BlockSpec((None, s, HEAD_DIM), lambda bi, i: (bi, 0, col))
    return pl.pallas_call(
        functools.partial(_nsa_kernel, tq=tq, tk=tk, n_top=n_top),
        grid=(b, s // tq),
        in_specs=[pl.BlockSpec((None, tq, GROUP), lambda bi, i: (bi, i, COL_AQ // N_HEADS)),
                  pl.BlockSpec((None, tq, GROUP), lambda bi, i: (bi, i, COL_AQR // N_HEADS)),
                  pl.BlockSpec((None, tq, LANES), lambda bi, i: (bi, i, AUX_G)),
                  pl.BlockSpec((None, nb, HEAD_DIM), lambda bi, i: (bi, 0, 0)),
                  pl.BlockSpec((None, nb, HEAD_DIM), lambda bi, i: (bi, 0, 0)),
                  kv(COL_AKS), kv(COL_AKW), kv(COL_AVS), kv(COL_AVW)],
        out_specs=pl.BlockSpec((None, tq, GROUP), lambda bi, i: (bi, i, 0)),
        out_shape=jax.ShapeDtypeStruct((b, s, GROUP), BF16),
        scratch_shapes=_flash_scratch(1, N_HEADS * tq, tk),
        compiler_params=_params(2),
        name="nsa_attention",
    )(proj, proj, aux, kc, vc, proj, proj, proj, proj)


def _out_kernel(oa_ref, ob_ref, oc_ref, od_ref, w_ref, x_ref, g_ref, o_ref):
    acc = _dot(oa_ref[...], w_ref[0:GROUP, :])
    acc = acc + _dot(ob_ref[...], w_ref[GROUP:2 * GROUP, :])
    acc = acc + _dot(oc_ref[...], w_ref[2 * GROUP:3 * GROUP, :])
    acc = acc + _dot(od_ref[...], w_ref[3 * GROUP:4 * GROUP, :])
    o_ref[...] = x_ref[...] + _rms(acc, g_ref[...])


def _out_proj(oa, ob, oc, od, w, layer, x2d, g, tm=512):
    t, d = x2d.shape
    mix = pl.BlockSpec((tm, GROUP), lambda i: (i, 0))
    return pl.pallas_call(
        _out_kernel,
        grid=(t // tm,),
        in_specs=[mix, mix, mix, mix,
                  pl.BlockSpec((None, 4 * GROUP, d), lambda i: (layer, 0, 0)),
                  pl.BlockSpec((tm, d), lambda i: (i, 0)),
                  pl.BlockSpec((1, d), lambda i: (0, 0))],
        out_specs=pl.BlockSpec((tm, d), lambda i: (i, 0)),
        out_shape=jax.ShapeDtypeStruct((t, d), F32),
        compiler_params=_params(1),
        name="out_proj",
    )(oa, ob, oc, od, w, x2d, g)


HALO = 16


def _mlp_kernel(x_ref, xh_ref, gpre_ref, wg_ref, wu_ref, cwg_ref, cwu_ref, cbg_ref, cbu_ref,
                wd_ref, gpost_ref, o_ref, xn_ref, acc_ref, *, tm, tiles_per_seq):
    i = pl.program_id(0)
    f = pl.program_id(1)

    @pl.when(f == 0)
    def _():
        xn_ref[HALO:, :] = _rms(x_ref[...], gpre_ref[...]).astype(BF16)
        halo = _rms(xh_ref[...], gpre_ref[...])
        seq_start = (i % tiles_per_seq) == 0
        xn_ref[:HALO, :] = jnp.where(seq_start, 0.0, halo).astype(BF16)
        acc_ref[...] = jnp.zeros_like(acc_ref)

    xn = xn_ref[...]

    def conv(u, cw_ref, cb_ref):
        cw = cw_ref[...]
        out = cb_ref[...] + cw[0:1] * u[HALO - 2:HALO - 2 + tm]
        out = out + cw[1:2] * u[HALO - 1:HALO - 1 + tm]
        return out + cw[2:3] * u[HALO:HALO + tm]

    gate = conv(_dot(xn, wg_ref[...]), cwg_ref, cbg_ref)
    up = conv(_dot(xn, wu_ref[...]), cwu_ref, cbu_ref)
    hid = (jax.nn.gelu(gate, approximate=True) * up).astype(BF16)
    acc_ref[...] += _dot(hid, wd_ref[...])

    @pl.when(f == pl.num_programs(1) - 1)
    def _():
        o_ref[...] = x_ref[...] + _rms(acc_ref[...], gpost_ref[...])


def _mlp(x2d, gpre, w_up, conv_w, conv_b, w_down, layer, gpost, seq, tm=1024, tf=512):
    t, d = x2d.shape
    ff = w_down.shape[1]
    nf = ff // tf
    hb = tm // HALO
    once = pl.Buffered(1)
    return pl.pallas_call(
        functools.partial(_mlp_kernel, tm=tm, tiles_per_seq=seq // tm),
        grid=(t // tm, nf),
        in_specs=[pl.BlockSpec((tm, d), lambda i, f: (i, 0), pipeline_mode=once),
                  pl.BlockSpec((HALO, d), lambda i, f: (jnp.maximum(i * hb - 1, 0), 0)),
                  pl.BlockSpec((1, d), lambda i, f: (0, 0)),
                  pl.BlockSpec((None, d, tf), lambda i, f: (layer, 0, f)),
                  pl.BlockSpec((None, d, tf), lambda i, f: (layer, 0, nf + f)),
                  pl.BlockSpec((3, tf), lambda i, f: (0, f)),
                  pl.BlockSpec((3, tf), lambda i, f: (0, nf + f)),
                  pl.BlockSpec((1, tf), lambda i, f: (0, f)),
                  pl.BlockSpec((1, tf), lambda i, f: (0, nf + f)),
                  pl.BlockSpec((None, tf, d), lambda i, f: (layer, f, 0)),
                  pl.BlockSpec((1, d), lambda i, f: (0, 0))],
        out_specs=pl.BlockSpec((tm, d), lambda i, f: (i, 0), pipeline_mode=once),
        out_shape=jax.ShapeDtypeStruct((t, d), F32),
        scratch_shapes=[pltpu.VMEM((tm + HALO, d), BF16), pltpu.VMEM((tm, d), F32)],
        compiler_params=_params(2),
        name="conv_geglu_mlp",
    )(x2d, x2d, gpre, w_up, w_up, conv_w, conv_w, conv_b, conv_b, w_down, gpost)


def _split_w_in(w):
    off = np.concatenate([[0], np.cumsum(_IN_SPLITS)]).tolist()
    cols = lambda first, last: w[:, :, off[first]:off[last + 1]].astype(BF16)
    a_q = cols(0, 0)
    heads = jnp.concatenate([a_q, a_q, cols(3, 6), cols(8, 13), cols(15, 17)], axis=-1)
    pad = lambda p: jnp.pad(p, ((0, 0), (0, 0), (0, LANES - p.shape[-1])))
    aux = jnp.concatenate([cols(1, 2), pad(cols(7, 7)), pad(cols(14, 14))], axis=-1)
    return heads, aux


def _rope_lane_tables(seq, rot_dim, period):
    pos = jnp.arange(seq, dtype=F32)
    inv = ROPE_THETA ** (-jnp.arange(0, rot_dim, 2, dtype=F32) / rot_dim)
    ang = pos[:, None] * inv[None, :]
    cos, sin = jnp.cos(ang), jnp.sin(ang)
    rest = period - rot_dim
    c = jnp.concatenate([cos, cos, jnp.ones((seq, rest), F32)], axis=-1)
    s = jnp.concatenate([-sin, sin, jnp.zeros((seq, rest), F32)], axis=-1)
    reps = LANES // period
    return jnp.tile(c, (1, reps)), jnp.tile(s, (1, reps))


@jax.jit
def _forward(x, norm_mix_pre, norm_mix_post, norm_mlp_pre, norm_mlp_post, w_in, w_out,
             nsa_pe_k, nsa_pe_v, nsa_cmp_k1, nsa_cmp_k2, nsa_cmp_v1, nsa_cmp_v2,
             fox_forget_bias, diff_lambda, diff_norm, mlp_w_up, mlp_conv_w, mlp_conv_b,
             mlp_w_down):
    b, s, d = x.shape
    depth = w_in.shape[0]
    cos_h, sin_h = _rope_lane_tables(s, HEAD_DIM // 4, LANES)
    cos_d, sin_d = _rope_lane_tables(s, DIFF_QK // 4, DIFF_QK)
    cos_sets = jnp.stack([cos_h * SCALE_HEAD, cos_h, cos_d * SCALE_DIFF, cos_d])
    sin_sets = jnp.stack([sin_h * SCALE_HEAD, sin_h, sin_d * SCALE_DIFF, sin_d])
    w_heads, w_aux = _split_w_in(w_in)
    w_out_b = w_out.astype(BF16)
    w_up_b = mlp_w_up.astype(BF16)
    w_down_b = mlp_w_down.astype(BF16)
    k1_b, k2_b = nsa_cmp_k1.astype(BF16), nsa_cmp_k2.astype(BF16)
    v1_b, v2_b = nsa_cmp_v1.astype(BF16), nsa_cmp_v2.astype(BF16)
    fox_bias = jnp.pad(fox_forget_bias, ((0, 0), (0, LANES - N_HEADS)))
    row2 = lambda a: a.reshape(1, -1)
    cmp_w = NSA_CMP_STRIDE * HEAD_DIM

    xt = x.reshape(b * s, d)
    for l in range(depth):
        g_pre = row2(norm_mix_pre[l])
        proj = _head_proj(xt, g_pre, w_heads, l, cos_sets, sin_sets, s, tm=min(1024, s))
        proj = proj.reshape(b, s, HEAD_BLOCKS * LANES)
        aux = _aux_proj(xt, g_pre, w_aux, l).reshape(b, s, AUX_BLOCKS * LANES)

        k2 = aux[:, :, AUX_KC * LANES:(AUX_KC + 1) * LANES].reshape(b, s // NSA_CMP_STRIDE, cmp_w)
        v2 = aux[:, :, AUX_VC * LANES:(AUX_VC + 1) * LANES].reshape(b, s // NSA_CMP_STRIDE, cmp_w)
        kc, vc = _nsa_compress(k2, v2, row2(nsa_pe_k[l]), row2(nsa_pe_v[l]),
                               k1_b[l], k2_b[l], v1_b[l], v2_b[l])
        o_a = _nsa_attention(proj, aux, kc, vc)

        o_b = _sb_attention(proj)

        cum_col, cum_row = _fox_cum(aux, fox_bias[l:l + 1])
        o_c = _fox_attention(proj, cum_col, cum_row)

        lam_init = 0.8 - 0.6 * math.exp(-0.3 * l)
        dl = jnp.concatenate([
            jnp.pad(diff_lambda[l], ((0, 0), (0, LANES - DIFF_QK))),
            jnp.full((1, LANES), lam_init, F32),
            jnp.full((1, LANES), 1.0 - lam_init, F32),
            jnp.zeros((2, LANES), F32)], axis=0)
        o_d = _diff_attention(proj, dl, row2(diff_norm[l]))

        flat = lambda o: o.reshape(b * s, GROUP)
        xt = _out_proj(flat(o_a), flat(o_b), flat(o_c), flat(o_d), w_out_b, l, xt,
                       row2(norm_mix_post[l]))
        xt = _mlp(xt, row2(norm_mlp_pre[l]), w_up_b, mlp_conv_w[l], row2(mlp_conv_b[l]),
                  w_down_b, l, row2(norm_mlp_post[l]), s, tm=min(1024, s))
    return xt.reshape(b, s, d)


def kernel(x, norm_mix_pre, norm_mix_post, norm_mlp_pre, norm_mlp_post, w_in, w_out, nsa_pe_k, nsa_pe_v, nsa_cmp_k1, nsa_cmp_k2, nsa_cmp_v1, nsa_cmp_v2, fox_forget_bias, diff_lambda, diff_norm, mlp_w_up, mlp_conv_w, mlp_conv_b, mlp_w_down):
    return _forward(x, norm_mix_pre, norm_mix_post, norm_mlp_pre, norm_mlp_post, w_in, w_out,
                    nsa_pe_k, nsa_pe_v, nsa_cmp_k1, nsa_cmp_k2, nsa_cmp_v1, nsa_cmp_v2,
                    fox_forget_bias, diff_lambda, diff_norm, mlp_w_up, mlp_conv_w, mlp_conv_b,
                    mlp_w_down)
```

```python
import functools
import math

import jax
import jax.numpy as jnp
import numpy as np
from jax import lax
from jax.experimental import pallas as pl
from jax.experimental.pallas import tpu as pltpu

F32 = jnp.float32
BF16 = jnp.bfloat16

LANES = 128
HEAD_DIM = 128
N_HEADS = 4
GROUP = N_HEADS * HEAD_DIM
NSA_CMP_BLOCK = 32
NSA_CMP_STRIDE = 16
NSA_SLC_BLOCK = 64
NSA_TOPK = 16
NSA_WINDOW = 512
DIFF_QK = HEAD_DIM // 2
ROPE_HALF_HEAD = HEAD_DIM // 4 // 2
ROPE_HALF_DIFF = DIFF_QK // 4 // 2
EPS = 1e-6
NEG_INF = -1e30
FORCE_SCORE = 1e9
ROPE_THETA = 500000.0
LOG2E = math.log2(math.e)
SCALE_HEAD = HEAD_DIM ** -0.5 * LOG2E
SCALE_DIFF = DIFF_QK ** -0.5 * LOG2E

_IN_SPLITS = (GROUP, 128, 128, 128, 128, 128, 128, 3 * N_HEADS,
              GROUP, GROUP, GROUP, GROUP, GROUP, GROUP, N_HEADS, GROUP, GROUP, GROUP)
COL_AQ, COL_AQR, COL_AKS, COL_AVS, COL_AKW, COL_AVW = 0, 4, 8, 9, 10, 11
COL_BQ, COL_BK, COL_BV = 12, 16, 20
COL_CQ, COL_CK, COL_CV = 24, 28, 32
COL_DQ, COL_DK, COL_DV = 36, 40, 44
HEAD_BLOCKS = 48
AUX_KC, AUX_VC, AUX_G, AUX_F = 0, 1, 2, 3
AUX_BLOCKS = 4

PROJ_TN = GROUP
_TILES_SCALE = (0, 3, 6)
_TILE_ROPE_Q = 1
_TILE_ROPE_KV = 2
_TILES_ROPE_DIFF = (9, 10)
_ROPE_SET_STARTS = (2, 9, 10)

VMEM_LIMIT = 56 * 1024 * 1024


def _params(n_axes, vmem=VMEM_LIMIT):
    return pltpu.CompilerParams(dimension_semantics=("arbitrary",) * n_axes,
                                vmem_limit_bytes=vmem)


def _rms(x, g):
    ms = jnp.mean(x * x, axis=-1, keepdims=True)
    return x * lax.rsqrt(ms + EPS) * g


def _dot(a, b):
    return jnp.dot(a, b, preferred_element_type=F32)


def _dot_nt(a, b):
    return lax.dot_general(a, b, (((1,), (1,)), ((), ())), preferred_element_type=F32)


def _split3(x):
    hi = x.astype(BF16)
    r = x - hi.astype(F32)
    mid = r.astype(BF16)
    lo = (r - mid.astype(F32)).astype(BF16)
    return hi, mid, lo


def _dot3(x, m):
    hi, mid, lo = _split3(x)
    return _dot(hi, m) + _dot(mid, m) + _dot(lo, m)


def _dot2(x, m):
    hi = x.astype(BF16)
    lo = (x - hi.astype(F32)).astype(BF16)
    return _dot(hi, m) + _dot(lo, m)


def _dot3_left(m, x):
    hi, mid, lo = _split3(x)
    return _dot(m, hi) + _dot(m, mid) + _dot(m, lo)


def _rope(x, c, s, half):
    width = x.shape[1]
    lane = lax.broadcasted_iota(jnp.int32, (1, width), 1)
    first = (lane % (2 * half)) < half
    xs = jnp.where(first, pltpu.roll(x, width - half, 1), pltpu.roll(x, half, 1))
    return x * c + xs * s


def _lane_fold(x, op):
    out = x[:, :LANES]
    for b in range(1, x.shape[1] // LANES):
        out = op(out, x[:, b * LANES:(b + 1) * LANES])
    return out


def _row_max(s):
    return jnp.max(_lane_fold(s, jnp.maximum), axis=1, keepdims=True)


def _flash_scratch(n_chain, rows, tk):
    stat = pltpu.VMEM((n_chain, rows, LANES), F32)
    return [pltpu.VMEM((n_chain, rows, tk), F32), pltpu.VMEM((n_chain, rows, tk), F32),
            pltpu.VMEM((n_chain, rows, tk), BF16), stat, stat,
            pltpu.VMEM((n_chain, rows, 2 * LANES), F32)]


def _flash_sweep_ref(lo, n_full, n_chain, score_fn, value_fn, mask_fn, rows, tk, bufs, rc=16):
    s_even, s_odd, p_buf, m_ref, a_ref, acc_ref = bufs
    chains = range(n_chain)
    reps = tk // LANES
    ones = jnp.ones((tk, LANES), BF16)
    m_ref[...] = jnp.full(m_ref.shape, NEG_INF, F32)
    acc_ref[...] = jnp.zeros(acc_ref.shape, F32)

    def scores(s_buf, j):
        for c in chains:
            s_buf[c] = score_fn(c, j)

    def consume(s_buf, j, masked):
        for c in chains:
            for r0 in range(0, rows, rc):
                rr = slice(r0, r0 + rc)
                s = s_buf[c, rr, :]
                if masked:
                    s = mask_fn(j, s, r0)
                m_old = m_ref[c, rr, :]
                m_new = jnp.maximum(m_old, _row_max(s))
                m_ref[c, rr, :] = m_new
                a_ref[c, rr, :] = jnp.exp2(m_old - m_new)
                shifted = s - jnp.concatenate([m_new] * reps, axis=1)
                p_buf[c, rr, :] = jnp.exp2(shifted.astype(BF16))
            v_ones = jnp.concatenate([value_fn(c, j), ones], axis=1)
            alpha = a_ref[c]
            acc_ref[c] = (jnp.concatenate([alpha, alpha], axis=1) * acc_ref[c]
                          + _dot(p_buf[c], v_ones))

    scores(s_even, lo)

    def pair(t, carry):
        j = lo + 2 * t
        scores(s_odd, j + 1)
        consume(s_even, j, False)
        scores(s_even, j + 2)
        consume(s_odd, j + 1, False)
        return carry

    n_pair = (n_full - lo) // 2
    lax.fori_loop(0, n_pair, pair, 0)
    j = lo + 2 * n_pair
    last_full = j < n_full

    @pl.when(last_full)
    def _():
        scores(s_odd, j + 1)

    consume(s_even, j, True)

    @pl.when(last_full)
    def _():
        consume(s_odd, j + 1, True)

    return [acc_ref[c, :, :LANES] * (1.0 / acc_ref[c, :, LANES:]) for c in chains]


def _head(ref, h, rows=None):
    cols = slice(h * HEAD_DIM, (h + 1) * HEAD_DIM)
    return ref[:, cols] if rows is None else ref[rows, cols]


def _key_rows(j, tk):
    return pl.ds(pl.multiple_of(j * tk, tk), tk)


def _tile_in(j, tiles):
    hit = j == tiles[0]
    for t in tiles[1:]:
        hit = hit | (j == t)
    return hit


def _head_proj_kernel(x_ref, g_ref, w_ref, c_ref, s_ref, o_ref, xn_ref, *, slabs):
    j = pl.program_id(1)

    @pl.when(j == 0)
    def _():
        xn_ref[...] = _rms(x_ref[...], g_ref[...]).astype(BF16)

    is_scale = _tile_in(j, _TILES_SCALE)
    is_rope_d = _tile_in(j, _TILES_ROPE_DIFF)
    is_rope = (j == _TILE_ROPE_Q) | (j == _TILE_ROPE_KV) | is_rope_d
    slab = o_ref.shape[0] // slabs

    def tile(epilogue):
        for r in range(slabs):
            rows = slice(r * slab, (r + 1) * slab)
            o_ref[rows, :] = epilogue(_dot(xn_ref[rows, :], w_ref[...]), rows).astype(BF16)

    def rope_tile(half, rotated):
        def epilogue(acc, rows):
            ones = jnp.ones((slab, LANES), F32)
            c = jnp.concatenate([c_ref[rows, :] if r else ones for r in rotated], axis=1)
            s = jnp.concatenate([s_ref[rows, :] if r else 0.0 * ones for r in rotated], axis=1)
            return _rope(acc, c, s, half)
        tile(epilogue)

    every = (True,) * N_HEADS
    pl.when(is_scale)(lambda: tile(lambda acc, rows: acc * SCALE_HEAD))
    pl.when(j == _TILE_ROPE_Q)(lambda: rope_tile(ROPE_HALF_HEAD, every))
    pl.when(j == _TILE_ROPE_KV)(lambda: rope_tile(ROPE_HALF_HEAD, (True, False, True, False)))
    pl.when(is_rope_d)(lambda: rope_tile(ROPE_HALF_DIFF, every))
    pl.when(jnp.logical_not(is_scale | is_rope))(lambda: tile(lambda acc, rows: acc))


def _rope_set(j):
    idx = 0
    for start in _ROPE_SET_STARTS:
        idx = idx + (j >= start).astype(jnp.int32)
    return idx


def _head_proj(x2d, g, w, layer, cos_sets, sin_sets, seq, tm=1024, slabs=4):
    t, d = x2d.shape
    n = w.shape[2]
    per_seq = seq // tm
    table = pl.BlockSpec((None, tm, LANES), lambda i, j: (_rope_set(j), i % per_seq, 0))
    return pl.pallas_call(
        functools.partial(_head_proj_kernel, slabs=slabs),
        grid=(t // tm, n // PROJ_TN),
        in_specs=[pl.BlockSpec((tm, d), lambda i, j: (i, 0)),
                  pl.BlockSpec((1, d), lambda i, j: (0, 0)),
                  pl.BlockSpec((None, d, PROJ_TN), lambda i, j: (layer, 0, j)),
                  table, table],
        out_specs=pl.BlockSpec((tm, PROJ_TN), lambda i, j: (i, j)),
        out_shape=jax.ShapeDtypeStruct((t, n), BF16),
        scratch_shapes=[pltpu.VMEM((tm, d), BF16)],
        compiler_params=_params(2),
        name="head_proj",
    )(x2d, g, w, cos_sets, sin_sets)


def _norm_matmul_kernel(x_ref, g_ref, w_ref, o_ref):
    o_ref[...] = _dot(_rms(x_ref[...], g_ref[...]).astype(BF16), w_ref[...])


def _aux_proj(x2d, g, w, layer, tm=512):
    t, d = x2d.shape
    n = w.shape[2]
    return pl.pallas_call(
        _norm_matmul_kernel,
        grid=(t // tm,),
        in_specs=[pl.BlockSpec((tm, d), lambda i: (i, 0)),
                  pl.BlockSpec((1, d), lambda i: (0, 0)),
                  pl.BlockSpec((None, d, n), lambda i: (layer, 0, 0))],
        out_specs=pl.BlockSpec((tm, n), lambda i: (i, 0)),
        out_shape=jax.ShapeDtypeStruct((t, n), F32),
        compiler_params=_params(1),
        name="aux_proj",
    )(x2d, g, w)


def _sb_kernel(q_ref, k_ref, v_ref, o_ref,
               ls_e, ls_o, hl_e, hl_o, tot_e, tot_o, suf_ref, e_ref, c_ref, acc_ref,
               *, tq, tk, hp, rc):
    assert tq == tk
    i = pl.program_id(2)
    reps = tk // LANES
    heads = range(hp)
    jr = lax.broadcasted_iota(jnp.int32, (tk, tk), 0)
    jc = lax.broadcasted_iota(jnp.int32, (tk, tk), 1)
    u = jnp.where(jr > jc, 1.0, 0.0).astype(BF16)
    qs = [_head(q_ref, h) for h in heads]
    c_ref[...] = jnp.zeros(c_ref.shape, F32)
    acc_ref[...] = jnp.zeros(acc_ref.shape, F32)

    lo_base = hp * tq

    def prepare(bufs, j, diagonal):
        ls_ref, hl_ref, tot_ref = bufs
        for h in heads:
            z = _dot_nt(qs[h], _head(k_ref, h, _key_rows(j, tk)))
            for r0 in range(0, tq, rc):
                rr = slice(r0, r0 + rc)
                zc = z[rr]
                ls = jnp.minimum(zc, 0.0) - jnp.log2(1.0 + jnp.exp2(-jnp.abs(zc)))
                lk = ls - zc
                if diagonal:
                    row = r0 + lax.broadcasted_iota(jnp.int32, (rc, 1), 0)
                    col = lax.broadcasted_iota(jnp.int32, (1, tk), 1)
                    past = col < row
                    lk = jnp.where(past, lk, 0.0)
                    ls = jnp.where(past, ls, NEG_INF)
                hi = lk.astype(BF16)
                hl_ref[h * tq + r0:h * tq + r0 + rc, :] = hi
                hl_ref[lo_base + h * tq + r0:lo_base + h * tq + r0 + rc, :] = (
                    lk - hi.astype(F32)).astype(BF16)
                ls_ref[h, rr, :] = ls
                tot = jnp.sum(_lane_fold(lk, jnp.add), axis=1, keepdims=True)
                tot_ref[h, rr, :] = jnp.broadcast_to(tot, (rc, LANES))

    def weigh(bufs, j):
        ls_ref, hl_ref, tot_ref = bufs
        suf_ref[...] = _dot(hl_ref[...], u)
        for h in heads:
            for r0 in range(0, tq, rc):
                rr = slice(r0, r0 + rc)
                c = c_ref[h, rr, :]
                suf = (suf_ref[h * tq + r0:h * tq + r0 + rc, :]
                       + suf_ref[lo_base + h * tq + r0:lo_base + h * tq + r0 + rc, :])
                pre = ls_ref[h, rr, :] + suf + jnp.concatenate([c] * reps, axis=1)
                e_ref[h, rr, :] = jnp.exp2(pre).astype(BF16)
                c_ref[h, rr, :] = c + tot_ref[h, rr, :]
            acc_ref[h] += _dot(e_ref[h], _head(v_ref, h, _key_rows(j, tk)))

    even = (ls_e, hl_e, tot_e)
    odd = (ls_o, hl_o, tot_o)
    prepare(even, i, True)

    def pair(p, carry):
        j = i - 2 * p
        prepare(odd, j - 1, False)
        weigh(even, j)
        prepare(even, j - 2, False)
        weigh(odd, j - 1)
        return carry

    n_pair = i // 2
    lax.fori_loop(0, n_pair, pair, 0)
    j = i - 2 * n_pair
    one_more = j > 0

    @pl.when(one_more)
    def _():
        prepare(odd, j - 1, False)

    weigh(even, j)

    @pl.when(one_more)
    def _():
        weigh(odd, j - 1)

    for h in heads:
        o_ref[:, h * HEAD_DIM:(h + 1) * HEAD_DIM] = acc_ref[h].astype(BF16)


def _sb_attention(proj, tq=256, tk=256, hp=4, rc=16):
    b, s, _ = proj.shape
    w = hp * HEAD_DIM
    wide_f32 = pltpu.VMEM((hp, tq, tk), F32)
    wide_bf16 = pltpu.VMEM((hp, tq, tk), BF16)
    stat = pltpu.VMEM((hp, tq, LANES), F32)
    return pl.pallas_call(
        functools.partial(_sb_kernel, tq=tq, tk=tk, hp=hp, rc=rc),
        grid=(b, N_HEADS // hp, s // tq),
        in_specs=[pl.BlockSpec((None, tq, w), lambda bi, hg, i: (bi, i, COL_BQ // hp + hg)),
                  pl.BlockSpec((None, s, w), lambda bi, hg, i: (bi, 0, COL_BK // hp + hg)),
                  pl.BlockSpec((None, s, w), lambda bi, hg, i: (bi, 0, COL_BV // hp + hg))],
        out_specs=pl.BlockSpec((None, tq, w), lambda bi, hg, i: (bi, i, hg)),
        out_shape=jax.ShapeDtypeStruct((b, s, GROUP), BF16),
        scratch_shapes=[wide_f32, wide_f32,
                        pltpu.VMEM((2 * hp * tq, tk), BF16), pltpu.VMEM((2 * hp * tq, tk), BF16),
                        stat, stat, pltpu.VMEM((2 * hp * tq, tk), F32), wide_bf16, stat, stat],
        compiler_params=_params(3),
        name="sb_attention",
    )(proj, proj, proj)


def _fox_cum_kernel(cf_ref, bias_ref, col_ref, row_ref, *, chunk):
    s = cf_ref.shape[0]
    lf = jax.nn.log_sigmoid(cf_ref[...] + bias_ref[...]) * LOG2E
    rr = lax.broadcasted_iota(jnp.int32, (chunk, chunk), 0)
    cc = lax.broadcasted_iota(jnp.int32, (chunk, chunk), 1)
    tri = jnp.where(rr >= cc, 1.0, 0.0).astype(BF16)
    carry = jnp.zeros((1, LANES), F32)
    for c in range(s // chunk):
        pre = _dot3_left(tri, lf[c * chunk:(c + 1) * chunk]) + carry
        col_ref[c * chunk:(c + 1) * chunk, :] = pre
        carry = pre[chunk - 1:chunk, :]
    row_ref[...] = jnp.transpose(col_ref[...])[:8, :]


def _fox_cum(aux, bias_row, chunk=256):
    b, s, _ = aux.shape
    return pl.pallas_call(
        functools.partial(_fox_cum_kernel, chunk=chunk),
        grid=(b,),
        in_specs=[pl.BlockSpec((None, s, LANES), lambda bi: (bi, 0, AUX_F)),
                  pl.BlockSpec((1, LANES), lambda bi: (0, 0))],
        out_specs=[pl.BlockSpec((None, s, LANES), lambda bi: (bi, 0, 0)),
                   pl.BlockSpec((None, 8, s), lambda bi: (bi, 0, 0))],
        out_shape=[jax.ShapeDtypeStruct((b, s, LANES), F32),
                   jax.ShapeDtypeStruct((b, 8, s), F32)],
        compiler_params=_params(1),
        name="fox_cum",
    )(aux, bias_row)


def _fox_kernel(q_ref, k_ref, v_ref, cq_ref, ck_ref, o_ref, *bufs, tq, tk, hp):
    assert tk >= tq
    hg = pl.program_id(1)
    i = pl.program_id(2)
    lane = lax.broadcasted_iota(jnp.int32, (1, LANES), 1)
    cum_q = cq_ref[...]
    qs = [_head(q_ref, h) for h in range(hp)]
    cqs = [jnp.sum(jnp.where(lane == hg * hp + h, cum_q, 0.0), axis=1, keepdims=True)
           for h in range(hp)]

    def score(h, j):
        return (_dot_nt(qs[h], _head(k_ref, h, _key_rows(j, tk))) + cqs[h]
                - ck_ref[h, pl.ds(j, 1), :])

    def value(h, j):
        return _head(v_ref, h, _key_rows(j, tk))

    def causal(j, s, r0):
        row = i * tq + r0 + lax.broadcasted_iota(jnp.int32, (s.shape[0], 1), 0)
        col = j * tk + lax.broadcasted_iota(jnp.int32, (1, tk), 1)
        return jnp.where(col <= row, s, NEG_INF)

    outs = _flash_sweep_ref(0, (i * tq) // tk, hp, score, value, causal, tq, tk, bufs)
    for h in range(hp):
        o_ref[:, h * HEAD_DIM:(h + 1) * HEAD_DIM] = outs[h].astype(BF16)


def _fox_attention(proj, cum_col, cum_row, tq=256, tk=512, hp=4):
    b, s, _ = proj.shape
    w = hp * HEAD_DIM
    ck = cum_row.reshape(b, 8, s // tk, tk)
    return pl.pallas_call(
        functools.partial(_fox_kernel, tq=tq, tk=tk, hp=hp),
        grid=(b, N_HEADS // hp, s // tq),
        in_specs=[pl.BlockSpec((None, tq, w), lambda bi, hg, i: (bi, i, COL_CQ // hp + hg)),
                  pl.BlockSpec((None, s, w), lambda bi, hg, i: (bi, 0, COL_CK // hp + hg)),
                  pl.BlockSpec((None, s, w), lambda bi, hg, i: (bi, 0, COL_CV // hp + hg)),
                  pl.BlockSpec((None, tq, LANES), lambda bi, hg, i: (bi, i, 0)),
                  pl.BlockSpec((None, hp, s // tk, tk), lambda bi, hg, i: (bi, hg, 0, 0))],
        out_specs=pl.BlockSpec((None, tq, w), lambda bi, hg, i: (bi, i, hg)),
        out_shape=jax.ShapeDtypeStruct((b, s, GROUP), BF16),
        scratch_shapes=_flash_scratch(hp, tq, tk),
        compiler_params=_params(3),
        name="fox_attention",
    )(proj, proj, proj, cum_col, ck)


def _diff_kernel(q_ref, k_ref, v_ref, dl_ref, gn_ref, o_ref, *bufs, tq, tk, hp):
    assert tk >= tq
    i = pl.program_id(2)
    lane = lax.broadcasted_iota(jnp.int32, (1, LANES), 1)
    zero = jnp.zeros((), BF16)
    qs = []
    for h in range(hp):
        qh = _head(q_ref, h)
        qs.append(jnp.concatenate([jnp.where(lane < DIFF_QK, qh, zero),
                                   jnp.where(lane >= DIFF_QK, qh, zero)], axis=0))

    def score(h, j):
        return _dot_nt(qs[h], _head(k_ref, h, _key_rows(j, tk)))

    def value(h, j):
        return _head(v_ref, h, _key_rows(j, tk))

    def causal(j, s, r0):
        row = i * tq + r0 % tq + lax.broadcasted_iota(jnp.int32, (s.shape[0], 1), 0)
        col = j * tk + lax.broadcasted_iota(jnp.int32, (1, tk), 1)
        return jnp.where(col <= row, s, NEG_INF)

    outs = _flash_sweep_ref(0, (i * tq) // tk, hp, score, value, causal, 2 * tq, tk, bufs,
                            rc=16)
    dl = dl_ref[...]
    lam = (jnp.exp(jnp.sum(dl[0:1] * dl[1:2], axis=1, keepdims=True))
           - jnp.exp(jnp.sum(dl[2:3] * dl[3:4], axis=1, keepdims=True)) + dl[4:5, 0:1])
    for h in range(hp):
        od = outs[h][:tq] - lam * outs[h][tq:]
        o_ref[:, h * HEAD_DIM:(h + 1) * HEAD_DIM] = (
            _rms(od, gn_ref[...]) * dl[5:6, 0:1]).astype(BF16)


def _diff_attention(proj, dl, gn, tq=256, tk=512, hp=4):
    b, s, _ = proj.shape
    w = hp * HEAD_DIM
    return pl.pallas_call(
        functools.partial(_diff_kernel, tq=tq, tk=tk, hp=hp),
        grid=(b, N_HEADS // hp, s // tq),
        in_specs=[pl.BlockSpec((None, tq, w), lambda bi, hg, i: (bi, i, COL_DQ // hp + hg)),
                  pl.BlockSpec((None, s, w), lambda bi, hg, i: (bi, 0, COL_DK // hp + hg)),
                  pl.BlockSpec((None, s, w), lambda bi, hg, i: (bi, 0, COL_DV // hp + hg)),
                  pl.BlockSpec((8, LANES), lambda bi, hg, i: (0, 0)),
                  pl.BlockSpec((1, LANES), lambda bi, hg, i: (0, 0))],
        out_specs=pl.BlockSpec((None, tq, w), lambda bi, hg, i: (bi, i, hg)),
        out_shape=jax.ShapeDtypeStruct((b, s, GROUP), BF16),
        scratch_shapes=_flash_scratch(hp, 2 * tq, tk),
        compiler_params=_params(3),
        name="diff_attention",
    )(proj, proj, proj, dl, gn)


def _cmp_kernel(k2_ref, v2_ref, pek_ref, pev_ref, wk1_ref, wk2_ref, wv1_ref, wv2_ref,
                kc_ref, vc_ref):
    nb = k2_ref.shape[0]
    w = NSA_CMP_STRIDE * HEAD_DIM

    def compress(x2, pe, w1_ref, w2_ref):
        a = _dot((x2 + pe[:, :w]).astype(BF16), w1_ref[:w, :])
        bb = _dot((x2 + pe[:, w:]).astype(BF16), w1_ref[w:, :])
        hid = jax.nn.gelu(a + pltpu.roll(bb, nb - 1, 0), approximate=True)
        return _dot(hid.astype(BF16), w2_ref[...])

    kc_ref[...] = compress(k2_ref[...], pek_ref[...], wk1_ref, wk2_ref).astype(BF16)
    vc_ref[...] = compress(v2_ref[...], pev_ref[...], wv1_ref, wv2_ref).astype(BF16)


def _nsa_compress(k2, v2, pek, pev, wk1, wk2, wv1, wv2):
    b, nb, w = k2.shape
    hid = wk1.shape[1]
    full = lambda *shape: pl.BlockSpec(shape, lambda bi: (0,) * len(shape))
    return pl.pallas_call(
        _cmp_kernel,
        grid=(b,),
        in_specs=[pl.BlockSpec((None, nb, w), lambda bi: (bi, 0, 0)),
                  pl.BlockSpec((None, nb, w), lambda bi: (bi, 0, 0)),
                  full(1, 2 * w), full(1, 2 * w),
                  full(2 * w, hid), full(hid, HEAD_DIM), full(2 * w, hid), full(hid, HEAD_DIM)],
        out_specs=[pl.BlockSpec((None, nb, HEAD_DIM), lambda bi: (bi, 0, 0)),
                   pl.BlockSpec((None, nb, HEAD_DIM), lambda bi: (bi, 0, 0))],
        out_shape=[jax.ShapeDtypeStruct((b, nb, HEAD_DIM), BF16),
                   jax.ShapeDtypeStruct((b, nb, HEAD_DIM), BF16)],
        compiler_params=_params(1),
        name="nsa_compress",
    )(k2, v2, pek, pev, wk1, wk2, wv1, wv2)


def _nsa_kernel(q_ref, qr_ref, g_ref, kc_ref, vc_ref, ks_ref, kw_ref, vs_ref, vw_ref, o_ref,
                *bufs, tq, tk, n_top):
    assert tk >= tq
    i = pl.program_id(1)
    nb = kc_ref.shape[0]
    n_sel = NSA_SLC_BLOCK
    hq = N_HEADS * tq

    q_raw = jnp.concatenate([_head(q_ref, h) for h in range(N_HEADS)], axis=0)
    q_rot = jnp.concatenate([_head(qr_ref, h) for h in range(N_HEADS)], axis=0)
    row = i * tq + lax.broadcasted_iota(jnp.int32, (tq, 1), 0)
    rep = lambda a: jnp.concatenate([a] * N_HEADS, axis=0)

    cidx = lax.broadcasted_iota(jnp.int32, (1, nb), 1)
    c_valid = rep((cidx * NSA_CMP_STRIDE + (NSA_CMP_BLOCK - 1)) <= row)
    sc = jnp.where(c_valid, _dot_nt(q_raw, kc_ref[...]), NEG_INF)
    e = jnp.exp2(sc - jnp.max(sc, axis=1, keepdims=True))
    p = jnp.where(c_valid, e * (1.0 / jnp.sum(e, axis=1, keepdims=True)), 0.0)
    o_cmp = _dot(p.astype(BF16), vc_ref[...])
    p_sum = p[0:tq]
    for h in range(1, N_HEADS):
        p_sum = p_sum + p[h * tq:(h + 1) * tq]
    oc = lax.broadcasted_iota(jnp.int32, (nb, LANES), 0)
    oj = lax.broadcasted_iota(jnp.int32, (nb, LANES), 1)
    per = NSA_SLC_BLOCK // NSA_CMP_STRIDE
    overlap = (jnp.where(oc // per == oj, 1.0, 0.0)
               + jnp.where((oc + 1) // per == oj, 1.0, 0.0)).astype(BF16)
    imp = _dot3(p_sum, overlap)

    imp_t = jnp.transpose(imp)[:n_sel, :]
    jb = lax.broadcasted_iota(jnp.int32, (n_sel, 1), 0)
    cur = (i * tq + lax.broadcasted_iota(jnp.int32, (1, tq), 1)) // NSA_SLC_BLOCK
    sel_valid = jb <= cur
    forced = sel_valid & ((jb == 0) | (jb >= cur - 1))
    score = jnp.where(forced, FORCE_SCORE, jnp.where(sel_valid, imp_t, NEG_INF))
    cnt = jnp.zeros((n_sel, tq), F32)
    for ii in range(n_sel):
        ri = score[ii:ii + 1, :]
        beats = (ri > score) | ((ri == score) & (jb > ii))
        cnt = cnt + jnp.where(beats, 1.0, 0.0)
    sel_t = jnp.where((cnt < n_top) & sel_valid, 1.0, 0.0)
    sel = jnp.transpose(jnp.concatenate([sel_t, jnp.zeros_like(sel_t)], axis=0)).astype(BF16)

    er = lax.broadcasted_iota(jnp.int32, (LANES, tk), 0)
    ec = lax.broadcasted_iota(jnp.int32, (LANES, tk), 1)

    def causal(j, s, r0):
        rows = i * tq + r0 % tq + lax.broadcasted_iota(jnp.int32, (s.shape[0], 1), 0)
        col = j * tk + lax.broadcasted_iota(jnp.int32, (1, tk), 1)
        return jnp.where(col <= rows, s, NEG_INF)

    def slc_score(_, j):
        expand = jnp.where(er == (j * tk + ec) // NSA_SLC_BLOCK, 1.0, 0.0).astype(BF16)
        ok = _dot(sel, expand) > 0.5
        return jnp.where(rep(ok), _dot_nt(q_rot, ks_ref[_key_rows(j, tk), :]), NEG_INF)

    n_full = (i * tq) // tk
    o_slc, = _flash_sweep_ref(0, n_full, 1, slc_score,
                              lambda _, j: vs_ref[_key_rows(j, tk), :], causal, hq, tk, bufs,
                              rc=16)

    def win_score(_, j):
        col = j * tk + lax.broadcasted_iota(jnp.int32, (1, tk), 1)
        return jnp.where(rep(col > row - NSA_WINDOW),
                         _dot_nt(q_rot, kw_ref[_key_rows(j, tk), :]), NEG_INF)

    j_lo = (jnp.maximum(i * tq - NSA_WINDOW, 0) // (2 * tk)) * 2
    o_win, = _flash_sweep_ref(j_lo, n_full, 1, win_score,
                              lambda _, j: vw_ref[_key_rows(j, tk), :], causal, hq, tk, bufs,
                              rc=16)

    g = jax.nn.sigmoid(g_ref[...])
    for h in range(N_HEADS):
        rows = slice(h * tq, (h + 1) * tq)
        o_ref[:, h * HEAD_DIM:(h + 1) * HEAD_DIM] = (
            g[:, 3 * h:3 * h + 1] * o_cmp[rows] + g[:, 3 * h + 1:3 * h + 2] * o_slc[rows]
            + g[:, 3 * h + 2:3 * h + 3] * o_win[rows]).astype(BF16)


def _nsa_attention(proj, aux, kc, vc, tq=256, tk=512):
    b, s, _ = proj.shape
    nb = kc.shape[1]
    n_top = min(NSA_TOPK, s // NSA_SLC_BLOCK)
    kv = lambda col: pl.BlockSpec((None, s, HEAD_DIM), lambda bi, i: (bi, 0, col))
    return pl.pallas_call(
        functools.partial(_nsa_kernel, tq=tq, tk=tk, n_top=n_top),
        grid=(b, s // tq),
        in_specs=[pl.BlockSpec((None, tq, GROUP), lambda bi, i: (bi, i, COL_AQ // N_HEADS)),
                  pl.BlockSpec((None, tq, GROUP), lambda bi, i: (bi, i, COL_AQR // N_HEADS)),
                  pl.BlockSpec((None, tq, LANES), lambda bi, i: (bi, i, AUX_G)),
                  pl.BlockSpec((None, nb, HEAD_DIM), lambda bi, i: (bi, 0, 0)),
                  pl.BlockSpec((None, nb, HEAD_DIM), lambda bi, i: (bi, 0, 0)),
                  kv(COL_AKS), kv(COL_AKW), kv(COL_AVS), kv(COL_AVW)],
        out_specs=pl.BlockSpec((None, tq, GROUP), lambda bi, i: (bi, i, 0)),
        out_shape=jax.ShapeDtypeStruct((b, s, GROUP), BF16),
        scratch_shapes=_flash_scratch(1, N_HEADS * tq, tk),
        compiler_params=_params(2),
        name="nsa_attention",
    )(proj, proj, aux, kc, vc, proj, proj, proj, proj)


def _out_kernel(oa_ref, ob_ref, oc_ref, od_ref, w_ref, x_ref, g_ref, o_ref):
    acc = _dot(oa_ref[...], w_ref[0:GROUP, :])
    acc = acc + _dot(ob_ref[...], w_ref[GROUP:2 * GROUP, :])
    acc = acc + _dot(oc_ref[...], w_ref[2 * GROUP:3 * GROUP, :])
    acc = acc + _dot(od_ref[...], w_ref[3 * GROUP:4 * GROUP, :])
    o_ref[...] = x_ref[...] + _rms(acc, g_ref[...])


def _out_proj(oa, ob, oc, od, w, layer, x2d, g, tm=512):
    t, d = x2d.shape
    mix = pl.BlockSpec((tm, GROUP), lambda i: (i, 0))
    return pl.pallas_call(
        _out_kernel,
        grid=(t // tm,),
        in_specs=[mix, mix, mix, mix,
                  pl.BlockSpec((None, 4 * GROUP, d), lambda i: (layer, 0, 0)),
                  pl.BlockSpec((tm, d), lambda i: (i, 0)),
                  pl.BlockSpec((1, d), lambda i: (0, 0))],
        out_specs=pl.BlockSpec((tm, d), lambda i: (i, 0)),
        out_shape=jax.ShapeDtypeStruct((t, d), F32),
        compiler_params=_params(1),
        name="out_proj",
    )(oa, ob, oc, od, w, x2d, g)


HALO = 16


def _mlp_kernel(x_ref, xh_ref, gpre_ref, wg_ref, wu_ref, cwg_ref, cwu_ref, cbg_ref, cbu_ref,
                wd_ref, gpost_ref, o_ref, xn_ref, acc_ref, *, tm, tiles_per_seq):
    i = pl.program_id(0)
    f = pl.program_id(1)

    @pl.when(f == 0)
    def _():
        xn_ref[HALO:, :] = _rms(x_ref[...], gpre_ref[...]).astype(BF16)
        halo = _rms(xh_ref[...], gpre_ref[...])
        seq_start = (i % tiles_per_seq) == 0
        xn_ref[:HALO, :] = jnp.where(seq_start, 0.0, halo).astype(BF16)
        acc_ref[...] = jnp.zeros_like(acc_ref)

    xn = xn_ref[...]

    def conv(u, cw_ref, cb_ref):
        cw = cw_ref[...]
        out = cb_ref[...] + cw[0:1] * u[HALO - 2:HALO - 2 + tm]
        out = out + cw[1:2] * u[HALO - 1:HALO - 1 + tm]
        return out + cw[2:3] * u[HALO:HALO + tm]

    gate = conv(_dot(xn, wg_ref[...]), cwg_ref, cbg_ref)
    up = conv(_dot(xn, wu_ref[...]), cwu_ref, cbu_ref)
    hid = (jax.nn.gelu(gate, approximate=True) * up).astype(BF16)
    acc_ref[...] += _dot(hid, wd_ref[...])

    @pl.when(f == pl.num_programs(1) - 1)
    def _():
        o_ref[...] = x_ref[...] + _rms(acc_ref[...], gpost_ref[...])


def _mlp(x2d, gpre, w_up, conv_w, conv_b, w_down, layer, gpost, seq, tm=1024, tf=512):
    t, d = x2d.shape
    ff = w_down.shape[1]
    nf = ff // tf
    hb = tm // HALO
    once = pl.Buffered(1)
    return pl.pallas_call(
        functools.partial(_mlp_kernel, tm=tm, tiles_per_seq=seq // tm),
        grid=(t // tm, nf),
        in_specs=[pl.BlockSpec((tm, d), lambda i, f: (i, 0), pipeline_mode=once),
                  pl.BlockSpec((HALO, d), lambda i, f: (jnp.maximum(i * hb - 1, 0), 0)),
                  pl.BlockSpec((1, d), lambda i, f: (0, 0)),
                  pl.BlockSpec((None, d, tf), lambda i, f: (layer, 0, f)),
                  pl.BlockSpec((None, d, tf), lambda i, f: (layer, 0, nf + f)),
                  pl.BlockSpec((3, tf), lambda i, f: (0, f)),
                  pl.BlockSpec((3, tf), lambda i, f: (0, nf + f)),
                  pl.BlockSpec((1, tf), lambda i, f: (0, f)),
                  pl.BlockSpec((1, tf), lambda i, f: (0, nf + f)),
                  pl.BlockSpec((None, tf, d), lambda i, f: (layer, f, 0)),
                  pl.BlockSpec((1, d), lambda i, f: (0, 0))],
        out_specs=pl.BlockSpec((tm, d), lambda i, f: (i, 0), pipeline_mode=once),
        out_shape=jax.ShapeDtypeStruct((t, d), F32),
        scratch_shapes=[pltpu.VMEM((tm + HALO, d), BF16), pltpu.VMEM((tm, d), F32)],
        compiler_params=_params(2),
        name="conv_geglu_mlp",
    )(x2d, x2d, gpre, w_up, w_up, conv_w, conv_w, conv_b, conv_b, w_down, gpost)


_IN_OFF = np.concatenate([[0], np.cumsum(_IN_SPLITS)]).tolist()
_HEAD_RUNS = ((0, 0), (0, 0), (3, 6), (8, 13), (15, 17))


def _repack_kernel(w_ref, heads_ref, aux_ref):
    def run(first, last):
        return w_ref[:, _IN_OFF[first]:_IN_OFF[last + 1]].astype(BF16)

    at = 0
    for first, last in _HEAD_RUNS:
        width = _IN_OFF[last + 1] - _IN_OFF[first]
        heads_ref[:, at:at + width] = run(first, last)
        at += width
    aux_ref[...] = jnp.zeros(aux_ref.shape, BF16)
    aux_ref[:, 0:2 * LANES] = run(1, 2)
    aux_ref[:, AUX_G * LANES:AUX_G * LANES + _IN_SPLITS[7]] = run(7, 7)
    aux_ref[:, AUX_F * LANES:AUX_F * LANES + _IN_SPLITS[14]] = run(14, 14)


def _split_w_in(w, rows=256):
    depth, d, n_in = w.shape
    n_heads, n_aux = HEAD_BLOCKS * LANES, AUX_BLOCKS * LANES
    return pl.pallas_call(
        _repack_kernel,
        grid=(depth, d // rows),
        in_specs=[pl.BlockSpec((None, rows, n_in), lambda l, r: (l, r, 0))],
        out_specs=[pl.BlockSpec((None, rows, n_heads), lambda l, r: (l, r, 0)),
                   pl.BlockSpec((None, rows, n_aux), lambda l, r: (l, r, 0))],
        out_shape=[jax.ShapeDtypeStruct((depth, d, n_heads), BF16),
                   jax.ShapeDtypeStruct((depth, d, n_aux), BF16)],
        compiler_params=_params(2),
        name="repack_w_in",
    )(w)


def _rope_lane_tables(seq, rot_dim, period):
    pos = jnp.arange(seq, dtype=F32)
    inv = ROPE_THETA ** (-jnp.arange(0, rot_dim, 2, dtype=F32) / rot_dim)
    ang = pos[:, None] * inv[None, :]
    cos, sin = jnp.cos(ang), jnp.sin(ang)
    rest = period - rot_dim
    c = jnp.concatenate([cos, cos, jnp.ones((seq, rest), F32)], axis=-1)
    s = jnp.concatenate([-sin, sin, jnp.zeros((seq, rest), F32)], axis=-1)
    reps = LANES // period
    return jnp.tile(c, (1, reps)), jnp.tile(s, (1, reps))


@jax.jit
def _forward(x, norm_mix_pre, norm_mix_post, norm_mlp_pre, norm_mlp_post, w_in, w_out,
             nsa_pe_k, nsa_pe_v, nsa_cmp_k1, nsa_cmp_k2, nsa_cmp_v1, nsa_cmp_v2,
             fox_forget_bias, diff_lambda, diff_norm, mlp_w_up, mlp_conv_w, mlp_conv_b,
             mlp_w_down):
    b, s, d = x.shape
    depth = w_in.shape[0]
    cos_h, sin_h = _rope_lane_tables(s, HEAD_DIM // 4, LANES)
    cos_d, sin_d = _rope_lane_tables(s, DIFF_QK // 4, DIFF_QK)
    cos_sets = jnp.stack([cos_h * SCALE_HEAD, cos_h, cos_d * SCALE_DIFF, cos_d])
    sin_sets = jnp.stack([sin_h * SCALE_HEAD, sin_h, sin_d * SCALE_DIFF, sin_d])
    w_heads, w_aux = _split_w_in(w_in)
    w_out_b = w_out.astype(BF16)
    w_up_b = mlp_w_up.astype(BF16)
    w_down_b = mlp_w_down.astype(BF16)
    k1_b, k2_b = nsa_cmp_k1.astype(BF16), nsa_cmp_k2.astype(BF16)
    v1_b, v2_b = nsa_cmp_v1.astype(BF16), nsa_cmp_v2.astype(BF16)
    fox_bias = jnp.pad(fox_forget_bias, ((0, 0), (0, LANES - N_HEADS)))
    row2 = lambda a: a.reshape(1, -1)
    cmp_w = NSA_CMP_STRIDE * HEAD_DIM

    xt = x.reshape(b * s, d)
    for l in range(depth):
        g_pre = row2(norm_mix_pre[l])
        proj = _head_proj(xt, g_pre, w_heads, l, cos_sets, sin_sets, s, tm=min(1024, s))
        proj = proj.reshape(b, s, HEAD_BLOCKS * LANES)
        aux = _aux_proj(xt, g_pre, w_aux, l).reshape(b, s, AUX_BLOCKS * LANES)

        k2 = aux[:, :, AUX_KC * LANES:(AUX_KC + 1) * LANES].reshape(b, s // NSA_CMP_STRIDE, cmp_w)
        v2 = aux[:, :, AUX_VC * LANES:(AUX_VC + 1) * LANES].reshape(b, s // NSA_CMP_STRIDE, cmp_w)
        kc, vc = _nsa_compress(k2, v2, row2(nsa_pe_k[l]), row2(nsa_pe_v[l]),
                               k1_b[l], k2_b[l], v1_b[l], v2_b[l])
        o_a = _nsa_attention(proj, aux, kc, vc)

        o_b = _sb_attention(proj)

        cum_col, cum_row = _fox_cum(aux, fox_bias[l:l + 1])
        o_c = _fox_attention(proj, cum_col, cum_row)

        lam_init = 0.8 - 0.6 * math.exp(-0.3 * l)
        dl = jnp.concatenate([
            jnp.pad(diff_lambda[l], ((0, 0), (0, LANES - DIFF_QK))),
            jnp.full((1, LANES), lam_init, F32),
            jnp.full((1, LANES), 1.0 - lam_init, F32),
            jnp.zeros((2, LANES), F32)], axis=0)
        o_d = _diff_attention(proj, dl, row2(diff_norm[l]))

        flat = lambda o: o.reshape(b * s, GROUP)
        xt = _out_proj(flat(o_a), flat(o_b), flat(o_c), flat(o_d), w_out_b, l, xt,
                       row2(norm_mix_post[l]))
        xt = _mlp(xt, row2(norm_mlp_pre[l]), w_up_b, mlp_conv_w[l], row2(mlp_conv_b[l]),
                  w_down_b, l, row2(norm_mlp_post[l]), s, tm=min(1024, s))
    return xt.reshape(b, s, d)


def kernel(x, norm_mix_pre, norm_mix_post, norm_mlp_pre, norm_mlp_post, w_in, w_out, nsa_pe_k, nsa_pe_v, nsa_cmp_k1, nsa_cmp_k2, nsa_cmp_v1, nsa_cmp_v2, fox_forget_bias, diff_lambda, diff_norm, mlp_w_up, mlp_conv_w, mlp_conv_b, mlp_w_down):
    return _forward(x, norm_mix_pre, norm_mix_post, norm_mlp_pre, norm_mlp_post, w_in, w_out,
                    nsa_pe_k, nsa_pe_v, nsa_cmp_k1, nsa_cmp_k2, nsa_cmp_v1, nsa_cmp_v2,
                    fox_forget_bias, diff_lambda, diff_norm, mlp_w_up, mlp_conv_w, mlp_conv_b,
                    mlp_w_down)
```

```python
import functools
import math

import jax
import jax.numpy as jnp
import numpy as np
from jax import lax
from jax.experimental import pallas as pl
from jax.experimental.pallas import tpu as pltpu

F32 = jnp.float32
BF16 = jnp.bfloat16

LANES = 128
HEAD_DIM = 128
N_HEADS = 4
GROUP = N_HEADS * HEAD_DIM
NSA_CMP_BLOCK = 32
NSA_CMP_STRIDE = 16
NSA_SLC_BLOCK = 64
NSA_TOPK = 16
NSA_WINDOW = 512
DIFF_QK = HEAD_DIM // 2
ROPE_HALF_HEAD = HEAD_DIM // 4 // 2
ROPE_HALF_DIFF = DIFF_QK // 4 // 2
EPS = 1e-6
NEG_INF = -1e30
FORCE_SCORE = 1e9
ROPE_THETA = 500000.0
LOG2E = math.log2(math.e)
SCALE_HEAD = HEAD_DIM ** -0.5 * LOG2E
SCALE_DIFF = DIFF_QK ** -0.5 * LOG2E

_IN_SPLITS = (GROUP, 128, 128, 128, 128, 128, 128, 3 * N_HEADS,
              GROUP, GROUP, GROUP, GROUP, GROUP, GROUP, N_HEADS, GROUP, GROUP, GROUP)
COL_AQ, COL_AQR, COL_AKS, COL_AVS, COL_AKW, COL_AVW = 0, 4, 8, 9, 10, 11
COL_BQ, COL_BK, COL_BV = 12, 16, 20
COL_CQ, COL_CK, COL_CV = 24, 28, 32
COL_DQ, COL_DK, COL_DV = 36, 40, 44
HEAD_BLOCKS = 48
AUX_KC, AUX_VC, AUX_G, AUX_F = 0, 1, 2, 3
AUX_BLOCKS = 4

PROJ_TN = GROUP
_TILES_SCALE = (0, 3, 6)
_TILE_ROPE_Q = 1
_TILE_ROPE_KV = 2
_TILES_ROPE_DIFF = (9, 10)
_ROPE_SET_STARTS = (2, 9, 10)

VMEM_LIMIT = 56 * 1024 * 1024


def _params(n_axes, vmem=VMEM_LIMIT):
    return pltpu.CompilerParams(dimension_semantics=("arbitrary",) * n_axes,
                                vmem_limit_bytes=vmem)


def _rms(x, g):
    ms = jnp.mean(x * x, axis=-1, keepdims=True)
    return x * lax.rsqrt(ms + EPS) * g


def _dot(a, b):
    return jnp.dot(a, b, preferred_element_type=F32)


def _dot_nt(a, b):
    return lax.dot_general(a, b, (((1,), (1,)), ((), ())), preferred_element_type=F32)


def _split3(x):
    hi = x.astype(BF16)
    r = x - hi.astype(F32)
    mid = r.astype(BF16)
    lo = (r - mid.astype(F32)).astype(BF16)
    return hi, mid, lo


def _dot3(x, m):
    hi, mid, lo = _split3(x)
    return _dot(hi, m) + _dot(mid, m) + _dot(lo, m)


def _dot2(x, m):
    hi = x.astype(BF16)
    lo = (x - hi.astype(F32)).astype(BF16)
    return _dot(hi, m) + _dot(lo, m)


def _dot3_left(m, x):
    hi, mid, lo = _split3(x)
    return _dot(m, hi) + _dot(m, mid) + _dot(m, lo)


def _rope(x, c, s, half):
    width = x.shape[1]
    lane = lax.broadcasted_iota(jnp.int32, (1, width), 1)
    first = (lane % (2 * half)) < half
    xs = jnp.where(first, pltpu.roll(x, width - half, 1), pltpu.roll(x, half, 1))
    return x * c + xs * s


def _lane_fold(x, op):
    out = x[:, :LANES]
    for b in range(1, x.shape[1] // LANES):
        out = op(out, x[:, b * LANES:(b + 1) * LANES])
    return out


def _row_max(s):
    return jnp.max(_lane_fold(s, jnp.maximum), axis=1, keepdims=True)


def _flash_scratch(n_chain, rows, tk):
    stat = pltpu.VMEM((n_chain, rows, LANES), F32)
    return [pltpu.VMEM((n_chain, rows, tk), F32), pltpu.VMEM((n_chain, rows, tk), F32),
            pltpu.VMEM((n_chain, rows, tk), BF16), stat, stat,
            pltpu.VMEM((n_chain, rows, 2 * LANES), F32)]


def _flash_sweep_ref(lo, n_full, n_chain, score_fn, value_fn, mask_fn, rows, tk, bufs, rc=16):
    s_even, s_odd, p_buf, m_ref, a_ref, acc_ref = bufs
    chains = range(n_chain)
    reps = tk // LANES
    ones = jnp.ones((tk, LANES), BF16)
    m_ref[...] = jnp.full(m_ref.shape, NEG_INF, F32)
    acc_ref[...] = jnp.zeros(acc_ref.shape, F32)

    def scores(s_buf, j):
        for c in chains:
            s_buf[c] = score_fn(c, j)

    def consume(s_buf, j, masked):
        for c in chains:
            for r0 in range(0, rows, rc):
                rr = slice(r0, r0 + rc)
                s = s_buf[c, rr, :]
                if masked:
                    s = mask_fn(j, s, r0)
                m_old = m_ref[c, rr, :]
                m_new = jnp.maximum(m_old, _row_max(s))
                m_ref[c, rr, :] = m_new
                a_ref[c, rr, :] = jnp.exp2(m_old - m_new)
                shifted = s - jnp.concatenate([m_new] * reps, axis=1)
                p_buf[c, rr, :] = jnp.exp2(shifted.astype(BF16))
            v_ones = jnp.concatenate([value_fn(c, j), ones], axis=1)
            alpha = a_ref[c]
            acc_ref[c] = (jnp.concatenate([alpha, alpha], axis=1) * acc_ref[c]
                          + _dot(p_buf[c], v_ones))

    scores(s_even, lo)

    def pair(t, carry):
        j = lo + 2 * t
        scores(s_odd, j + 1)
        consume(s_even, j, False)
        scores(s_even, j + 2)
        consume(s_odd, j + 1, False)
        return carry

    n_pair = (n_full - lo) // 2
    lax.fori_loop(0, n_pair, pair, 0)
    j = lo + 2 * n_pair
    last_full = j < n_full

    @pl.when(last_full)
    def _():
        scores(s_odd, j + 1)

    consume(s_even, j, True)

    @pl.when(last_full)
    def _():
        consume(s_odd, j + 1, True)

    return [acc_ref[c, :, :LANES] * (1.0 / acc_ref[c, :, LANES:]) for c in chains]


def _head(ref, h, rows=None):
    cols = slice(h * HEAD_DIM, (h + 1) * HEAD_DIM)
    return ref[:, cols] if rows is None else ref[rows, cols]


def _key_rows(j, tk):
    return pl.ds(pl.multiple_of(j * tk, tk), tk)


def _tile_in(j, tiles):
    hit = j == tiles[0]
    for t in tiles[1:]:
        hit = hit | (j == t)
    return hit


def _head_proj_kernel(x_ref, g_ref, w_ref, c_ref, s_ref, o_ref, xn_ref, *, slabs):
    j = pl.program_id(1)

    @pl.when(j == 0)
    def _():
        xn_ref[...] = _rms(x_ref[...], g_ref[...]).astype(BF16)

    is_scale = _tile_in(j, _TILES_SCALE)
    is_rope_d = _tile_in(j, _TILES_ROPE_DIFF)
    is_rope = (j == _TILE_ROPE_Q) | (j == _TILE_ROPE_KV) | is_rope_d
    slab = o_ref.shape[0] // slabs

    def tile(epilogue):
        for r in range(slabs):
            rows = slice(r * slab, (r + 1) * slab)
            o_ref[rows, :] = epilogue(_dot(xn_ref[rows, :], w_ref[...]), rows).astype(BF16)

    def rope_tile(half, rotated):
        def epilogue(acc, rows):
            ones = jnp.ones((slab, LANES), F32)
            c = jnp.concatenate([c_ref[rows, :] if r else ones for r in rotated], axis=1)
            s = jnp.concatenate([s_ref[rows, :] if r else 0.0 * ones for r in rotated], axis=1)
            return _rope(acc, c, s, half)
        tile(epilogue)

    every = (True,) * N_HEADS
    pl.when(is_scale)(lambda: tile(lambda acc, rows: acc * SCALE_HEAD))
    pl.when(j == _TILE_ROPE_Q)(lambda: rope_tile(ROPE_HALF_HEAD, every))
    pl.when(j == _TILE_ROPE_KV)(lambda: rope_tile(ROPE_HALF_HEAD, (True, False, True, False)))
    pl.when(is_rope_d)(lambda: rope_tile(ROPE_HALF_DIFF, every))
    pl.when(jnp.logical_not(is_scale | is_rope))(lambda: tile(lambda acc, rows: acc))


def _rope_set(j):
    idx = 0
    for start in _ROPE_SET_STARTS:
        idx = idx + (j >= start).astype(jnp.int32)
    return idx


def _head_proj(x2d, g, w, layer, cos_sets, sin_sets, seq, tm=1024, slabs=4):
    t, d = x2d.shape
    n = w.shape[2]
    per_seq = seq // tm
    table = pl.BlockSpec((None, tm, LANES), lambda i, j: (_rope_set(j), i % per_seq, 0))
    return pl.pallas_call(
        functools.partial(_head_proj_kernel, slabs=slabs),
        grid=(t // tm, n // PROJ_TN),
        in_specs=[pl.BlockSpec((tm, d), lambda i, j: (i, 0)),
                  pl.BlockSpec((1, d), lambda i, j: (0, 0)),
                  pl.BlockSpec((None, d, PROJ_TN), lambda i, j: (layer, 0, j)),
                  table, table],
        out_specs=pl.BlockSpec((tm, PROJ_TN), lambda i, j: (i, j)),
        out_shape=jax.ShapeDtypeStruct((t, n), BF16),
        scratch_shapes=[pltpu.VMEM((tm, d), BF16)],
        compiler_params=_params(2),
        name="head_proj",
    )(x2d, g, w, cos_sets, sin_sets)


def _norm_matmul_kernel(x_ref, g_ref, w_ref, o_ref):
    o_ref[...] = _dot(_rms(x_ref[...], g_ref[...]).astype(BF16), w_ref[...])


def _aux_proj(x2d, g, w, layer, tm=512):
    t, d = x2d.shape
    n = w.shape[2]
    return pl.pallas_call(
        _norm_matmul_kernel,
        grid=(t // tm,),
        in_specs=[pl.BlockSpec((tm, d), lambda i: (i, 0)),
                  pl.BlockSpec((1, d), lambda i: (0, 0)),
                  pl.BlockSpec((None, d, n), lambda i: (layer, 0, 0))],
        out_specs=pl.BlockSpec((tm, n), lambda i: (i, 0)),
        out_shape=jax.ShapeDtypeStruct((t, n), F32),
        compiler_params=_params(1),
        name="aux_proj",
    )(x2d, g, w)


def _sb_kernel(q_ref, k_ref, v_ref, o_ref,
               ls_e, ls_o, hl_e, hl_o, tot_e, tot_o, suf_ref, e_ref, c_ref, acc_ref,
               *, tq, tk, hp, rc):
    assert tq == tk
    i = pl.program_id(2)
    reps = tk // LANES
    heads = range(hp)
    jr = lax.broadcasted_iota(jnp.int32, (tk, tk), 0)
    jc = lax.broadcasted_iota(jnp.int32, (tk, tk), 1)
    u = jnp.where(jr > jc, 1.0, 0.0).astype(BF16)
    u2 = jnp.concatenate([u, u], axis=0)
    qs = [_head(q_ref, h) for h in heads]
    c_ref[...] = jnp.zeros(c_ref.shape, F32)
    acc_ref[...] = jnp.zeros(acc_ref.shape, F32)

    def prepare(bufs, j, diagonal):
        ls_ref, hl_ref, tot_ref = bufs
        for h in heads:
            z = _dot_nt(qs[h], _head(k_ref, h, _key_rows(j, tk)))
            for r0 in range(0, tq, rc):
                rr = slice(r0, r0 + rc)
                zc = z[rr]
                ls = jnp.minimum(zc, 0.0) - jnp.log2(1.0 + jnp.exp2(-jnp.abs(zc)))
                lk = ls - zc
                if diagonal:
                    row = r0 + lax.broadcasted_iota(jnp.int32, (rc, 1), 0)
                    col = lax.broadcasted_iota(jnp.int32, (1, tk), 1)
                    past = col < row
                    lk = jnp.where(past, lk, 0.0)
                    ls = jnp.where(past, ls, NEG_INF)
                hi = lk.astype(BF16)
                hl_ref[h * tq + r0:h * tq + r0 + rc, :tk] = hi
                hl_ref[h * tq + r0:h * tq + r0 + rc, tk:] = (lk - hi.astype(F32)).astype(BF16)
                ls_ref[h, rr, :] = ls
                tot = jnp.sum(_lane_fold(lk, jnp.add), axis=1, keepdims=True)
                tot_ref[h, rr, :] = jnp.broadcast_to(tot, (rc, LANES))

    def weigh(bufs, j):
        ls_ref, hl_ref, tot_ref = bufs
        suf_ref[...] = _dot(hl_ref[...], u2)
        for h in heads:
            for r0 in range(0, tq, rc):
                rr = slice(r0, r0 + rc)
                c = c_ref[h, rr, :]
                suf = suf_ref[h * tq + r0:h * tq + r0 + rc, :]
                pre = ls_ref[h, rr, :] + suf + jnp.concatenate([c] * reps, axis=1)
                e_ref[h, rr, :] = jnp.exp2(pre).astype(BF16)
                c_ref[h, rr, :] = c + tot_ref[h, rr, :]
            acc_ref[h] += _dot(e_ref[h], _head(v_ref, h, _key_rows(j, tk)))

    even = (ls_e, hl_e, tot_e)
    odd = (ls_o, hl_o, tot_o)
    prepare(even, i, True)

    def pair(p, carry):
        j = i - 2 * p
        prepare(odd, j - 1, False)
        weigh(even, j)
        prepare(even, j - 2, False)
        weigh(odd, j - 1)
        return carry

    n_pair = i // 2
    lax.fori_loop(0, n_pair, pair, 0)
    j = i - 2 * n_pair
    one_more = j > 0

    @pl.when(one_more)
    def _():
        prepare(odd, j - 1, False)

    weigh(even, j)

    @pl.when(one_more)
    def _():
        weigh(odd, j - 1)

    for h in heads:
        o_ref[:, h * HEAD_DIM:(h + 1) * HEAD_DIM] = acc_ref[h].astype(BF16)


def _sb_attention(proj, tq=256, tk=256, hp=4, rc=16):
    b, s, _ = proj.shape
    w = hp * HEAD_DIM
    wide_f32 = pltpu.VMEM((hp, tq, tk), F32)
    wide_bf16 = pltpu.VMEM((hp, tq, tk), BF16)
    stat = pltpu.VMEM((hp, tq, LANES), F32)
    return pl.pallas_call(
        functools.partial(_sb_kernel, tq=tq, tk=tk, hp=hp, rc=rc),
        grid=(b, N_HEADS // hp, s // tq),
        in_specs=[pl.BlockSpec((None, tq, w), lambda bi, hg, i: (bi, i, COL_BQ // hp + hg)),
                  pl.BlockSpec((None, s, w), lambda bi, hg, i: (bi, 0, COL_BK // hp + hg)),
                  pl.BlockSpec((None, s, w), lambda bi, hg, i: (bi, 0, COL_BV // hp + hg))],
        out_specs=pl.BlockSpec((None, tq, w), lambda bi, hg, i: (bi, i, hg)),
        out_shape=jax.ShapeDtypeStruct((b, s, GROUP), BF16),
        scratch_shapes=[wide_f32, wide_f32,
                        pltpu.VMEM((hp * tq, 2 * tk), BF16), pltpu.VMEM((hp * tq, 2 * tk), BF16),
                        stat, stat, pltpu.VMEM((hp * tq, tk), F32), wide_bf16, stat, stat],
        compiler_params=_params(3),
        name="sb_attention",
    )(proj, proj, proj)


def _fox_cum_kernel(cf_ref, bias_ref, col_ref, row_ref, *, chunk):
    s = cf_ref.shape[0]
    lf = jax.nn.log_sigmoid(cf_ref[...] + bias_ref[...]) * LOG2E
    rr = lax.broadcasted_iota(jnp.int32, (chunk, chunk), 0)
    cc = lax.broadcasted_iota(jnp.int32, (chunk, chunk), 1)
    tri = jnp.where(rr >= cc, 1.0, 0.0).astype(BF16)
    carry = jnp.zeros((1, LANES), F32)
    for c in range(s // chunk):
        pre = _dot3_left(tri, lf[c * chunk:(c + 1) * chunk]) + carry
        col_ref[c * chunk:(c + 1) * chunk, :] = pre
        carry = pre[chunk - 1:chunk, :]
    row_ref[...] = jnp.transpose(col_ref[...])[:8, :]


def _fox_cum(aux, bias_row, chunk=256):
    b, s, _ = aux.shape
    return pl.pallas_call(
        functools.partial(_fox_cum_kernel, chunk=chunk),
        grid=(b,),
        in_specs=[pl.BlockSpec((None, s, LANES), lambda bi: (bi, 0, AUX_F)),
                  pl.BlockSpec((1, LANES), lambda bi: (0, 0))],
        out_specs=[pl.BlockSpec((None, s, LANES), lambda bi: (bi, 0, 0)),
                   pl.BlockSpec((None, 8, s), lambda bi: (bi, 0, 0))],
        out_shape=[jax.ShapeDtypeStruct((b, s, LANES), F32),
                   jax.ShapeDtypeStruct((b, 8, s), F32)],
        compiler_params=_params(1),
        name="fox_cum",
    )(aux, bias_row)


def _fox_kernel(q_ref, k_ref, v_ref, cq_ref, ck_ref, o_ref, *bufs, tq, tk, hp):
    assert tk >= tq
    hg = pl.program_id(1)
    i = pl.program_id(2)
    lane = lax.broadcasted_iota(jnp.int32, (1, LANES), 1)
    cum_q = cq_ref[...]
    qs = [_head(q_ref, h) for h in range(hp)]
    cqs = [jnp.sum(jnp.where(lane == hg * hp + h, cum_q, 0.0), axis=1, keepdims=True)
           for h in range(hp)]

    def score(h, j):
        return (_dot_nt(qs[h], _head(k_ref, h, _key_rows(j, tk))) + cqs[h]
                - ck_ref[h, pl.ds(j, 1), :])

    def value(h, j):
        return _head(v_ref, h, _key_rows(j, tk))

    def causal(j, s, r0):
        row = i * tq + r0 + lax.broadcasted_iota(jnp.int32, (s.shape[0], 1), 0)
        col = j * tk + lax.broadcasted_iota(jnp.int32, (1, tk), 1)
        return jnp.where(col <= row, s, NEG_INF)

    outs = _flash_sweep_ref(0, (i * tq) // tk, hp, score, value, causal, tq, tk, bufs)
    for h in range(hp):
        o_ref[:, h * HEAD_DIM:(h + 1) * HEAD_DIM] = outs[h].astype(BF16)


def _fox_attention(proj, cum_col, cum_row, tq=256, tk=512, hp=4):
    b, s, _ = proj.shape
    w = hp * HEAD_DIM
    ck = cum_row.reshape(b, 8, s // tk, tk)
    return pl.pallas_call(
        functools.partial(_fox_kernel, tq=tq, tk=tk, hp=hp),
        grid=(b, N_HEADS // hp, s // tq),
        in_specs=[pl.BlockSpec((None, tq, w), lambda bi, hg, i: (bi, i, COL_CQ // hp + hg)),
                  pl.BlockSpec((None, s, w), lambda bi, hg, i: (bi, 0, COL_CK // hp + hg)),
                  pl.BlockSpec((None, s, w), lambda bi, hg, i: (bi, 0, COL_CV // hp + hg)),
                  pl.BlockSpec((None, tq, LANES), lambda bi, hg, i: (bi, i, 0)),
                  pl.BlockSpec((None, hp, s // tk, tk), lambda bi, hg, i: (bi, hg, 0, 0))],
        out_specs=pl.BlockSpec((None, tq, w), lambda bi, hg, i: (bi, i, hg)),
        out_shape=jax.ShapeDtypeStruct((b, s, GROUP), BF16),
        scratch_shapes=_flash_scratch(hp, tq, tk),
        compiler_params=_params(3),
        name="fox_attention",
    )(proj, proj, proj, cum_col, ck)


def _diff_kernel(q_ref, k_ref, v_ref, dl_ref, gn_ref, o_ref, *bufs, tq, tk, hp):
    assert tk >= tq
    i = pl.program_id(2)
    lane = lax.broadcasted_iota(jnp.int32, (1, LANES), 1)
    zero = jnp.zeros((), BF16)
    qs = []
    for h in range(hp):
        qh = _head(q_ref, h)
        qs.append(jnp.concatenate([jnp.where(lane < DIFF_QK, qh, zero),
                                   jnp.where(lane >= DIFF_QK, qh, zero)], axis=0))

    def score(h, j):
        return _dot_nt(qs[h], _head(k_ref, h, _key_rows(j, tk)))

    def value(h, j):
        return _head(v_ref, h, _key_rows(j, tk))

    def causal(j, s, r0):
        row = i * tq + r0 % tq + lax.broadcasted_iota(jnp.int32, (s.shape[0], 1), 0)
        col = j * tk + lax.broadcasted_iota(jnp.int32, (1, tk), 1)
        return jnp.where(col <= row, s, NEG_INF)

    outs = _flash_sweep_ref(0, (i * tq) // tk, hp, score, value, causal, 2 * tq, tk, bufs,
                            rc=16)
    dl = dl_ref[...]
    lam = (jnp.exp(jnp.sum(dl[0:1] * dl[1:2], axis=1, keepdims=True))
           - jnp.exp(jnp.sum(dl[2:3] * dl[3:4], axis=1, keepdims=True)) + dl[4:5, 0:1])
    for h in range(hp):
        od = outs[h][:tq] - lam * outs[h][tq:]
        o_ref[:, h * HEAD_DIM:(h + 1) * HEAD_DIM] = (
            _rms(od, gn_ref[...]) * dl[5:6, 0:1]).astype(BF16)


def _diff_attention(proj, dl, gn, tq=256, tk=512, hp=4):
    b, s, _ = proj.shape
    w = hp * HEAD_DIM
    return pl.pallas_call(
        functools.partial(_diff_kernel, tq=tq, tk=tk, hp=hp),
        grid=(b, N_HEADS // hp, s // tq),
        in_specs=[pl.BlockSpec((None, tq, w), lambda bi, hg, i: (bi, i, COL_DQ // hp + hg)),
                  pl.BlockSpec((None, s, w), lambda bi, hg, i: (bi, 0, COL_DK // hp + hg)),
                  pl.BlockSpec((None, s, w), lambda bi, hg, i: (bi, 0, COL_DV // hp + hg)),
                  pl.BlockSpec((8, LANES), lambda bi, hg, i: (0, 0)),
                  pl.BlockSpec((1, LANES), lambda bi, hg, i: (0, 0))],
        out_specs=pl.BlockSpec((None, tq, w), lambda bi, hg, i: (bi, i, hg)),
        out_shape=jax.ShapeDtypeStruct((b, s, GROUP), BF16),
        scratch_shapes=_flash_scratch(hp, 2 * tq, tk),
        compiler_params=_params(3),
        name="diff_attention",
    )(proj, proj, proj, dl, gn)


def _cmp_kernel(k2_ref, v2_ref, pek_ref, pev_ref, wk1_ref, wk2_ref, wv1_ref, wv2_ref,
                kc_ref, vc_ref):
    nb = k2_ref.shape[0]
    w = NSA_CMP_STRIDE * HEAD_DIM

    def compress(x2, pe, w1_ref, w2_ref):
        a = _dot((x2 + pe[:, :w]).astype(BF16), w1_ref[:w, :])
        bb = _dot((x2 + pe[:, w:]).astype(BF16), w1_ref[w:, :])
        hid = jax.nn.gelu(a + pltpu.roll(bb, nb - 1, 0), approximate=True)
        return _dot(hid.astype(BF16), w2_ref[...])

    kc_ref[...] = compress(k2_ref[...], pek_ref[...], wk1_ref, wk2_ref).astype(BF16)
    vc_ref[...] = compress(v2_ref[...], pev_ref[...], wv1_ref, wv2_ref).astype(BF16)


def _nsa_compress(k2, v2, pek, pev, wk1, wk2, wv1, wv2):
    b, nb, w = k2.shape
    hid = wk1.shape[1]
    full = lambda *shape: pl.BlockSpec(shape, lambda bi: (0,) * len(shape))
    return pl.pallas_call(
        _cmp_kernel,
        grid=(b,),
        in_specs=[pl.BlockSpec((None, nb, w), lambda bi: (bi, 0, 0)),
                  pl.BlockSpec((None, nb, w), lambda bi: (bi, 0, 0)),
                  full(1, 2 * w), full(1, 2 * w),
                  full(2 * w, hid), full(hid, HEAD_DIM), full(2 * w, hid), full(hid, HEAD_DIM)],
        out_specs=[pl.BlockSpec((None, nb, HEAD_DIM), lambda bi: (bi, 0, 0)),
                   pl.BlockSpec((None, nb, HEAD_DIM), lambda bi: (bi, 0, 0))],
        out_shape=[jax.ShapeDtypeStruct((b, nb, HEAD_DIM), BF16),
                   jax.ShapeDtypeStruct((b, nb, HEAD_DIM), BF16)],
        compiler_params=_params(1),
        name="nsa_compress",
    )(k2, v2, pek, pev, wk1, wk2, wv1, wv2)


def _nsa_kernel(q_ref, qr_ref, g_ref, kc_ref, vc_ref, ks_ref, kw_ref, vs_ref, vw_ref, o_ref,
                *bufs, tq, tk, n_top):
    assert tk >= tq
    i = pl.program_id(1)
    nb = kc_ref.shape[0]
    n_sel = NSA_SLC_BLOCK
    hq = N_HEADS * tq

    q_raw = jnp.concatenate([_head(q_ref, h) for h in range(N_HEADS)], axis=0)
    q_rot = jnp.concatenate([_head(qr_ref, h) for h in range(N_HEADS)], axis=0)
    row = i * tq + lax.broadcasted_iota(jnp.int32, (tq, 1), 0)
    rep = lambda a: jnp.concatenate([a] * N_HEADS, axis=0)
    n_full = (i * tq) // tk

    def causal(j, s, r0):
        rows = i * tq + r0 % tq + lax.broadcasted_iota(jnp.int32, (s.shape[0], 1), 0)
        col = j * tk + lax.broadcasted_iota(jnp.int32, (1, tk), 1)
        return jnp.where(col <= rows, s, NEG_INF)

    cidx = lax.broadcasted_iota(jnp.int32, (1, nb), 1)
    c_valid = rep((cidx * NSA_CMP_STRIDE + (NSA_CMP_BLOCK - 1)) <= row)
    sc = jnp.where(c_valid, _dot_nt(q_raw, kc_ref[...]), NEG_INF)
    e = jnp.exp2(sc - jnp.max(sc, axis=1, keepdims=True))
    p = jnp.where(c_valid, e * (1.0 / jnp.sum(e, axis=1, keepdims=True)), 0.0)
    o_cmp = _dot(p.astype(BF16), vc_ref[...])
    p_sum = p[0:tq]
    for h in range(1, N_HEADS):
        p_sum = p_sum + p[h * tq:(h + 1) * tq]
    oc = lax.broadcasted_iota(jnp.int32, (nb, LANES), 0)
    oj = lax.broadcasted_iota(jnp.int32, (nb, LANES), 1)
    per = NSA_SLC_BLOCK // NSA_CMP_STRIDE
    overlap = (jnp.where(oc // per == oj, 1.0, 0.0)
               + jnp.where((oc + 1) // per == oj, 1.0, 0.0)).astype(BF16)
    imp = _dot3(p_sum, overlap)

    imp_t = jnp.transpose(imp)[:n_sel, :]
    jb = lax.broadcasted_iota(jnp.int32, (n_sel, 1), 0)
    cur = (i * tq + lax.broadcasted_iota(jnp.int32, (1, tq), 1)) // NSA_SLC_BLOCK
    sel_valid = jb <= cur
    forced = sel_valid & ((jb == 0) | (jb >= cur - 1))
    score = jnp.where(forced, FORCE_SCORE, jnp.where(sel_valid, imp_t, NEG_INF))
    cnt = jnp.zeros((n_sel, tq), F32)
    for ii in range(n_sel):
        ri = score[ii:ii + 1, :]
        beats = (ri > score) | ((ri == score) & (jb > ii))
        cnt = cnt + jnp.where(beats, 1.0, 0.0)
    sel_t = jnp.where((cnt < n_top) & sel_valid, 1.0, 0.0)
    sel = jnp.transpose(jnp.concatenate([sel_t, jnp.zeros_like(sel_t)], axis=0)).astype(BF16)

    er = lax.broadcasted_iota(jnp.int32, (LANES, tk), 0)
    ec = lax.broadcasted_iota(jnp.int32, (LANES, tk), 1)

    def slc_score(_, j):
        expand = jnp.where(er == (j * tk + ec) // NSA_SLC_BLOCK, 1.0, 0.0).astype(BF16)
        ok = _dot(sel, expand) > 0.5
        return jnp.where(rep(ok), _dot_nt(q_rot, ks_ref[_key_rows(j, tk), :]), NEG_INF)

    o_slc, = _flash_sweep_ref(0, n_full, 1, slc_score,
                              lambda _, j: vs_ref[_key_rows(j, tk), :], causal, hq, tk, bufs,
                              rc=16)

    def win_score(_, j):
        col = j * tk + lax.broadcasted_iota(jnp.int32, (1, tk), 1)
        return jnp.where(rep(col > row - NSA_WINDOW),
                         _dot_nt(q_rot, kw_ref[_key_rows(j, tk), :]), NEG_INF)

    j_lo = (jnp.maximum(i * tq - NSA_WINDOW, 0) // (2 * tk)) * 2
    o_win, = _flash_sweep_ref(j_lo, n_full, 1, win_score,
                              lambda _, j: vw_ref[_key_rows(j, tk), :], causal, hq, tk, bufs,
                              rc=16)

    g = jax.nn.sigmoid(g_ref[...])
    for h in range(N_HEADS):
        rows = slice(h * tq, (h + 1) * tq)
        o_ref[:, h * HEAD_DIM:(h + 1) * HEAD_DIM] = (
            g[:, 3 * h:3 * h + 1] * o_cmp[rows] + g[:, 3 * h + 1:3 * h + 2] * o_slc[rows]
            + g[:, 3 * h + 2:3 * h + 3] * o_win[rows]).astype(BF16)


def _nsa_attention(proj, aux, kc, vc, tq=256, tk=256):
    b, s, _ = proj.shape
    nb = kc.shape[1]
    n_top = min(NSA_TOPK, s // NSA_SLC_BLOCK)
    kv = lambda col: pl.BlockSpec((None, s, HEAD_DIM), lambda bi, i: (bi, 0, col))
    return pl.pallas_call(
        functools.partial(_nsa_kernel, tq=tq, tk=tk, n_top=n_top),
        grid=(b, s // tq),
        in_specs=[pl.BlockSpec((None, tq, GROUP), lambda bi, i: (bi, i, COL_AQ // N_HEADS)),
                  pl.BlockSpec((None, tq, GROUP), lambda bi, i: (bi, i, COL_AQR // N_HEADS)),
                  pl.BlockSpec((None, tq, LANES), lambda bi, i: (bi, i, AUX_G)),
                  pl.BlockSpec((None, nb, HEAD_DIM), lambda bi, i: (bi, 0, 0)),
                  pl.BlockSpec((None, nb, HEAD_DIM), lambda bi, i: (bi, 0, 0)),
                  kv(COL_AKS), kv(COL_AKW), kv(COL_AVS), kv(COL_AVW)],
        out_specs=pl.BlockSpec((None, tq, GROUP), lambda bi, i: (bi, i, 0)),
        out_shape=jax.ShapeDtypeStruct((b, s, GROUP), BF16),
        scratch_shapes=_flash_scratch(1, N_HEADS * tq, tk),
        compiler_params=_params(2),
        name="nsa_attention",
    )(proj, proj, aux, kc, vc, proj, proj, proj, proj)


def _out_kernel(oa_ref, ob_ref, oc_ref, od_ref, w_ref, x_ref, g_ref, o_ref):
    acc = _dot(oa_ref[...], w_ref[0:GROUP, :])
    acc = acc + _dot(ob_ref[...], w_ref[GROUP:2 * GROUP, :])
    acc = acc + _dot(oc_ref[...], w_ref[2 * GROUP:3 * GROUP, :])
    acc = acc + _dot(od_ref[...], w_ref[3 * GROUP:4 * GROUP, :])
    o_ref[...] = x_ref[...] + _rms(acc, g_ref[...])


def _out_proj(oa, ob, oc, od, w, layer, x2d, g, tm=512):
    t, d = x2d.shape
    mix = pl.BlockSpec((tm, GROUP), lambda i: (i, 0))
    return pl.pallas_call(
        _out_kernel,
        grid=(t // tm,),
        in_specs=[mix, mix, mix, mix,
                  pl.BlockSpec((None, 4 * GROUP, d), lambda i: (layer, 0, 0)),
                  pl.BlockSpec((tm, d), lambda i: (i, 0)),
                  pl.BlockSpec((1, d), lambda i: (0, 0))],
        out_specs=pl.BlockSpec((tm, d), lambda i: (i, 0)),
        out_shape=jax.ShapeDtypeStruct((t, d), F32),
        compiler_params=_params(1),
        name="out_proj",
    )(oa, ob, oc, od, w, x2d, g)


HALO = 16


def _mlp_kernel(x_ref, xh_ref, gpre_ref, wg_ref, wu_ref, cwg_ref, cwu_ref, cbg_ref, cbu_ref,
                wd_ref, gpost_ref, o_ref, xn_ref, acc_ref, *, tm, tiles_per_seq):
    i = pl.program_id(0)
    f = pl.program_id(1)

    @pl.when(f == 0)
    def _():
        xn_ref[HALO:, :] = _rms(x_ref[...], gpre_ref[...]).astype(BF16)
        halo = _rms(xh_ref[...], gpre_ref[...])
        seq_start = (i % tiles_per_seq) == 0
        xn_ref[:HALO, :] = jnp.where(seq_start, 0.0, halo).astype(BF16)
        acc_ref[...] = jnp.zeros_like(acc_ref)

    xn = xn_ref[...]

    def conv(u, cw_ref, cb_ref):
        cw = cw_ref[...]
        out = cb_ref[...] + cw[0:1] * u[HALO - 2:HALO - 2 + tm]
        out = out + cw[1:2] * u[HALO - 1:HALO - 1 + tm]
        return out + cw[2:3] * u[HALO:HALO + tm]

    gate = conv(_dot(xn, wg_ref[...]), cwg_ref, cbg_ref)
    up = conv(_dot(xn, wu_ref[...]), cwu_ref, cbu_ref)
    hid = (jax.nn.gelu(gate, approximate=True) * up).astype(BF16)
    acc_ref[...] += _dot(hid, wd_ref[...])

    @pl.when(f == pl.num_programs(1) - 1)
    def _():
        o_ref[...] = x_ref[...] + _rms(acc_ref[...], gpost_ref[...])


def _mlp(x2d, gpre, w_up, conv_w, conv_b, w_down, layer, gpost, seq, tm=1024, tf=512):
    t, d = x2d.shape
    ff = w_down.shape[1]
    nf = ff // tf
    hb = tm // HALO
    once = pl.Buffered(1)
    return pl.pallas_call(
        functools.partial(_mlp_kernel, tm=tm, tiles_per_seq=seq // tm),
        grid=(t // tm, nf),
        in_specs=[pl.BlockSpec((tm, d), lambda i, f: (i, 0), pipeline_mode=once),
                  pl.BlockSpec((HALO, d), lambda i, f: (jnp.maximum(i * hb - 1, 0), 0)),
                  pl.BlockSpec((1, d), lambda i, f: (0, 0)),
                  pl.BlockSpec((None, d, tf), lambda i, f: (layer, 0, f)),
                  pl.BlockSpec((None, d, tf), lambda i, f: (layer, 0, nf + f)),
                  pl.BlockSpec((3, tf), lambda i, f: (0, f)),
                  pl.BlockSpec((3, tf), lambda i, f: (0, nf + f)),
                  pl.BlockSpec((1, tf), lambda i, f: (0, f)),
                  pl.BlockSpec((1, tf), lambda i, f: (0, nf + f)),
                  pl.BlockSpec((None, tf, d), lambda i, f: (layer, f, 0)),
                  pl.BlockSpec((1, d), lambda i, f: (0, 0))],
        out_specs=pl.BlockSpec((tm, d), lambda i, f: (i, 0), pipeline_mode=once),
        out_shape=jax.ShapeDtypeStruct((t, d), F32),
        scratch_shapes=[pltpu.VMEM((tm + HALO, d), BF16), pltpu.VMEM((tm, d), F32)],
        compiler_params=_params(2),
        name="conv_geglu_mlp",
    )(x2d, x2d, gpre, w_up, w_up, conv_w, conv_w, conv_b, conv_b, w_down, gpost)


_IN_OFF = np.concatenate([[0], np.cumsum(_IN_SPLITS)]).tolist()
_HEAD_RUNS = ((0, 0), (0, 0), (3, 6), (8, 13), (15, 17))


def _repack_kernel(w_ref, heads_ref, aux_ref):
    def run(first, last):
        return w_ref[:, _IN_OFF[first]:_IN_OFF[last + 1]].astype(BF16)

    at = 0
    for first, last in _HEAD_RUNS:
        width = _IN_OFF[last + 1] - _IN_OFF[first]
        heads_ref[:, at:at + width] = run(first, last)
        at += width
    aux_ref[...] = jnp.zeros(aux_ref.shape, BF16)
    aux_ref[:, 0:2 * LANES] = run(1, 2)
    aux_ref[:, AUX_G * LANES:AUX_G * LANES + _IN_SPLITS[7]] = run(7, 7)
    aux_ref[:, AUX_F * LANES:AUX_F * LANES + _IN_SPLITS[14]] = run(14, 14)


def _split_w_in(w, rows=256):
    depth, d, n_in = w.shape
    n_heads, n_aux = HEAD_BLOCKS * LANES, AUX_BLOCKS * LANES
    return pl.pallas_call(
        _repack_kernel,
        grid=(depth, d // rows),
        in_specs=[pl.BlockSpec((None, rows, n_in), lambda l, r: (l, r, 0))],
        out_specs=[pl.BlockSpec((None, rows, n_heads), lambda l, r: (l, r, 0)),
                   pl.BlockSpec((None, rows, n_aux), lambda l, r: (l, r, 0))],
        out_shape=[jax.ShapeDtypeStruct((depth, d, n_heads), BF16),
                   jax.ShapeDtypeStruct((depth, d, n_aux), BF16)],
        compiler_params=_params(2),
        name="repack_w_in",
    )(w.astype(BF16))


def _rope_lane_tables(seq, rot_dim, period):
    pos = jnp.arange(seq, dtype=F32)
    inv = ROPE_THETA ** (-jnp.arange(0, rot_dim, 2, dtype=F32) / rot_dim)
    ang = pos[:, None] * inv[None, :]
    cos, sin = jnp.cos(ang), jnp.sin(ang)
    rest = period - rot_dim
    c = jnp.concatenate([cos, cos, jnp.ones((seq, rest), F32)], axis=-1)
    s = jnp.concatenate([-sin, sin, jnp.zeros((seq, rest), F32)], axis=-1)
    reps = LANES // period
    return jnp.tile(c, (1, reps)), jnp.tile(s, (1, reps))


@jax.jit
def _forward(x, norm_mix_pre, norm_mix_post, norm_mlp_pre, norm_mlp_post, w_in, w_out,
             nsa_pe_k, nsa_pe_v, nsa_cmp_k1, nsa_cmp_k2, nsa_cmp_v1, nsa_cmp_v2,
             fox_forget_bias, diff_lambda, diff_norm, mlp_w_up, mlp_conv_w, mlp_conv_b,
             mlp_w_down):
    b, s, d = x.shape
    depth = w_in.shape[0]
    cos_h, sin_h = _rope_lane_tables(s, HEAD_DIM // 4, LANES)
    cos_d, sin_d = _rope_lane_tables(s, DIFF_QK // 4, DIFF_QK)
    cos_sets = jnp.stack([cos_h * SCALE_HEAD, cos_h, cos_d * SCALE_DIFF, cos_d])
    sin_sets = jnp.stack([sin_h * SCALE_HEAD, sin_h, sin_d * SCALE_DIFF, sin_d])
    w_heads, w_aux = _split_w_in(w_in)
    w_out_b = w_out.astype(BF16)
    w_up_b = mlp_w_up.astype(BF16)
    w_down_b = mlp_w_down.astype(BF16)
    k1_b, k2_b = nsa_cmp_k1.astype(BF16), nsa_cmp_k2.astype(BF16)
    v1_b, v2_b = nsa_cmp_v1.astype(BF16), nsa_cmp_v2.astype(BF16)
    fox_bias = jnp.pad(fox_forget_bias, ((0, 0), (0, LANES - N_HEADS)))
    row2 = lambda a: a.reshape(1, -1)
    cmp_w = NSA_CMP_STRIDE * HEAD_DIM

    xt = x.reshape(b * s, d)
    for l in range(depth):
        g_pre = row2(norm_mix_pre[l])
        proj = _head_proj(xt, g_pre, w_heads, l, cos_sets, sin_sets, s, tm=min(1024, s))
        proj = proj.reshape(b, s, HEAD_BLOCKS * LANES)
        aux = _aux_proj(xt, g_pre, w_aux, l).reshape(b, s, AUX_BLOCKS * LANES)

        k2 = aux[:, :, AUX_KC * LANES:(AUX_KC + 1) * LANES].reshape(b, s // NSA_CMP_STRIDE, cmp_w)
        v2 = aux[:, :, AUX_VC * LANES:(AUX_VC + 1) * LANES].reshape(b, s // NSA_CMP_STRIDE, cmp_w)
        kc, vc = _nsa_compress(k2, v2, row2(nsa_pe_k[l]), row2(nsa_pe_v[l]),
                               k1_b[l], k2_b[l], v1_b[l], v2_b[l])
        o_a = _nsa_attention(proj, aux, kc, vc)

        o_b = _sb_attention(proj)

        cum_col, cum_row = _fox_cum(aux, fox_bias[l:l + 1])
        o_c = _fox_attention(proj, cum_col, cum_row)

        lam_init = 0.8 - 0.6 * math.exp(-0.3 * l)
        dl = jnp.concatenate([
            jnp.pad(diff_lambda[l], ((0, 0), (0, LANES - DIFF_QK))),
            jnp.full((1, LANES), lam_init, F32),
            jnp.full((1, LANES), 1.0 - lam_init, F32),
            jnp.zeros((2, LANES), F32)], axis=0)
        o_d = _diff_attention(proj, dl, row2(diff_norm[l]))

        flat = lambda o: o.reshape(b * s, GROUP)
        xt = _out_proj(flat(o_a), flat(o_b), flat(o_c), flat(o_d), w_out_b, l, xt,
                       row2(norm_mix_post[l]))
        xt = _mlp(xt, row2(norm_mlp_pre[l]), w_up_b, mlp_conv_w[l], row2(mlp_conv_b[l]),
                  w_down_b, l, row2(norm_mlp_post[l]), s, tm=min(1024, s))
    return xt.reshape(b, s, d)


def kernel(x, norm_mix_pre, norm_mix_post, norm_mlp_pre, norm_mlp_post, w_in, w_out, nsa_pe_k, nsa_pe_v, nsa_cmp_k1, nsa_cmp_k2, nsa_cmp_v1, nsa_cmp_v2, fox_forget_bias, diff_lambda, diff_norm, mlp_w_up, mlp_conv_w, mlp_conv_b, mlp_w_down):
    return _forward(x, norm_mix_pre, norm_mix_post, norm_mlp_pre, norm_mlp_post, w_in, w_out,
                    nsa_pe_k, nsa_pe_v, nsa_cmp_k1, nsa_cmp_k2, nsa_cmp_v1, nsa_cmp_v2,
                    fox_forget_bias, diff_lambda, diff_norm, mlp_w_up, mlp_conv_w, mlp_conv_b,
                    mlp_w_down)
```

```python
import functools
import math

import jax
import jax.numpy as jnp
import numpy as np
from jax import lax
from jax.experimental import pallas as pl
from jax.experimental.pallas import tpu as pltpu

F32 = jnp.float32
BF16 = jnp.bfloat16

LANES = 128
HEAD_DIM = 128
N_HEADS = 4
GROUP = N_HEADS * HEAD_DIM
NSA_CMP_BLOCK = 32
NSA_CMP_STRIDE = 16
NSA_SLC_BLOCK = 64
NSA_TOPK = 16
NSA_WINDOW = 512
DIFF_QK = HEAD_DIM // 2
ROPE_HALF_HEAD = HEAD_DIM // 4 // 2
ROPE_HALF_DIFF = DIFF_QK // 4 // 2
EPS = 1e-6
NEG_INF = -1e30
FORCE_SCORE = 1e9
ROPE_THETA = 500000.0
LOG2E = math.log2(math.e)
SCALE_HEAD = HEAD_DIM ** -0.5 * LOG2E
SCALE_DIFF = DIFF_QK ** -0.5 * LOG2E

_IN_SPLITS = (GROUP, 128, 128, 128, 128, 128, 128, 3 * N_HEADS,
              GROUP, GROUP, GROUP, GROUP, GROUP, GROUP, N_HEADS, GROUP, GROUP, GROUP)
COL_AQ, COL_AQR, COL_AKS, COL_AVS, COL_AKW, COL_AVW = 0, 4, 8, 9, 10, 11
COL_BQ, COL_BK, COL_BV = 12, 16, 20
COL_CQ, COL_CK, COL_CV = 24, 28, 32
COL_DQ, COL_DK, COL_DV = 36, 40, 44
HEAD_BLOCKS = 48
AUX_KC, AUX_VC, AUX_G, AUX_F = 0, 1, 2, 3
AUX_BLOCKS = 4

PROJ_TN = GROUP
_TILES_SCALE = (0, 3, 6)
_TILE_ROPE_Q = 1
_TILE_ROPE_KV = 2
_TILES_ROPE_DIFF = (9, 10)
_ROPE_SET_STARTS = (2, 9, 10)

VMEM_LIMIT = 56 * 1024 * 1024


def _params(n_axes, vmem=VMEM_LIMIT):
    return pltpu.CompilerParams(dimension_semantics=("arbitrary",) * n_axes,
                                vmem_limit_bytes=vmem)


def _rms(x, g):
    ms = jnp.mean(x * x, axis=-1, keepdims=True)
    return x * lax.rsqrt(ms + EPS) * g


def _dot(a, b):
    return jnp.dot(a, b, preferred_element_type=F32)


def _dot_nt(a, b):
    return lax.dot_general(a, b, (((1,), (1,)), ((), ())), preferred_element_type=F32)


def _split3(x):
    hi = x.astype(BF16)
    r = x - hi.astype(F32)
    mid = r.astype(BF16)
    lo = (r - mid.astype(F32)).astype(BF16)
    return hi, mid, lo


def _dot3(x, m):
    hi, mid, lo = _split3(x)
    return _dot(hi, m) + _dot(mid, m) + _dot(lo, m)


def _dot2(x, m):
    hi = x.astype(BF16)
    lo = (x - hi.astype(F32)).astype(BF16)
    return _dot(hi, m) + _dot(lo, m)


def _dot3_left(m, x):
    hi, mid, lo = _split3(x)
    return _dot(m, hi) + _dot(m, mid) + _dot(m, lo)


def _rope(x, c, s, half):
    width = x.shape[1]
    lane = lax.broadcasted_iota(jnp.int32, (1, width), 1)
    first = (lane % (2 * half)) < half
    xs = jnp.where(first, pltpu.roll(x, width - half, 1), pltpu.roll(x, half, 1))
    return x * c + xs * s


def _lane_fold(x, op):
    out = x[:, :LANES]
    for b in range(1, x.shape[1] // LANES):
        out = op(out, x[:, b * LANES:(b + 1) * LANES])
    return out


def _row_max(s):
    return jnp.max(_lane_fold(s, jnp.maximum), axis=1, keepdims=True)


def _flash_scratch(n_chain, rows, tk):
    stat = pltpu.VMEM((n_chain, rows, LANES), F32)
    return [pltpu.VMEM((n_chain, rows, tk), F32), pltpu.VMEM((n_chain, rows, tk), F32),
            pltpu.VMEM((n_chain, rows, tk), BF16), stat, stat,
            pltpu.VMEM((n_chain, rows, 2 * LANES), F32)]


def _flash_sweep_ref(lo, n_full, n_chain, score_fn, value_fn, mask_fn, rows, tk, bufs, rc=16):
    s_even, s_odd, p_buf, m_ref, a_ref, acc_ref = bufs
    chains = range(n_chain)
    reps = tk // LANES
    ones = jnp.ones((tk, LANES), BF16)
    m_ref[...] = jnp.full(m_ref.shape, NEG_INF, F32)
    acc_ref[...] = jnp.zeros(acc_ref.shape, F32)

    def scores(s_buf, j):
        for c in chains:
            s_buf[c] = score_fn(c, j)

    def consume(s_buf, j, masked):
        for c in chains:
            for r0 in range(0, rows, rc):
                rr = slice(r0, r0 + rc)
                s = s_buf[c, rr, :]
                if masked:
                    s = mask_fn(j, s, r0)
                m_old = m_ref[c, rr, :]
                m_new = jnp.maximum(m_old, _row_max(s))
                m_ref[c, rr, :] = m_new
                a_ref[c, rr, :] = jnp.exp2(m_old - m_new)
                shifted = s - jnp.concatenate([m_new] * reps, axis=1)
                p_buf[c, rr, :] = jnp.exp2(shifted.astype(BF16))
            v_ones = jnp.concatenate([value_fn(c, j), ones], axis=1)
            alpha = a_ref[c]
            acc_ref[c] = (jnp.concatenate([alpha, alpha], axis=1) * acc_ref[c]
                          + _dot(p_buf[c], v_ones))

    scores(s_even, lo)

    def pair(t, carry):
        j = lo + 2 * t
        scores(s_odd, j + 1)
        consume(s_even, j, False)
        scores(s_even, j + 2)
        consume(s_odd, j + 1, False)
        return carry

    n_pair = (n_full - lo) // 2
    lax.fori_loop(0, n_pair, pair, 0)
    j = lo + 2 * n_pair
    last_full = j < n_full

    @pl.when(last_full)
    def _():
        scores(s_odd, j + 1)

    consume(s_even, j, True)

    @pl.when(last_full)
    def _():
        consume(s_odd, j + 1, True)

    return [acc_ref[c, :, :LANES] * (1.0 / acc_ref[c, :, LANES:]) for c in chains]


def _head(ref, h, rows=None):
    cols = slice(h * HEAD_DIM, (h + 1) * HEAD_DIM)
    return ref[:, cols] if rows is None else ref[rows, cols]


def _key_rows(j, tk):
    return pl.ds(pl.multiple_of(j * tk, tk), tk)


def _tile_in(j, tiles):
    hit = j == tiles[0]
    for t in tiles[1:]:
        hit = hit | (j == t)
    return hit


def _head_proj_kernel(x_ref, g_ref, w_ref, c_ref, s_ref, o_ref, xn_ref, *, slabs):
    j = pl.program_id(1)

    @pl.when(j == 0)
    def _():
        xn_ref[...] = _rms(x_ref[...], g_ref[...]).astype(BF16)

    is_scale = _tile_in(j, _TILES_SCALE)
    is_rope_d = _tile_in(j, _TILES_ROPE_DIFF)
    is_rope = (j == _TILE_ROPE_Q) | (j == _TILE_ROPE_KV) | is_rope_d
    slab = o_ref.shape[0] // slabs

    def tile(epilogue):
        for r in range(slabs):
            rows = slice(r * slab, (r + 1) * slab)
            o_ref[rows, :] = epilogue(_dot(xn_ref[rows, :], w_ref[...]), rows).astype(BF16)

    def rope_tile(half, rotated):
        def epilogue(acc, rows):
            ones = jnp.ones((slab, LANES), F32)
            c = jnp.concatenate([c_ref[rows, :] if r else ones for r in rotated], axis=1)
            s = jnp.concatenate([s_ref[rows, :] if r else 0.0 * ones for r in rotated], axis=1)
            return _rope(acc, c, s, half)
        tile(epilogue)

    every = (True,) * N_HEADS
    pl.when(is_scale)(lambda: tile(lambda acc, rows: acc * SCALE_HEAD))
    pl.when(j == _TILE_ROPE_Q)(lambda: rope_tile(ROPE_HALF_HEAD, every))
    pl.when(j == _TILE_ROPE_KV)(lambda: rope_tile(ROPE_HALF_HEAD, (True, False, True, False)))
    pl.when(is_rope_d)(lambda: rope_tile(ROPE_HALF_DIFF, every))
    pl.when(jnp.logical_not(is_scale | is_rope))(lambda: tile(lambda acc, rows: acc))


def _rope_set(j):
    idx = 0
    for start in _ROPE_SET_STARTS:
        idx = idx + (j >= start).astype(jnp.int32)
    return idx


def _head_proj(x2d, g, w, layer, cos_sets, sin_sets, seq, tm=1024, slabs=4):
    t, d = x2d.shape
    n = w.shape[2]
    per_seq = seq // tm
    table = pl.BlockSpec((None, tm, LANES), lambda i, j: (_rope_set(j), i % per_seq, 0))
    return pl.pallas_call(
        functools.partial(_head_proj_kernel, slabs=slabs),
        grid=(t // tm, n // PROJ_TN),
        in_specs=[pl.BlockSpec((tm, d), lambda i, j: (i, 0)),
                  pl.BlockSpec((1, d), lambda i, j: (0, 0)),
                  pl.BlockSpec((None, d, PROJ_TN), lambda i, j: (layer, 0, j)),
                  table, table],
        out_specs=pl.BlockSpec((tm, PROJ_TN), lambda i, j: (i, j)),
        out_shape=jax.ShapeDtypeStruct((t, n), BF16),
        scratch_shapes=[pltpu.VMEM((tm, d), BF16)],
        compiler_params=_params(2),
        name="head_proj",
    )(x2d, g, w, cos_sets, sin_sets)


def _norm_matmul_kernel(x_ref, g_ref, w_ref, o_ref):
    o_ref[...] = _dot(_rms(x_ref[...], g_ref[...]).astype(BF16), w_ref[...])


def _aux_proj(x2d, g, w, layer, tm=512):
    t, d = x2d.shape
    n = w.shape[2]
    return pl.pallas_call(
        _norm_matmul_kernel,
        grid=(t // tm,),
        in_specs=[pl.BlockSpec((tm, d), lambda i: (i, 0)),
                  pl.BlockSpec((1, d), lambda i: (0, 0)),
                  pl.BlockSpec((None, d, n), lambda i: (layer, 0, 0))],
        out_specs=pl.BlockSpec((tm, n), lambda i: (i, 0)),
        out_shape=jax.ShapeDtypeStruct((t, n), F32),
        compiler_params=_params(1),
        name="aux_proj",
    )(x2d, g, w)


def _sb_kernel(q_ref, k_ref, v_ref, o_ref,
               ls_e, ls_o, hl_e, hl_o, tot_e, tot_o, suf_ref, e_ref, c_ref, acc_ref,
               *, tq, tk, hp, rc):
    assert tq == tk
    i = pl.program_id(2)
    reps = tk // LANES
    heads = range(hp)
    jr = lax.broadcasted_iota(jnp.int32, (tk, tk), 0)
    jc = lax.broadcasted_iota(jnp.int32, (tk, tk), 1)
    u = jnp.where(jr > jc, 1.0, 0.0).astype(BF16)
    u2 = jnp.concatenate([u, u], axis=0)
    qs = [_head(q_ref, h) for h in heads]
    c_ref[...] = jnp.zeros(c_ref.shape, F32)
    acc_ref[...] = jnp.zeros(acc_ref.shape, F32)

    def prepare(bufs, j, diagonal):
        ls_ref, hl_ref, tot_ref = bufs
        for h in heads:
            z = _dot_nt(qs[h], _head(k_ref, h, _key_rows(j, tk)))
            for r0 in range(0, tq, rc):
                rr = slice(r0, r0 + rc)
                zc = z[rr]
                ls = jnp.minimum(zc, 0.0) - jnp.log2(1.0 + jnp.exp2(-jnp.abs(zc)))
                lk = ls - zc
                if diagonal:
                    row = r0 + lax.broadcasted_iota(jnp.int32, (rc, 1), 0)
                    col = lax.broadcasted_iota(jnp.int32, (1, tk), 1)
                    past = col < row
                    lk = jnp.where(past, lk, 0.0)
                    ls = jnp.where(past, ls, NEG_INF)
                hi = lk.astype(BF16)
                hl_ref[h * tq + r0:h * tq + r0 + rc, :tk] = hi
                hl_ref[h * tq + r0:h * tq + r0 + rc, tk:] = (lk - hi.astype(F32)).astype(BF16)
                ls_ref[h, rr, :] = ls
                tot = jnp.sum(_lane_fold(lk, jnp.add), axis=1, keepdims=True)
                tot_ref[h, rr, :] = jnp.broadcast_to(tot, (rc, LANES))

    def weigh(bufs, j):
        ls_ref, hl_ref, tot_ref = bufs
        suf_ref[...] = _dot(hl_ref[...], u2)
        for h in heads:
            for r0 in range(0, tq, rc):
                rr = slice(r0, r0 + rc)
                c = c_ref[h, rr, :]
                suf = suf_ref[h * tq + r0:h * tq + r0 + rc, :]
                pre = ls_ref[h, rr, :] + suf + jnp.concatenate([c] * reps, axis=1)
                e_ref[h, rr, :] = jnp.exp2(pre).astype(BF16)
                c_ref[h, rr, :] = c + tot_ref[h, rr, :]
            acc_ref[h] += _dot(e_ref[h], _head(v_ref, h, _key_rows(j, tk)))

    even = (ls_e, hl_e, tot_e)
    odd = (ls_o, hl_o, tot_o)
    prepare(even, i, True)

    def pair(p, carry):
        j = i - 2 * p
        prepare(odd, j - 1, False)
        weigh(even, j)
        prepare(even, j - 2, False)
        weigh(odd, j - 1)
        return carry

    n_pair = i // 2
    lax.fori_loop(0, n_pair, pair, 0)
    j = i - 2 * n_pair
    one_more = j > 0

    @pl.when(one_more)
    def _():
        prepare(odd, j - 1, False)

    weigh(even, j)

    @pl.when(one_more)
    def _():
        weigh(odd, j - 1)

    for h in heads:
        o_ref[:, h * HEAD_DIM:(h + 1) * HEAD_DIM] = acc_ref[h].astype(BF16)


def _sb_attention(proj, tq=256, tk=256, hp=4, rc=16):
    b, s, _ = proj.shape
    w = hp * HEAD_DIM
    wide_f32 = pltpu.VMEM((hp, tq, tk), F32)
    wide_bf16 = pltpu.VMEM((hp, tq, tk), BF16)
    stat = pltpu.VMEM((hp, tq, LANES), F32)
    return pl.pallas_call(
        functools.partial(_sb_kernel, tq=tq, tk=tk, hp=hp, rc=rc),
        grid=(b, N_HEADS // hp, s // tq),
        in_specs=[pl.BlockSpec((None, tq, w), lambda bi, hg, i: (bi, i, COL_BQ // hp + hg)),
                  pl.BlockSpec((None, s, w), lambda bi, hg, i: (bi, 0, COL_BK // hp + hg)),
                  pl.BlockSpec((None, s, w), lambda bi, hg, i: (bi, 0, COL_BV // hp + hg))],
        out_specs=pl.BlockSpec((None, tq, w), lambda bi, hg, i: (bi, i, hg)),
        out_shape=jax.ShapeDtypeStruct((b, s, GROUP), BF16),
        scratch_shapes=[wide_f32, wide_f32,
                        pltpu.VMEM((hp * tq, 2 * tk), BF16), pltpu.VMEM((hp * tq, 2 * tk), BF16),
                        stat, stat, pltpu.VMEM((hp * tq, tk), F32), wide_bf16, stat, stat],
        compiler_params=_params(3),
        name="sb_attention",
    )(proj, proj, proj)


def _fox_cum_kernel(cf_ref, bias_ref, col_ref, row_ref, *, chunk):
    s = cf_ref.shape[0]
    lf = jax.nn.log_sigmoid(cf_ref[...] + bias_ref[...]) * LOG2E
    rr = lax.broadcasted_iota(jnp.int32, (chunk, chunk), 0)
    cc = lax.broadcasted_iota(jnp.int32, (chunk, chunk), 1)
    tri = jnp.where(rr >= cc, 1.0, 0.0).astype(BF16)
    carry = jnp.zeros((1, LANES), F32)
    for c in range(s // chunk):
        pre = _dot3_left(tri, lf[c * chunk:(c + 1) * chunk]) + carry
        col_ref[c * chunk:(c + 1) * chunk, :] = pre
        carry = pre[chunk - 1:chunk, :]
    row_ref[...] = jnp.transpose(col_ref[...])[:8, :]


def _fox_cum(aux, bias_row, chunk=256):
    b, s, _ = aux.shape
    return pl.pallas_call(
        functools.partial(_fox_cum_kernel, chunk=chunk),
        grid=(b,),
        in_specs=[pl.BlockSpec((None, s, LANES), lambda bi: (bi, 0, AUX_F)),
                  pl.BlockSpec((1, LANES), lambda bi: (0, 0))],
        out_specs=[pl.BlockSpec((None, s, LANES), lambda bi: (bi, 0, 0)),
                   pl.BlockSpec((None, 8, s), lambda bi: (bi, 0, 0))],
        out_shape=[jax.ShapeDtypeStruct((b, s, LANES), F32),
                   jax.ShapeDtypeStruct((b, 8, s), F32)],
        compiler_params=_params(1),
        name="fox_cum",
    )(aux, bias_row)


def _fox_kernel(q_ref, k_ref, v_ref, cq_ref, ck_ref, o_ref, *bufs, tq, tk, hp):
    assert tk >= tq
    hg = pl.program_id(1)
    i = pl.program_id(2)
    lane = lax.broadcasted_iota(jnp.int32, (1, LANES), 1)
    cum_q = cq_ref[...]
    qs = [_head(q_ref, h) for h in range(hp)]
    cqs = [jnp.sum(jnp.where(lane == hg * hp + h, cum_q, 0.0), axis=1, keepdims=True)
           for h in range(hp)]

    def score(h, j):
        return (_dot_nt(qs[h], _head(k_ref, h, _key_rows(j, tk))) + cqs[h]
                - ck_ref[h, pl.ds(j, 1), :])

    def value(h, j):
        return _head(v_ref, h, _key_rows(j, tk))

    def causal(j, s, r0):
        row = i * tq + r0 + lax.broadcasted_iota(jnp.int32, (s.shape[0], 1), 0)
        col = j * tk + lax.broadcasted_iota(jnp.int32, (1, tk), 1)
        return jnp.where(col <= row, s, NEG_INF)

    outs = _flash_sweep_ref(0, (i * tq) // tk, hp, score, value, causal, tq, tk, bufs)
    for h in range(hp):
        o_ref[:, h * HEAD_DIM:(h + 1) * HEAD_DIM] = outs[h].astype(BF16)


def _fox_attention(proj, cum_col, cum_row, tq=512, tk=512, hp=4):
    b, s, _ = proj.shape
    w = hp * HEAD_DIM
    ck = cum_row.reshape(b, 8, s // tk, tk)
    return pl.pallas_call(
        functools.partial(_fox_kernel, tq=tq, tk=tk, hp=hp),
        grid=(b, N_HEADS // hp, s // tq),
        in_specs=[pl.BlockSpec((None, tq, w), lambda bi, hg, i: (bi, i, COL_CQ // hp + hg)),
                  pl.BlockSpec((None, s, w), lambda bi, hg, i: (bi, 0, COL_CK // hp + hg)),
                  pl.BlockSpec((None, s, w), lambda bi, hg, i: (bi, 0, COL_CV // hp + hg)),
                  pl.BlockSpec((None, tq, LANES), lambda bi, hg, i: (bi, i, 0)),
                  pl.BlockSpec((None, hp, s // tk, tk), lambda bi, hg, i: (bi, hg, 0, 0))],
        out_specs=pl.BlockSpec((None, tq, w), lambda bi, hg, i: (bi, i, hg)),
        out_shape=jax.ShapeDtypeStruct((b, s, GROUP), BF16),
        scratch_shapes=_flash_scratch(hp, tq, tk),
        compiler_params=_params(3),
        name="fox_attention",
    )(proj, proj, proj, cum_col, ck)


def _diff_kernel(q_ref, k_ref, v_ref, dl_ref, gn_ref, o_ref, *bufs, tq, tk, hp):
    assert tk >= tq
    i = pl.program_id(2)
    lane = lax.broadcasted_iota(jnp.int32, (1, LANES), 1)
    zero = jnp.zeros((), BF16)
    qs = []
    for h in range(hp):
        qh = _head(q_ref, h)
        qs.append(jnp.concatenate([jnp.where(lane < DIFF_QK, qh, zero),
                                   jnp.where(lane >= DIFF_QK, qh, zero)], axis=0))

    def score(h, j):
        return _dot_nt(qs[h], _head(k_ref, h, _key_rows(j, tk)))

    def value(h, j):
        return _head(v_ref, h, _key_rows(j, tk))

    def causal(j, s, r0):
        row = i * tq + r0 % tq + lax.broadcasted_iota(jnp.int32, (s.shape[0], 1), 0)
        col = j * tk + lax.broadcasted_iota(jnp.int32, (1, tk), 1)
        return jnp.where(col <= row, s, NEG_INF)

    outs = _flash_sweep_ref(0, (i * tq) // tk, hp, score, value, causal, 2 * tq, tk, bufs,
                            rc=16)
    dl = dl_ref[...]
    lam = (jnp.exp(jnp.sum(dl[0:1] * dl[1:2], axis=1, keepdims=True))
           - jnp.exp(jnp.sum(dl[2:3] * dl[3:4], axis=1, keepdims=True)) + dl[4:5, 0:1])
    for h in range(hp):
        od = outs[h][:tq] - lam * outs[h][tq:]
        o_ref[:, h * HEAD_DIM:(h + 1) * HEAD_DIM] = (
            _rms(od, gn_ref[...]) * dl[5:6, 0:1]).astype(BF16)


def _diff_attention(proj, dl, gn, tq=256, tk=512, hp=4):
    b, s, _ = proj.shape
    w = hp * HEAD_DIM
    return pl.pallas_call(
        functools.partial(_diff_kernel, tq=tq, tk=tk, hp=hp),
        grid=(b, N_HEADS // hp, s // tq),
        in_specs=[pl.BlockSpec((None, tq, w), lambda bi, hg, i: (bi, i, COL_DQ // hp + hg)),
                  pl.BlockSpec((None, s, w), lambda bi, hg, i: (bi, 0, COL_DK // hp + hg)),
                  pl.BlockSpec((None, s, w), lambda bi, hg, i: (bi, 0, COL_DV // hp + hg)),
                  pl.BlockSpec((8, LANES), lambda bi, hg, i: (0, 0)),
                  pl.BlockSpec((1, LANES), lambda bi, hg, i: (0, 0))],
        out_specs=pl.BlockSpec((None, tq, w), lambda bi, hg, i: (bi, i, hg)),
        out_shape=jax.ShapeDtypeStruct((b, s, GROUP), BF16),
        scratch_shapes=_flash_scratch(hp, 2 * tq, tk),
        compiler_params=_params(3),
        name="diff_attention",
    )(proj, proj, proj, dl, gn)


def _cmp_kernel(k2_ref, v2_ref, pek_ref, pev_ref, wk1_ref, wk2_ref, wv1_ref, wv2_ref,
                kc_ref, vc_ref):
    nb = k2_ref.shape[0]
    w = NSA_CMP_STRIDE * HEAD_DIM

    def compress(x2, pe, w1_ref, w2_ref):
        a = _dot((x2 + pe[:, :w]).astype(BF16), w1_ref[:w, :])
        bb = _dot((x2 + pe[:, w:]).astype(BF16), w1_ref[w:, :])
        hid = jax.nn.gelu(a + pltpu.roll(bb, nb - 1, 0), approximate=True)
        return _dot(hid.astype(BF16), w2_ref[...])

    kc_ref[...] = compress(k2_ref[...], pek_ref[...], wk1_ref, wk2_ref).astype(BF16)
    vc_ref[...] = compress(v2_ref[...], pev_ref[...], wv1_ref, wv2_ref).astype(BF16)


def _nsa_compress(k2, v2, pek, pev, wk1, wk2, wv1, wv2):
    b, nb, w = k2.shape
    hid = wk1.shape[1]
    full = lambda *shape: pl.BlockSpec(shape, lambda bi: (0,) * len(shape))
    return pl.pallas_call(
        _cmp_kernel,
        grid=(b,),
        in_specs=[pl.BlockSpec((None, nb, w), lambda bi: (bi, 0, 0)),
                  pl.BlockSpec((None, nb, w), lambda bi: (bi, 0, 0)),
                  full(1, 2 * w), full(1, 2 * w),
                  full(2 * w, hid), full(hid, HEAD_DIM), full(2 * w, hid), full(hid, HEAD_DIM)],
        out_specs=[pl.BlockSpec((None, nb, HEAD_DIM), lambda bi: (bi, 0, 0)),
                   pl.BlockSpec((None, nb, HEAD_DIM), lambda bi: (bi, 0, 0))],
        out_shape=[jax.ShapeDtypeStruct((b, nb, HEAD_DIM), BF16),
                   jax.ShapeDtypeStruct((b, nb, HEAD_DIM), BF16)],
        compiler_params=_params(1),
        name="nsa_compress",
    )(k2, v2, pek, pev, wk1, wk2, wv1, wv2)


def _nsa_kernel(q_ref, qr_ref, g_ref, kc_ref, vc_ref, ks_ref, kw_ref, vs_ref, vw_ref, o_ref,
                *bufs, tq, tk, n_top):
    assert tk >= tq
    i = pl.program_id(1)
    nb = kc_ref.shape[0]
    n_sel = NSA_SLC_BLOCK
    hq = N_HEADS * tq

    q_raw = jnp.concatenate([_head(q_ref, h) for h in range(N_HEADS)], axis=0)
    q_rot = jnp.concatenate([_head(qr_ref, h) for h in range(N_HEADS)], axis=0)
    row = i * tq + lax.broadcasted_iota(jnp.int32, (tq, 1), 0)
    rep = lambda a: jnp.concatenate([a] * N_HEADS, axis=0)
    n_full = (i * tq) // tk

    def causal(j, s, r0):
        rows = i * tq + r0 % tq + lax.broadcasted_iota(jnp.int32, (s.shape[0], 1), 0)
        col = j * tk + lax.broadcasted_iota(jnp.int32, (1, tk), 1)
        return jnp.where(col <= rows, s, NEG_INF)

    cidx = lax.broadcasted_iota(jnp.int32, (1, nb), 1)
    c_valid = rep((cidx * NSA_CMP_STRIDE + (NSA_CMP_BLOCK - 1)) <= row)
    sc = jnp.where(c_valid, _dot_nt(q_raw, kc_ref[...]), NEG_INF)
    e = jnp.exp2(sc - jnp.max(sc, axis=1, keepdims=True))
    p = jnp.where(c_valid, e * (1.0 / jnp.sum(e, axis=1, keepdims=True)), 0.0)
    o_cmp = _dot(p.astype(BF16), vc_ref[...])
    p_sum = p[0:tq]
    for h in range(1, N_HEADS):
        p_sum = p_sum + p[h * tq:(h + 1) * tq]
    oc = lax.broadcasted_iota(jnp.int32, (nb, LANES), 0)
    oj = lax.broadcasted_iota(jnp.int32, (nb, LANES), 1)
    per = NSA_SLC_BLOCK // NSA_CMP_STRIDE
    overlap = (jnp.where(oc // per == oj, 1.0, 0.0)
               + jnp.where((oc + 1) // per == oj, 1.0, 0.0)).astype(BF16)
    imp = _dot3(p_sum, overlap)

    imp_t = jnp.transpose(imp)[:n_sel, :]
    jb = lax.broadcasted_iota(jnp.int32, (n_sel, 1), 0)
    cur = (i * tq + lax.broadcasted_iota(jnp.int32, (1, tq), 1)) // NSA_SLC_BLOCK
    sel_valid = jb <= cur
    forced = sel_valid & ((jb == 0) | (jb >= cur - 1))
    score = jnp.where(forced, FORCE_SCORE, jnp.where(sel_valid, imp_t, NEG_INF))
    cnt = jnp.zeros((n_sel, tq), F32)
    for ii in range(n_sel):
        ri = score[ii:ii + 1, :]
        beats = (ri > score) | ((ri == score) & (jb > ii))
        cnt = cnt + jnp.where(beats, 1.0, 0.0)
    sel_t = jnp.where((cnt < n_top) & sel_valid, 1.0, 0.0)
    sel = jnp.transpose(jnp.concatenate([sel_t, jnp.zeros_like(sel_t)], axis=0)).astype(BF16)

    er = lax.broadcasted_iota(jnp.int32, (LANES, tk), 0)
    ec = lax.broadcasted_iota(jnp.int32, (LANES, tk), 1)

    def slc_score(_, j):
        expand = jnp.where(er == (j * tk + ec) // NSA_SLC_BLOCK, 1.0, 0.0).astype(BF16)
        ok = _dot(sel, expand) > 0.5
        return jnp.where(rep(ok), _dot_nt(q_rot, ks_ref[_key_rows(j, tk), :]), NEG_INF)

    o_slc, = _flash_sweep_ref(0, n_full, 1, slc_score,
                              lambda _, j: vs_ref[_key_rows(j, tk), :], causal, hq, tk, bufs,
                              rc=16)

    def win_score(_, j):
        col = j * tk + lax.broadcasted_iota(jnp.int32, (1, tk), 1)
        return jnp.where(rep(col > row - NSA_WINDOW),
                         _dot_nt(q_rot, kw_ref[_key_rows(j, tk), :]), NEG_INF)

    j_lo = (jnp.maximum(i * tq - NSA_WINDOW, 0) // (2 * tk)) * 2
    o_win, = _flash_sweep_ref(j_lo, n_full, 1, win_score,
                              lambda _, j: vw_ref[_key_rows(j, tk), :], causal, hq, tk, bufs,
                              rc=16)

    g = jax.nn.sigmoid(g_ref[...])
    for h in range(N_HEADS):
        rows = slice(h * tq, (h + 1) * tq)
        o_ref[:, h * HEAD_DIM:(h + 1) * HEAD_DIM] = (
            g[:, 3 * h:3 * h + 1] * o_cmp[rows] + g[:, 3 * h + 1:3 * h + 2] * o_slc[rows]
            + g[:, 3 * h + 2:3 * h + 3] * o_win[rows]).astype(BF16)


def _nsa_attention(proj, aux, kc, vc, tq=256, tk=256):
    b, s, _ = proj.shape
    nb = kc.shape[1]
    n_top = min(NSA_TOPK, s // NSA_SLC_BLOCK)
    kv = lambda col: pl.BlockSpec((None, s, HEAD_DIM), lambda bi, i: (bi, 0, col))
    return pl.pallas_call(
        functools.partial(_nsa_kernel, tq=tq, tk=tk, n_top=n_top),
        grid=(b, s // tq),
        in_specs=[pl.BlockSpec((None, tq, GROUP), lambda bi, i: (bi, i, COL_AQ // N_HEADS)),
                  pl.BlockSpec((None, tq, GROUP), lambda bi, i: (bi, i, COL_AQR // N_HEADS)),
                  pl.BlockSpec((None, tq, LANES), lambda bi, i: (bi, i, AUX_G)),
                  pl.BlockSpec((None, nb, HEAD_DIM), lambda bi, i: (bi, 0, 0)),
                  pl.BlockSpec((None, nb, HEAD_DIM), lambda bi, i: (bi, 0, 0)),
                  kv(COL_AKS), kv(COL_AKW), kv(COL_AVS), kv(COL_AVW)],
        out_specs=pl.BlockSpec((None, tq, GROUP), lambda bi, i: (bi, i, 0)),
        out_shape=jax.ShapeDtypeStruct((b, s, GROUP), BF16),
        scratch_shapes=_flash_scratch(1, N_HEADS * tq, tk),
        compiler_params=_params(2),
        name="nsa_attention",
    )(proj, proj, aux, kc, vc, proj, proj, proj, proj)


def _out_kernel(oa_ref, ob_ref, oc_ref, od_ref, w_ref, x_ref, g_ref, o_ref):
    acc = _dot(oa_ref[...], w_ref[0:GROUP, :])
    acc = acc + _dot(ob_ref[...], w_ref[GROUP:2 * GROUP, :])
    acc = acc + _dot(oc_ref[...], w_ref[2 * GROUP:3 * GROUP, :])
    acc = acc + _dot(od_ref[...], w_ref[3 * GROUP:4 * GROUP, :])
    o_ref[...] = x_ref[...] + _rms(acc, g_ref[...])


def _out_proj(oa, ob, oc, od, w, layer, x2d, g, tm=512):
    t, d = x2d.shape
    mix = pl.BlockSpec((tm, GROUP), lambda i: (i, 0))
    return pl.pallas_call(
        _out_kernel,
        grid=(t // tm,),
        in_specs=[mix, mix, mix, mix,
                  pl.BlockSpec((None, 4 * GROUP, d), lambda i: (layer, 0, 0)),
                  pl.BlockSpec((tm, d), lambda i: (i, 0)),
                  pl.BlockSpec((1, d), lambda i: (0, 0))],
        out_specs=pl.BlockSpec((tm, d), lambda i: (i, 0)),
        out_shape=jax.ShapeDtypeStruct((t, d), F32),
        compiler_params=_params(1),
        name="out_proj",
    )(oa, ob, oc, od, w, x2d, g)


HALO = 16


def _mlp_kernel(x_ref, xh_ref, gpre_ref, wg_ref, wu_ref, cwg_ref, cwu_ref, cbg_ref, cbu_ref,
                wd_ref, gpost_ref, o_ref, xn_ref, acc_ref, *, tm, tiles_per_seq):
    i = pl.program_id(0)
    f = pl.program_id(1)

    @pl.when(f == 0)
    def _():
        xn_ref[HALO:, :] = _rms(x_ref[...], gpre_ref[...]).astype(BF16)
        halo = _rms(xh_ref[...], gpre_ref[...])
        seq_start = (i % tiles_per_seq) == 0
        xn_ref[:HALO, :] = jnp.where(seq_start, 0.0, halo).astype(BF16)
        acc_ref[...] = jnp.zeros_like(acc_ref)

    xn = xn_ref[...]

    def conv(u, cw_ref, cb_ref):
        cw = cw_ref[...]
        out = cb_ref[...] + cw[0:1] * u[HALO - 2:HALO - 2 + tm]
        out = out + cw[1:2] * u[HALO - 1:HALO - 1 + tm]
        return out + cw[2:3] * u[HALO:HALO + tm]

    gate = conv(_dot(xn, wg_ref[...]), cwg_ref, cbg_ref)
    up = conv(_dot(xn, wu_ref[...]), cwu_ref, cbu_ref)
    hid = (jax.nn.gelu(gate, approximate=True) * up).astype(BF16)
    acc_ref[...] += _dot(hid, wd_ref[...])

    @pl.when(f == pl.num_programs(1) - 1)
    def _():
        o_ref[...] = x_ref[...] + _rms(acc_ref[...], gpost_ref[...])


def _mlp(x2d, gpre, w_up, conv_w, conv_b, w_down, layer, gpost, seq, tm=1024, tf=512):
    t, d = x2d.shape
    ff = w_down.shape[1]
    nf = ff // tf
    hb = tm // HALO
    once = pl.Buffered(1)
    return pl.pallas_call(
        functools.partial(_mlp_kernel, tm=tm, tiles_per_seq=seq // tm),
        grid=(t // tm, nf),
        in_specs=[pl.BlockSpec((tm, d), lambda i, f: (i, 0), pipeline_mode=once),
                  pl.BlockSpec((HALO, d), lambda i, f: (jnp.maximum(i * hb - 1, 0), 0)),
                  pl.BlockSpec((1, d), lambda i, f: (0, 0)),
                  pl.BlockSpec((None, d, tf), lambda i, f: (layer, 0, f)),
                  pl.BlockSpec((None, d, tf), lambda i, f: (layer, 0, nf + f)),
                  pl.BlockSpec((3, tf), lambda i, f: (0, f)),
                  pl.BlockSpec((3, tf), lambda i, f: (0, nf + f)),
                  pl.BlockSpec((1, tf), lambda i, f: (0, f)),
                  pl.BlockSpec((1, tf), lambda i, f: (0, nf + f)),
                  pl.BlockSpec((None, tf, d), lambda i, f: (layer, f, 0)),
                  pl.BlockSpec((1, d), lambda i, f: (0, 0))],
        out_specs=pl.BlockSpec((tm, d), lambda i, f: (i, 0), pipeline_mode=once),
        out_shape=jax.ShapeDtypeStruct((t, d), F32),
        scratch_shapes=[pltpu.VMEM((tm + HALO, d), BF16), pltpu.VMEM((tm, d), F32)],
        compiler_params=_params(2),
        name="conv_geglu_mlp",
    )(x2d, x2d, gpre, w_up, w_up, conv_w, conv_w, conv_b, conv_b, w_down, gpost)


_IN_OFF = np.concatenate([[0], np.cumsum(_IN_SPLITS)]).tolist()
_HEAD_RUNS = ((0, 0), (0, 0), (3, 6), (8, 13), (15, 17))


def _repack_kernel(w_ref, heads_ref, aux_ref):
    def run(first, last):
        return w_ref[:, _IN_OFF[first]:_IN_OFF[last + 1]].astype(BF16)

    at = 0
    for first, last in _HEAD_RUNS:
        width = _IN_OFF[last + 1] - _IN_OFF[first]
        heads_ref[:, at:at + width] = run(first, last)
        at += width
    aux_ref[...] = jnp.zeros(aux_ref.shape, BF16)
    aux_ref[:, 0:2 * LANES] = run(1, 2)
    aux_ref[:, AUX_G * LANES:AUX_G * LANES + _IN_SPLITS[7]] = run(7, 7)
    aux_ref[:, AUX_F * LANES:AUX_F * LANES + _IN_SPLITS[14]] = run(14, 14)


def _split_w_in(w, rows=256):
    w = jnp.pad(w, ((0, 0), (0, 0), (0, -w.shape[2] % LANES))).astype(BF16)
    depth, d, n_in = w.shape
    n_heads, n_aux = HEAD_BLOCKS * LANES, AUX_BLOCKS * LANES
    return pl.pallas_call(
        _repack_kernel,
        grid=(depth, d // rows),
        in_specs=[pl.BlockSpec((None, rows, n_in), lambda l, r: (l, r, 0))],
        out_specs=[pl.BlockSpec((None, rows, n_heads), lambda l, r: (l, r, 0)),
                   pl.BlockSpec((None, rows, n_aux), lambda l, r: (l, r, 0))],
        out_shape=[jax.ShapeDtypeStruct((depth, d, n_heads), BF16),
                   jax.ShapeDtypeStruct((depth, d, n_aux), BF16)],
        compiler_params=_params(2),
        name="repack_w_in",
    )(w)


def _rope_lane_tables(seq, rot_dim, period):
    pos = jnp.arange(seq, dtype=F32)
    inv = ROPE_THETA ** (-jnp.arange(0, rot_dim, 2, dtype=F32) / rot_dim)
    ang = pos[:, None] * inv[None, :]
    cos, sin = jnp.cos(ang), jnp.sin(ang)
    rest = period - rot_dim
    c = jnp.concatenate([cos, cos, jnp.ones((seq, rest), F32)], axis=-1)
    s = jnp.concatenate([-sin, sin, jnp.zeros((seq, rest), F32)], axis=-1)
    reps = LANES // period
    return jnp.tile(c, (1, reps)), jnp.tile(s, (1, reps))


@jax.jit
def _forward(x, norm_mix_pre, norm_mix_post, norm_mlp_pre, norm_mlp_post, w_in, w_out,
             nsa_pe_k, nsa_pe_v, nsa_cmp_k1, nsa_cmp_k2, nsa_cmp_v1, nsa_cmp_v2,
             fox_forget_bias, diff_lambda, diff_norm, mlp_w_up, mlp_conv_w, mlp_conv_b,
             mlp_w_down):
    b, s, d = x.shape
    depth = w_in.shape[0]
    cos_h, sin_h = _rope_lane_tables(s, HEAD_DIM // 4, LANES)
    cos_d, sin_d = _rope_lane_tables(s, DIFF_QK // 4, DIFF_QK)
    cos_sets = jnp.stack([cos_h * SCALE_HEAD, cos_h, cos_d * SCALE_DIFF, cos_d])
    sin_sets = jnp.stack([sin_h * SCALE_HEAD, sin_h, sin_d * SCALE_DIFF, sin_d])
    w_heads, w_aux = _split_w_in(w_in)
    w_out_b = w_out.astype(BF16)
    w_up_b = mlp_w_up.astype(BF16)
    w_down_b = mlp_w_down.astype(BF16)
    k1_b, k2_b = nsa_cmp_k1.astype(BF16), nsa_cmp_k2.astype(BF16)
    v1_b, v2_b = nsa_cmp_v1.astype(BF16), nsa_cmp_v2.astype(BF16)
    fox_bias = jnp.pad(fox_forget_bias, ((0, 0), (0, LANES - N_HEADS)))
    row2 = lambda a: a.reshape(1, -1)
    cmp_w = NSA_CMP_STRIDE * HEAD_DIM

    xt = x.reshape(b * s, d)
    for l in range(depth):
        g_pre = row2(norm_mix_pre[l])
        proj = _head_proj(xt, g_pre, w_heads, l, cos_sets, sin_sets, s, tm=min(1024, s))
        proj = proj.reshape(b, s, HEAD_BLOCKS * LANES)
        aux = _aux_proj(xt, g_pre, w_aux, l).reshape(b, s, AUX_BLOCKS * LANES)

        k2 = aux[:, :, AUX_KC * LANES:(AUX_KC + 1) * LANES].reshape(b, s // NSA_CMP_STRIDE, cmp_w)
        v2 = aux[:, :, AUX_VC * LANES:(AUX_VC + 1) * LANES].reshape(b, s // NSA_CMP_STRIDE, cmp_w)
        kc, vc = _nsa_compress(k2, v2, row2(nsa_pe_k[l]), row2(nsa_pe_v[l]),
                               k1_b[l], k2_b[l], v1_b[l], v2_b[l])
        o_a = _nsa_attention(proj, aux, kc, vc)

        o_b = _sb_attention(proj)

        cum_col, cum_row = _fox_cum(aux, fox_bias[l:l + 1])
        o_c = _fox_attention(proj, cum_col, cum_row)

        lam_init = 0.8 - 0.6 * math.exp(-0.3 * l)
        dl = jnp.concatenate([
            jnp.pad(diff_lambda[l], ((0, 0), (0, LANES - DIFF_QK))),
            jnp.full((1, LANES), lam_init, F32),
            jnp.full((1, LANES), 1.0 - lam_init, F32),
            jnp.zeros((2, LANES), F32)], axis=0)
        o_d = _diff_attention(proj, dl, row2(diff_norm[l]))

        flat = lambda o: o.reshape(b * s, GROUP)
        xt = _out_proj(flat(o_a), flat(o_b), flat(o_c), flat(o_d), w_out_b, l, xt,
                       row2(norm_mix_post[l]))
        xt = _mlp(xt, row2(norm_mlp_pre[l]), w_up_b, mlp_conv_w[l], row2(mlp_conv_b[l]),
                  w_down_b, l, row2(norm_mlp_post[l]), s, tm=min(1024, s))
    return xt.reshape(b, s, d)


def kernel(x, norm_mix_pre, norm_mix_post, norm_mlp_pre, norm_mlp_post, w_in, w_out, nsa_pe_k, nsa_pe_v, nsa_cmp_k1, nsa_cmp_k2, nsa_cmp_v1, nsa_cmp_v2, fox_forget_bias, diff_lambda, diff_norm, mlp_w_up, mlp_conv_w, mlp_conv_b, mlp_w_down):
    return _forward(x, norm_mix_pre, norm_mix_post, norm_mlp_pre, norm_mlp_post, w_in, w_out,
                    nsa_pe_k, nsa_pe_v, nsa_cmp_k1, nsa_cmp_k2, nsa_cmp_v1, nsa_cmp_v2,
                    fox_forget_bias, diff_lambda, diff_norm, mlp_w_up, mlp_conv_w, mlp_conv_b,
                    mlp_w_down)
```

```python
import functools
import math

import jax
import jax.numpy as jnp
import numpy as np
from jax import lax
from jax.experimental import pallas as pl
from jax.experimental.pallas import tpu as pltpu

F32 = jnp.float32
BF16 = jnp.bfloat16

LANES = 128
HEAD_DIM = 128
N_HEADS = 4
GROUP = N_HEADS * HEAD_DIM
NSA_CMP_BLOCK = 32
NSA_CMP_STRIDE = 16
NSA_SLC_BLOCK = 64
NSA_TOPK = 16
NSA_WINDOW = 512
DIFF_QK = HEAD_DIM // 2
ROPE_HALF_HEAD = HEAD_DIM // 4 // 2
ROPE_HALF_DIFF = DIFF_QK // 4 // 2
EPS = 1e-6
NEG_INF = -1e30
FORCE_SCORE = 1e9
ROPE_THETA = 500000.0
LOG2E = math.log2(math.e)
SCALE_HEAD = HEAD_DIM ** -0.5 * LOG2E
SCALE_DIFF = DIFF_QK ** -0.5 * LOG2E

_IN_SPLITS = (GROUP, 128, 128, 128, 128, 128, 128, 3 * N_HEADS,
              GROUP, GROUP, GROUP, GROUP, GROUP, GROUP, N_HEADS, GROUP, GROUP, GROUP)
COL_AQ, COL_AQR, COL_AKS, COL_AVS, COL_AKW, COL_AVW = 0, 4, 8, 9, 10, 11
COL_BQ, COL_BK, COL_BV = 12, 16, 20
COL_CQ, COL_CK, COL_CV = 24, 28, 32
COL_DQ, COL_DK, COL_DV = 36, 40, 44
HEAD_BLOCKS = 48
AUX_KC, AUX_VC, AUX_G, AUX_F = 0, 1, 2, 3
AUX_BLOCKS = 4

PROJ_TN = GROUP
_TILES_SCALE = (0, 3, 6)
_TILE_ROPE_Q = 1
_TILE_ROPE_KV = 2
_TILES_ROPE_DIFF = (9, 10)
_ROPE_SET_STARTS = (2, 9, 10)

VMEM_LIMIT = 56 * 1024 * 1024


def _params(n_axes, vmem=VMEM_LIMIT):
    return pltpu.CompilerParams(dimension_semantics=("arbitrary",) * n_axes,
                                vmem_limit_bytes=vmem)


def _rms(x, g):
    ms = jnp.mean(x * x, axis=-1, keepdims=True)
    return x * lax.rsqrt(ms + EPS) * g


def _dot(a, b):
    return jnp.dot(a, b, preferred_element_type=F32)


def _dot_nt(a, b):
    return lax.dot_general(a, b, (((1,), (1,)), ((), ())), preferred_element_type=F32)


def _split3(x):
    hi = x.astype(BF16)
    r = x - hi.astype(F32)
    mid = r.astype(BF16)
    lo = (r - mid.astype(F32)).astype(BF16)
    return hi, mid, lo


def _dot3(x, m):
    hi, mid, lo = _split3(x)
    return _dot(hi, m) + _dot(mid, m) + _dot(lo, m)


def _dot3_left(m, x):
    hi, mid, lo = _split3(x)
    return _dot(m, hi) + _dot(m, mid) + _dot(m, lo)


def _rope(x, c, s, half):
    width = x.shape[1]
    lane = lax.broadcasted_iota(jnp.int32, (1, width), 1)
    first = (lane % (2 * half)) < half
    xs = jnp.where(first, pltpu.roll(x, width - half, 1), pltpu.roll(x, half, 1))
    return x * c + xs * s


def _lane_fold(x, op):
    out = x[:, :LANES]
    for b in range(1, x.shape[1] // LANES):
        out = op(out, x[:, b * LANES:(b + 1) * LANES])
    return out


def _row_max(s):
    return jnp.max(_lane_fold(s, jnp.maximum), axis=1, keepdims=True)


def _flash_scratch(n_chain, rows, tk):
    stat = pltpu.VMEM((n_chain, rows, LANES), F32)
    return [pltpu.VMEM((n_chain, rows, tk), F32), pltpu.VMEM((n_chain, rows, tk), F32),
            pltpu.VMEM((n_chain, rows, tk), BF16), stat, stat,
            pltpu.VMEM((n_chain, rows, 2 * LANES), F32)]


def _flash_sweep_ref(lo, n_full, n_chain, score_fn, value_fn, mask_fn, rows, tk, bufs, rc=16):
    s_even, s_odd, p_buf, m_ref, a_ref, acc_ref = bufs
    chains = range(n_chain)
    reps = tk // LANES
    ones = jnp.ones((tk, LANES), BF16)
    m_ref[...] = jnp.full(m_ref.shape, NEG_INF, F32)
    acc_ref[...] = jnp.zeros(acc_ref.shape, F32)

    def scores(s_buf, j):
        for c in chains:
            s_buf[c] = score_fn(c, j)

    def consume(s_buf, j, masked):
        for c in chains:
            for r0 in range(0, rows, rc):
                rr = slice(r0, r0 + rc)
                s = s_buf[c, rr, :]
                if masked:
                    s = mask_fn(j, s, r0)
                m_old = m_ref[c, rr, :]
                m_new = jnp.maximum(m_old, _row_max(s))
                m_ref[c, rr, :] = m_new
                a_ref[c, rr, :] = jnp.exp2(m_old - m_new)
                shifted = s - jnp.concatenate([m_new] * reps, axis=1)
                p_buf[c, rr, :] = jnp.exp2(shifted.astype(BF16))
            v_ones = jnp.concatenate([value_fn(c, j), ones], axis=1)
            alpha = a_ref[c]
            acc_ref[c] = (jnp.concatenate([alpha, alpha], axis=1) * acc_ref[c]
                          + _dot(p_buf[c], v_ones))

    scores(s_even, lo)

    def pair(t, carry):
        j = lo + 2 * t
        scores(s_odd, j + 1)
        consume(s_even, j, False)
        scores(s_even, j + 2)
        consume(s_odd, j + 1, False)
        return carry

    n_pair = (n_full - lo) // 2
    lax.fori_loop(0, n_pair, pair, 0)
    j = lo + 2 * n_pair
    last_full = j < n_full

    @pl.when(last_full)
    def _():
        scores(s_odd, j + 1)

    consume(s_even, j, True)

    @pl.when(last_full)
    def _():
        consume(s_odd, j + 1, True)

    return [acc_ref[c, :, :LANES] * (1.0 / acc_ref[c, :, LANES:]) for c in chains]


def _head(ref, h, rows=None):
    cols = slice(h * HEAD_DIM, (h + 1) * HEAD_DIM)
    return ref[:, cols] if rows is None else ref[rows, cols]


def _key_rows(j, tk):
    return pl.ds(pl.multiple_of(j * tk, tk), tk)


def _tile_in(j, tiles):
    hit = j == tiles[0]
    for t in tiles[1:]:
        hit = hit | (j == t)
    return hit


def _head_proj_kernel(x_ref, g_ref, w_ref, c_ref, s_ref, o_ref, xn_ref, *, slabs):
    j = pl.program_id(1)

    @pl.when(j == 0)
    def _():
        xn_ref[...] = _rms(x_ref[...], g_ref[...]).astype(BF16)

    is_scale = _tile_in(j, _TILES_SCALE)
    is_rope_d = _tile_in(j, _TILES_ROPE_DIFF)
    is_rope = (j == _TILE_ROPE_Q) | (j == _TILE_ROPE_KV) | is_rope_d
    slab = o_ref.shape[0] // slabs

    def tile(epilogue):
        for r in range(slabs):
            rows = slice(r * slab, (r + 1) * slab)
            o_ref[rows, :] = epilogue(_dot(xn_ref[rows, :], w_ref[...]), rows).astype(BF16)

    def rope_tile(half, rotated):
        def epilogue(acc, rows):
            ones = jnp.ones((slab, LANES), F32)
            c = jnp.concatenate([c_ref[rows, :] if r else ones for r in rotated], axis=1)
            s = jnp.concatenate([s_ref[rows, :] if r else 0.0 * ones for r in rotated], axis=1)
            return _rope(acc, c, s, half)
        tile(epilogue)

    every = (True,) * N_HEADS
    pl.when(is_scale)(lambda: tile(lambda acc, rows: acc * SCALE_HEAD))
    pl.when(j == _TILE_ROPE_Q)(lambda: rope_tile(ROPE_HALF_HEAD, every))
    pl.when(j == _TILE_ROPE_KV)(lambda: rope_tile(ROPE_HALF_HEAD, (True, False, True, False)))
    pl.when(is_rope_d)(lambda: rope_tile(ROPE_HALF_DIFF, every))
    pl.when(jnp.logical_not(is_scale | is_rope))(lambda: tile(lambda acc, rows: acc))


def _rope_set(j):
    idx = 0
    for start in _ROPE_SET_STARTS:
        idx = idx + (j >= start).astype(jnp.int32)
    return idx


def _head_proj(x2d, g, w, layer, cos_sets, sin_sets, seq, tm=1024, slabs=4):
    t, d = x2d.shape
    n = w.shape[2]
    per_seq = seq // tm
    table = pl.BlockSpec((None, tm, LANES), lambda i, j: (_rope_set(j), i % per_seq, 0))
    return pl.pallas_call(
        functools.partial(_head_proj_kernel, slabs=slabs),
        grid=(t // tm, n // PROJ_TN),
        in_specs=[pl.BlockSpec((tm, d), lambda i, j: (i, 0)),
                  pl.BlockSpec((1, d), lambda i, j: (0, 0)),
                  pl.BlockSpec((None, d, PROJ_TN), lambda i, j: (layer, 0, j)),
                  table, table],
        out_specs=pl.BlockSpec((tm, PROJ_TN), lambda i, j: (i, j)),
        out_shape=jax.ShapeDtypeStruct((t, n), BF16),
        scratch_shapes=[pltpu.VMEM((tm, d), BF16)],
        compiler_params=_params(2),
        name="head_proj",
    )(x2d, g, w, cos_sets, sin_sets)


def _norm_matmul_kernel(x_ref, g_ref, w_ref, o_ref):
    o_ref[...] = _dot(_rms(x_ref[...], g_ref[...]).astype(BF16), w_ref[...])


def _aux_proj(x2d, g, w, layer, tm=512):
    t, d = x2d.shape
    n = w.shape[2]
    return pl.pallas_call(
        _norm_matmul_kernel,
        grid=(t // tm,),
        in_specs=[pl.BlockSpec((tm, d), lambda i: (i, 0)),
                  pl.BlockSpec((1, d), lambda i: (0, 0)),
                  pl.BlockSpec((None, d, n), lambda i: (layer, 0, 0))],
        out_specs=pl.BlockSpec((tm, n), lambda i: (i, 0)),
        out_shape=jax.ShapeDtypeStruct((t, n), F32),
        compiler_params=_params(1),
        name="aux_proj",
    )(x2d, g, w)


def _sb_kernel(q_ref, k_ref, v_ref, o_ref,
               ls_e, ls_o, hl_e, hl_o, tot_e, tot_o, suf_ref, e_ref, c_ref, acc_ref,
               *, tq, tk, hp, rc):
    assert tq == tk
    i = pl.program_id(2)
    reps = tk // LANES
    heads = range(hp)
    jr = lax.broadcasted_iota(jnp.int32, (tk, tk), 0)
    jc = lax.broadcasted_iota(jnp.int32, (tk, tk), 1)
    u = jnp.where(jr > jc, 1.0, 0.0).astype(BF16)
    u2 = jnp.concatenate([u, u], axis=0)
    qs = [_head(q_ref, h) for h in heads]
    c_ref[...] = jnp.zeros(c_ref.shape, F32)
    acc_ref[...] = jnp.zeros(acc_ref.shape, F32)

    def prepare(bufs, j, diagonal):
        ls_ref, hl_ref, tot_ref = bufs
        for h in heads:
            z = _dot_nt(qs[h], _head(k_ref, h, _key_rows(j, tk)))
            for r0 in range(0, tq, rc):
                rr = slice(r0, r0 + rc)
                zc = z[rr]
                ls = jnp.minimum(zc, 0.0) - jnp.log2(1.0 + jnp.exp2(-jnp.abs(zc)))
                lk = ls - zc
                if diagonal:
                    row = r0 + lax.broadcasted_iota(jnp.int32, (rc, 1), 0)
                    col = lax.broadcasted_iota(jnp.int32, (1, tk), 1)
                    past = col < row
                    lk = jnp.where(past, lk, 0.0)
                    ls = jnp.where(past, ls, NEG_INF)
                hi = lk.astype(BF16)
                hl_ref[h * tq + r0:h * tq + r0 + rc, :tk] = hi
                hl_ref[h * tq + r0:h * tq + r0 + rc, tk:] = (lk - hi.astype(F32)).astype(BF16)
                ls_ref[h, rr, :] = ls
                tot = jnp.sum(_lane_fold(lk, jnp.add), axis=1, keepdims=True)
                tot_ref[h, rr, :] = jnp.broadcast_to(tot, (rc, LANES))

    def weigh(bufs, j):
        ls_ref, hl_ref, tot_ref = bufs
        suf_ref[...] = _dot(hl_ref[...], u2)
        for h in heads:
            for r0 in range(0, tq, rc):
                rr = slice(r0, r0 + rc)
                c = c_ref[h, rr, :]
                suf = suf_ref[h * tq + r0:h * tq + r0 + rc, :]
                pre = ls_ref[h, rr, :] + suf + jnp.concatenate([c] * reps, axis=1)
                e_ref[h, rr, :] = jnp.exp2(pre).astype(BF16)
                c_ref[h, rr, :] = c + tot_ref[h, rr, :]
            acc_ref[h] += _dot(e_ref[h], _head(v_ref, h, _key_rows(j, tk)))

    even = (ls_e, hl_e, tot_e)
    odd = (ls_o, hl_o, tot_o)
    prepare(even, i, True)

    def pair(p, carry):
        j = i - 2 * p
        prepare(odd, j - 1, False)
        weigh(even, j)
        prepare(even, j - 2, False)
        weigh(odd, j - 1)
        return carry

    n_pair = i // 2
    lax.fori_loop(0, n_pair, pair, 0)
    j = i - 2 * n_pair
    one_more = j > 0

    @pl.when(one_more)
    def _():
        prepare(odd, j - 1, False)

    weigh(even, j)

    @pl.when(one_more)
    def _():
        weigh(odd, j - 1)

    for h in heads:
        o_ref[:, h * HEAD_DIM:(h + 1) * HEAD_DIM] = acc_ref[h].astype(BF16)


def _sb_attention(proj, tq=256, tk=256, hp=4, rc=16):
    b, s, _ = proj.shape
    w = hp * HEAD_DIM
    wide_f32 = pltpu.VMEM((hp, tq, tk), F32)
    wide_bf16 = pltpu.VMEM((hp, tq, tk), BF16)
    stat = pltpu.VMEM((hp, tq, LANES), F32)
    return pl.pallas_call(
        functools.partial(_sb_kernel, tq=tq, tk=tk, hp=hp, rc=rc),
        grid=(b, N_HEADS // hp, s // tq),
        in_specs=[pl.BlockSpec((None, tq, w), lambda bi, hg, i: (bi, i, COL_BQ // hp + hg)),
                  pl.BlockSpec((None, s, w), lambda bi, hg, i: (bi, 0, COL_BK // hp + hg)),
                  pl.BlockSpec((None, s, w), lambda bi, hg, i: (bi, 0, COL_BV // hp + hg))],
        out_specs=pl.BlockSpec((None, tq, w), lambda bi, hg, i: (bi, i, hg)),
        out_shape=jax.ShapeDtypeStruct((b, s, GROUP), BF16),
        scratch_shapes=[wide_f32, wide_f32,
                        pltpu.VMEM((hp * tq, 2 * tk), BF16), pltpu.VMEM((hp * tq, 2 * tk), BF16),
                        stat, stat, pltpu.VMEM((hp * tq, tk), F32), wide_bf16, stat, stat],
        compiler_params=_params(3),
        name="sb_attention",
    )(proj, proj, proj)


def _fox_cum_kernel(cf_ref, bias_ref, col_ref, row_ref, *, chunk):
    s = cf_ref.shape[0]
    lf = jax.nn.log_sigmoid(cf_ref[...] + bias_ref[...]) * LOG2E
    rr = lax.broadcasted_iota(jnp.int32, (chunk, chunk), 0)
    cc = lax.broadcasted_iota(jnp.int32, (chunk, chunk), 1)
    tri = jnp.where(rr >= cc, 1.0, 0.0).astype(BF16)
    carry = jnp.zeros((1, LANES), F32)
    for c in range(s // chunk):
        pre = _dot3_left(tri, lf[c * chunk:(c + 1) * chunk]) + carry
        col_ref[c * chunk:(c + 1) * chunk, :] = pre
        carry = pre[chunk - 1:chunk, :]
    row_ref[...] = jnp.transpose(col_ref[...])[:8, :]


def _fox_cum(aux, bias_row, chunk=256):
    b, s, _ = aux.shape
    return pl.pallas_call(
        functools.partial(_fox_cum_kernel, chunk=chunk),
        grid=(b,),
        in_specs=[pl.BlockSpec((None, s, LANES), lambda bi: (bi, 0, AUX_F)),
                  pl.BlockSpec((1, LANES), lambda bi: (0, 0))],
        out_specs=[pl.BlockSpec((None, s, LANES), lambda bi: (bi, 0, 0)),
                   pl.BlockSpec((None, 8, s), lambda bi: (bi, 0, 0))],
        out_shape=[jax.ShapeDtypeStruct((b, s, LANES), F32),
                   jax.ShapeDtypeStruct((b, 8, s), F32)],
        compiler_params=_params(1),
        name="fox_cum",
    )(aux, bias_row)


def _fox_kernel(q_ref, k_ref, v_ref, cq_ref, ck_ref, o_ref, *bufs, tq, tk, hp):
    assert tk >= tq
    hg = pl.program_id(1)
    i = pl.program_id(2)
    lane = lax.broadcasted_iota(jnp.int32, (1, LANES), 1)
    cum_q = cq_ref[...]
    qs = [_head(q_ref, h) for h in range(hp)]
    cqs = [jnp.sum(jnp.where(lane == hg * hp + h, cum_q, 0.0), axis=1, keepdims=True)
           for h in range(hp)]

    def score(h, j):
        return (_dot_nt(qs[h], _head(k_ref, h, _key_rows(j, tk))) + cqs[h]
                - ck_ref[h, pl.ds(j, 1), :])

    def value(h, j):
        return _head(v_ref, h, _key_rows(j, tk))

    def causal(j, s, r0):
        row = i * tq + r0 + lax.broadcasted_iota(jnp.int32, (s.shape[0], 1), 0)
        col = j * tk + lax.broadcasted_iota(jnp.int32, (1, tk), 1)
        return jnp.where(col <= row, s, NEG_INF)

    outs = _flash_sweep_ref(0, (i * tq) // tk, hp, score, value, causal, tq, tk, bufs)
    for h in range(hp):
        o_ref[:, h * HEAD_DIM:(h + 1) * HEAD_DIM] = outs[h].astype(BF16)


def _fox_attention(proj, cum_col, cum_row, tq=512, tk=512, hp=4):
    b, s, _ = proj.shape
    w = hp * HEAD_DIM
    ck = cum_row.reshape(b, 8, s // tk, tk)
    return pl.pallas_call(
        functools.partial(_fox_kernel, tq=tq, tk=tk, hp=hp),
        grid=(b, N_HEADS // hp, s // tq),
        in_specs=[pl.BlockSpec((None, tq, w), lambda bi, hg, i: (bi, i, COL_CQ // hp + hg)),
                  pl.BlockSpec((None, s, w), lambda bi, hg, i: (bi, 0, COL_CK // hp + hg)),
                  pl.BlockSpec((None, s, w), lambda bi, hg, i: (bi, 0, COL_CV // hp + hg)),
                  pl.BlockSpec((None, tq, LANES), lambda bi, hg, i: (bi, i, 0)),
                  pl.BlockSpec((None, hp, s // tk, tk), lambda bi, hg, i: (bi, hg, 0, 0))],
        out_specs=pl.BlockSpec((None, tq, w), lambda bi, hg, i: (bi, i, hg)),
        out_shape=jax.ShapeDtypeStruct((b, s, GROUP), BF16),
        scratch_shapes=_flash_scratch(hp, tq, tk),
        compiler_params=_params(3),
        name="fox_attention",
    )(proj, proj, proj, cum_col, ck)


def _diff_kernel(q_ref, k_ref, v_ref, dl_ref, gn_ref, o_ref, *bufs, tq, tk, hp):
    assert tk >= tq
    i = pl.program_id(2)
    lane = lax.broadcasted_iota(jnp.int32, (1, LANES), 1)
    zero = jnp.zeros((), BF16)
    qs = []
    for h in range(hp):
        qh = _head(q_ref, h)
        qs.append(jnp.concatenate([jnp.where(lane < DIFF_QK, qh, zero),
                                   jnp.where(lane >= DIFF_QK, qh, zero)], axis=0))

    def score(h, j):
        return _dot_nt(qs[h], _head(k_ref, h, _key_rows(j, tk)))

    def value(h, j):
        return _head(v_ref, h, _key_rows(j, tk))

    def causal(j, s, r0):
        row = i * tq + r0 % tq + lax.broadcasted_iota(jnp.int32, (s.shape[0], 1), 0)
        col = j * tk + lax.broadcasted_iota(jnp.int32, (1, tk), 1)
        return jnp.where(col <= row, s, NEG_INF)

    outs = _flash_sweep_ref(0, (i * tq) // tk, hp, score, value, causal, 2 * tq, tk, bufs,
                            rc=16)
    dl = dl_ref[...]
    lam = (jnp.exp(jnp.sum(dl[0:1] * dl[1:2], axis=1, keepdims=True))
           - jnp.exp(jnp.sum(dl[2:3] * dl[3:4], axis=1, keepdims=True)) + dl[4:5, 0:1])
    for h in range(hp):
        od = outs[h][:tq] - lam * outs[h][tq:]
        o_ref[:, h * HEAD_DIM:(h + 1) * HEAD_DIM] = (
            _rms(od, gn_ref[...]) * dl[5:6, 0:1]).astype(BF16)


def _diff_attention(proj, dl, gn, tq=256, tk=512, hp=4):
    b, s, _ = proj.shape
    w = hp * HEAD_DIM
    return pl.pallas_call(
        functools.partial(_diff_kernel, tq=tq, tk=tk, hp=hp),
        grid=(b, N_HEADS // hp, s // tq),
        in_specs=[pl.BlockSpec((None, tq, w), lambda bi, hg, i: (bi, i, COL_DQ // hp + hg)),
                  pl.BlockSpec((None, s, w), lambda bi, hg, i: (bi, 0, COL_DK // hp + hg)),
                  pl.BlockSpec((None, s, w), lambda bi, hg, i: (bi, 0, COL_DV // hp + hg)),
                  pl.BlockSpec((8, LANES), lambda bi, hg, i: (0, 0)),
                  pl.BlockSpec((1, LANES), lambda bi, hg, i: (0, 0))],
        out_specs=pl.BlockSpec((None, tq, w), lambda bi, hg, i: (bi, i, hg)),
        out_shape=jax.ShapeDtypeStruct((b, s, GROUP), BF16),
        scratch_shapes=_flash_scratch(hp, 2 * tq, tk),
        compiler_params=_params(3),
        name="diff_attention",
    )(proj, proj, proj, dl, gn)


def _cmp_kernel(k2_ref, v2_ref, pek_ref, pev_ref, wk1_ref, wk2_ref, wv1_ref, wv2_ref,
                kc_ref, vc_ref):
    nb = k2_ref.shape[0]
    w = NSA_CMP_STRIDE * HEAD_DIM

    def compress(x2, pe, w1_ref, w2_ref):
        a = _dot((x2 + pe[:, :w]).astype(BF16), w1_ref[:w, :])
        bb = _dot((x2 + pe[:, w:]).astype(BF16), w1_ref[w:, :])
        hid = jax.nn.gelu(a + pltpu.roll(bb, nb - 1, 0), approximate=True)
        return _dot(hid.astype(BF16), w2_ref[...])

    kc_ref[...] = compress(k2_ref[...], pek_ref[...], wk1_ref, wk2_ref).astype(BF16)
    vc_ref[...] = compress(v2_ref[...], pev_ref[...], wv1_ref, wv2_ref).astype(BF16)


def _nsa_compress(k2, v2, pek, pev, wk1, wk2, wv1, wv2):
    b, nb, w = k2.shape
    hid = wk1.shape[1]
    full = lambda *shape: pl.BlockSpec(shape, lambda bi: (0,) * len(shape))
    return pl.pallas_call(
        _cmp_kernel,
        grid=(b,),
        in_specs=[pl.BlockSpec((None, nb, w), lambda bi: (bi, 0, 0)),
                  pl.BlockSpec((None, nb, w), lambda bi: (bi, 0, 0)),
                  full(1, 2 * w), full(1, 2 * w),
                  full(2 * w, hid), full(hid, HEAD_DIM), full(2 * w, hid), full(hid, HEAD_DIM)],
        out_specs=[pl.BlockSpec((None, nb, HEAD_DIM), lambda bi: (bi, 0, 0)),
                   pl.BlockSpec((None, nb, HEAD_DIM), lambda bi: (bi, 0, 0))],
        out_shape=[jax.ShapeDtypeStruct((b, nb, HEAD_DIM), BF16),
                   jax.ShapeDtypeStruct((b, nb, HEAD_DIM), BF16)],
        compiler_params=_params(1),
        name="nsa_compress",
    )(k2, v2, pek, pev, wk1, wk2, wv1, wv2)


def _nsa_kernel(q_ref, qr_ref, g_ref, kc_ref, vc_ref, ks_ref, kw_ref, vs_ref, vw_ref, o_ref,
                *bufs, tq, tk, n_top):
    assert tk >= tq
    i = pl.program_id(1)
    nb = kc_ref.shape[0]
    n_sel = NSA_SLC_BLOCK
    hq = N_HEADS * tq

    q_raw = jnp.concatenate([_head(q_ref, h) for h in range(N_HEADS)], axis=0)
    q_rot = jnp.concatenate([_head(qr_ref, h) for h in range(N_HEADS)], axis=0)
    row = i * tq + lax.broadcasted_iota(jnp.int32, (tq, 1), 0)
    rep = lambda a: jnp.concatenate([a] * N_HEADS, axis=0)
    n_full = (i * tq) // tk

    def causal(j, s, r0):
        rows = i * tq + r0 % tq + lax.broadcasted_iota(jnp.int32, (s.shape[0], 1), 0)
        col = j * tk + lax.broadcasted_iota(jnp.int32, (1, tk), 1)
        return jnp.where(col <= rows, s, NEG_INF)

    cidx = lax.broadcasted_iota(jnp.int32, (1, nb), 1)
    c_valid = rep((cidx * NSA_CMP_STRIDE + (NSA_CMP_BLOCK - 1)) <= row)
    sc = jnp.where(c_valid, _dot_nt(q_raw, kc_ref[...]), NEG_INF)
    e = jnp.exp2(sc - jnp.max(sc, axis=1, keepdims=True))
    p = jnp.where(c_valid, e * (1.0 / jnp.sum(e, axis=1, keepdims=True)), 0.0)
    o_cmp = _dot(p.astype(BF16), vc_ref[...])
    p_sum = p[0:tq]
    for h in range(1, N_HEADS):
        p_sum = p_sum + p[h * tq:(h + 1) * tq]
    oc = lax.broadcasted_iota(jnp.int32, (nb, LANES), 0)
    oj = lax.broadcasted_iota(jnp.int32, (nb, LANES), 1)
    per = NSA_SLC_BLOCK // NSA_CMP_STRIDE
    overlap = (jnp.where(oc // per == oj, 1.0, 0.0)
               + jnp.where((oc + 1) // per == oj, 1.0, 0.0)).astype(BF16)
    imp = _dot3(p_sum, overlap)

    imp_t = jnp.transpose(imp)[:n_sel, :]
    jb = lax.broadcasted_iota(jnp.int32, (n_sel, 1), 0)
    cur = (i * tq + lax.broadcasted_iota(jnp.int32, (1, tq), 1)) // NSA_SLC_BLOCK
    sel_valid = jb <= cur
    forced = sel_valid & ((jb == 0) | (jb >= cur - 1))
    score = jnp.where(forced, FORCE_SCORE, jnp.where(sel_valid, imp_t, NEG_INF))
    cnt = jnp.zeros((n_sel, tq), F32)
    for ii in range(n_sel):
        ri = score[ii:ii + 1, :]
        beats = (ri > score) | ((ri == score) & (jb > ii))
        cnt = cnt + jnp.where(beats, 1.0, 0.0)
    sel_t = jnp.where((cnt < n_top) & sel_valid, 1.0, 0.0)
    sel = jnp.transpose(jnp.concatenate([sel_t, jnp.zeros_like(sel_t)], axis=0)).astype(BF16)

    er = lax.broadcasted_iota(jnp.int32, (LANES, tk), 0)
    ec = lax.broadcasted_iota(jnp.int32, (LANES, tk), 1)

    def slc_score(_, j):
        expand = jnp.where(er == (j * tk + ec) // NSA_SLC_BLOCK, 1.0, 0.0).astype(BF16)
        ok = _dot(sel, expand) > 0.5
        return jnp.where(rep(ok), _dot_nt(q_rot, ks_ref[_key_rows(j, tk), :]), NEG_INF)

    o_slc, = _flash_sweep_ref(0, n_full, 1, slc_score,
                              lambda _, j: vs_ref[_key_rows(j, tk), :], causal, hq, tk, bufs,
                              rc=16)

    def win_score(_, j):
        col = j * tk + lax.broadcasted_iota(jnp.int32, (1, tk), 1)
        return jnp.where(rep(col > row - NSA_WINDOW),
                         _dot_nt(q_rot, kw_ref[_key_rows(j, tk), :]), NEG_INF)

    j_lo = (jnp.maximum(i * tq - NSA_WINDOW, 0) // (2 * tk)) * 2
    o_win, = _flash_sweep_ref(j_lo, n_full, 1, win_score,
                              lambda _, j: vw_ref[_key_rows(j, tk), :], causal, hq, tk, bufs,
                              rc=16)

    g = jax.nn.sigmoid(g_ref[...])
    for h in range(N_HEADS):
        rows = slice(h * tq, (h + 1) * tq)
        o_ref[:, h * HEAD_DIM:(h + 1) * HEAD_DIM] = (
            g[:, 3 * h:3 * h + 1] * o_cmp[rows] + g[:, 3 * h + 1:3 * h + 2] * o_slc[rows]
            + g[:, 3 * h + 2:3 * h + 3] * o_win[rows]).astype(BF16)


def _nsa_attention(proj, aux, kc, vc, tq=256, tk=256):
    b, s, _ = proj.shape
    nb = kc.shape[1]
    n_top = min(NSA_TOPK, s // NSA_SLC_BLOCK)
    kv = lambda col: pl.BlockSpec((None, s, HEAD_DIM), lambda bi, i: (bi, 0, col))
    return pl.pallas_call(
        functools.partial(_nsa_kernel, tq=tq, tk=tk, n_top=n_top),
        grid=(b, s // tq),
        in_specs=[pl.BlockSpec((None, tq, GROUP), lambda bi, i: (bi, i, COL_AQ // N_HEADS)),
                  pl.BlockSpec((None, tq, GROUP), lambda bi, i: (bi, i, COL_AQR // N_HEADS)),
                  pl.BlockSpec((None, tq, LANES), lambda bi, i: (bi, i, AUX_G)),
                  pl.BlockSpec((None, nb, HEAD_DIM), lambda bi, i: (bi, 0, 0)),
                  pl.BlockSpec((None, nb, HEAD_DIM), lambda bi, i: (bi, 0, 0)),
                  kv(COL_AKS), kv(COL_AKW), kv(COL_AVS), kv(COL_AVW)],
        out_specs=pl.BlockSpec((None, tq, GROUP), lambda bi, i: (bi, i, 0)),
        out_shape=jax.ShapeDtypeStruct((b, s, GROUP), BF16),
        scratch_shapes=_flash_scratch(1, N_HEADS * tq, tk),
        compiler_params=_params(2),
        name="nsa_attention",
    )(proj, proj, aux, kc, vc, proj, proj, proj, proj)


def _out_kernel(oa_ref, ob_ref, oc_ref, od_ref, w_ref, x_ref, g_ref, o_ref):
    acc = _dot(oa_ref[...], w_ref[0:GROUP, :])
    acc = acc + _dot(ob_ref[...], w_ref[GROUP:2 * GROUP, :])
    acc = acc + _dot(oc_ref[...], w_ref[2 * GROUP:3 * GROUP, :])
    acc = acc + _dot(od_ref[...], w_ref[3 * GROUP:4 * GROUP, :])
    o_ref[...] = x_ref[...] + _rms(acc, g_ref[...])


def _out_proj(oa, ob, oc, od, w, layer, x2d, g, tm=512):
    t, d = x2d.shape
    mix = pl.BlockSpec((tm, GROUP), lambda i: (i, 0))
    return pl.pallas_call(
        _out_kernel,
        grid=(t // tm,),
        in_specs=[mix, mix, mix, mix,
                  pl.BlockSpec((None, 4 * GROUP, d), lambda i: (layer, 0, 0)),
                  pl.BlockSpec((tm, d), lambda i: (i, 0)),
                  pl.BlockSpec((1, d), lambda i: (0, 0))],
        out_specs=pl.BlockSpec((tm, d), lambda i: (i, 0)),
        out_shape=jax.ShapeDtypeStruct((t, d), F32),
        compiler_params=_params(1),
        name="out_proj",
    )(oa, ob, oc, od, w, x2d, g)


HALO = 16


def _mlp_kernel(x_ref, xh_ref, gpre_ref, wg_ref, wu_ref, cwg_ref, cwu_ref, cbg_ref, cbu_ref,
                wd_ref, gpost_ref, o_ref, xn_ref, acc_ref, *, tm, tiles_per_seq):
    i = pl.program_id(0)
    f = pl.program_id(1)

    @pl.when(f == 0)
    def _():
        xn_ref[HALO:, :] = _rms(x_ref[...], gpre_ref[...]).astype(BF16)
        halo = _rms(xh_ref[...], gpre_ref[...])
        seq_start = (i % tiles_per_seq) == 0
        xn_ref[:HALO, :] = jnp.where(seq_start, 0.0, halo).astype(BF16)
        acc_ref[...] = jnp.zeros_like(acc_ref)

    xn = xn_ref[...]

    def conv(u, cw_ref, cb_ref):
        cw = cw_ref[...]
        out = cb_ref[...] + cw[0:1] * u[HALO - 2:HALO - 2 + tm]
        out = out + cw[1:2] * u[HALO - 1:HALO - 1 + tm]
        return out + cw[2:3] * u[HALO:HALO + tm]

    gate = conv(_dot(xn, wg_ref[...]), cwg_ref, cbg_ref)
    up = conv(_dot(xn, wu_ref[...]), cwu_ref, cbu_ref)
    hid = (jax.nn.gelu(gate, approximate=True) * up).astype(BF16)
    acc_ref[...] += _dot(hid, wd_ref[...])

    @pl.when(f == pl.num_programs(1) - 1)
    def _():
        o_ref[...] = x_ref[...] + _rms(acc_ref[...], gpost_ref[...])


def _mlp(x2d, gpre, w_up, conv_w, conv_b, w_down, layer, gpost, seq, tm=1024, tf=512):
    t, d = x2d.shape
    ff = w_down.shape[1]
    nf = ff // tf
    hb = tm // HALO
    once = pl.Buffered(1)
    return pl.pallas_call(
        functools.partial(_mlp_kernel, tm=tm, tiles_per_seq=seq // tm),
        grid=(t // tm, nf),
        in_specs=[pl.BlockSpec((tm, d), lambda i, f: (i, 0), pipeline_mode=once),
                  pl.BlockSpec((HALO, d), lambda i, f: (jnp.maximum(i * hb - 1, 0), 0)),
                  pl.BlockSpec((1, d), lambda i, f: (0, 0)),
                  pl.BlockSpec((None, d, tf), lambda i, f: (layer, 0, f)),
                  pl.BlockSpec((None, d, tf), lambda i, f: (layer, 0, nf + f)),
                  pl.BlockSpec((3, tf), lambda i, f: (0, f)),
                  pl.BlockSpec((3, tf), lambda i, f: (0, nf + f)),
                  pl.BlockSpec((1, tf), lambda i, f: (0, f)),
                  pl.BlockSpec((1, tf), lambda i, f: (0, nf + f)),
                  pl.BlockSpec((None, tf, d), lambda i, f: (layer, f, 0)),
                  pl.BlockSpec((1, d), lambda i, f: (0, 0))],
        out_specs=pl.BlockSpec((tm, d), lambda i, f: (i, 0), pipeline_mode=once),
        out_shape=jax.ShapeDtypeStruct((t, d), F32),
        scratch_shapes=[pltpu.VMEM((tm + HALO, d), BF16), pltpu.VMEM((tm, d), F32)],
        compiler_params=_params(2),
        name="conv_geglu_mlp",
    )(x2d, x2d, gpre, w_up, w_up, conv_w, conv_w, conv_b, conv_b, w_down, gpost)


_IN_OFF = np.concatenate([[0], np.cumsum(_IN_SPLITS)]).tolist()
_HEAD_RUNS = ((0, 0), (0, 0), (3, 6), (8, 13), (15, 17))


def _repack_kernel(w_ref, heads_ref, aux_ref):
    def run(first, last):
        return w_ref[:, _IN_OFF[first]:_IN_OFF[last + 1]].astype(BF16)

    at = 0
    for first, last in _HEAD_RUNS:
        width = _IN_OFF[last + 1] - _IN_OFF[first]
        heads_ref[:, at:at + width] = run(first, last)
        at += width
    aux_ref[...] = jnp.zeros(aux_ref.shape, BF16)
    aux_ref[:, 0:2 * LANES] = run(1, 2)
    aux_ref[:, AUX_G * LANES:AUX_G * LANES + _IN_SPLITS[7]] = run(7, 7)
    aux_ref[:, AUX_F * LANES:AUX_F * LANES + _IN_SPLITS[14]] = run(14, 14)


def _split_w_in(w, rows=256):
    w = jnp.pad(w, ((0, 0), (0, 0), (0, -w.shape[2] % LANES))).astype(BF16)
    depth, d, n_in = w.shape
    n_heads, n_aux = HEAD_BLOCKS * LANES, AUX_BLOCKS * LANES
    return pl.pallas_call(
        _repack_kernel,
        grid=(depth, d // rows),
        in_specs=[pl.BlockSpec((None, rows, n_in), lambda l, r: (l, r, 0))],
        out_specs=[pl.BlockSpec((None, rows, n_heads), lambda l, r: (l, r, 0)),
                   pl.BlockSpec((None, rows, n_aux), lambda l, r: (l, r, 0))],
        out_shape=[jax.ShapeDtypeStruct((depth, d, n_heads), BF16),
                   jax.ShapeDtypeStruct((depth, d, n_aux), BF16)],
        compiler_params=_params(2),
        name="repack_w_in",
    )(w)


def _rope_lane_tables(seq, rot_dim, period):
    pos = jnp.arange(seq, dtype=F32)
    inv = ROPE_THETA ** (-jnp.arange(0, rot_dim, 2, dtype=F32) / rot_dim)
    ang = pos[:, None] * inv[None, :]
    cos, sin = jnp.cos(ang), jnp.sin(ang)
    rest = period - rot_dim
    c = jnp.concatenate([cos, cos, jnp.ones((seq, rest), F32)], axis=-1)
    s = jnp.concatenate([-sin, sin, jnp.zeros((seq, rest), F32)], axis=-1)
    reps = LANES // period
    return jnp.tile(c, (1, reps)), jnp.tile(s, (1, reps))


@jax.jit
def _forward(x, norm_mix_pre, norm_mix_post, norm_mlp_pre, norm_mlp_post, w_in, w_out,
             nsa_pe_k, nsa_pe_v, nsa_cmp_k1, nsa_cmp_k2, nsa_cmp_v1, nsa_cmp_v2,
             fox_forget_bias, diff_lambda, diff_norm, mlp_w_up, mlp_conv_w, mlp_conv_b,
             mlp_w_down):
    b, s, d = x.shape
    depth = w_in.shape[0]
    cos_h, sin_h = _rope_lane_tables(s, HEAD_DIM // 4, LANES)
    cos_d, sin_d = _rope_lane_tables(s, DIFF_QK // 4, DIFF_QK)
    cos_sets = jnp.stack([cos_h * SCALE_HEAD, cos_h, cos_d * SCALE_DIFF, cos_d])
    sin_sets = jnp.stack([sin_h * SCALE_HEAD, sin_h, sin_d * SCALE_DIFF, sin_d])
    w_heads, w_aux = _split_w_in(w_in)
    w_out_b = w_out.astype(BF16)
    w_up_b = mlp_w_up.astype(BF16)
    w_down_b = mlp_w_down.astype(BF16)
    k1_b, k2_b = nsa_cmp_k1.astype(BF16), nsa_cmp_k2.astype(BF16)
    v1_b, v2_b = nsa_cmp_v1.astype(BF16), nsa_cmp_v2.astype(BF16)
    fox_bias = jnp.pad(fox_forget_bias, ((0, 0), (0, LANES - N_HEADS)))
    row2 = lambda a: a.reshape(1, -1)
    cmp_w = NSA_CMP_STRIDE * HEAD_DIM

    xt = x.reshape(b * s, d)
    for l in range(depth):
        g_pre = row2(norm_mix_pre[l])
        proj = _head_proj(xt, g_pre, w_heads, l, cos_sets, sin_sets, s, tm=min(1024, s))
        proj = proj.reshape(b, s, HEAD_BLOCKS * LANES)
        aux = _aux_proj(xt, g_pre, w_aux, l).reshape(b, s, AUX_BLOCKS * LANES)

        k2 = aux[:, :, AUX_KC * LANES:(AUX_KC + 1) * LANES].reshape(b, s // NSA_CMP_STRIDE, cmp_w)
        v2 = aux[:, :, AUX_VC * LANES:(AUX_VC + 1) * LANES].reshape(b, s // NSA_CMP_STRIDE, cmp_w)
        kc, vc = _nsa_compress(k2, v2, row2(nsa_pe_k[l]), row2(nsa_pe_v[l]),
                               k1_b[l], k2_b[l], v1_b[l], v2_b[l])
        o_a = _nsa_attention(proj, aux, kc, vc)

        o_b = _sb_attention(proj)

        cum_col, cum_row = _fox_cum(aux, fox_bias[l:l + 1])
        o_c = _fox_attention(proj, cum_col, cum_row)

        lam_init = 0.8 - 0.6 * math.exp(-0.3 * l)
        dl = jnp.concatenate([
            jnp.pad(diff_lambda[l], ((0, 0), (0, LANES - DIFF_QK))),
            jnp.full((1, LANES), lam_init, F32),
            jnp.full((1, LANES), 1.0 - lam_init, F32),
            jnp.zeros((2, LANES), F32)], axis=0)
        o_d = _diff_attention(proj, dl, row2(diff_norm[l]))

        flat = lambda o: o.reshape(b * s, GROUP)
        xt = _out_proj(flat(o_a), flat(o_b), flat(o_c), flat(o_d), w_out_b, l, xt,
                       row2(norm_mix_post[l]))
        xt = _mlp(xt, row2(norm_mlp_pre[l]), w_up_b, mlp_conv_w[l], row2(mlp_conv_b[l]),
                  w_down_b, l, row2(norm_mlp_post[l]), s, tm=min(1024, s))
    return xt.reshape(b, s, d)


def kernel(x, norm_mix_pre, norm_mix_post, norm_mlp_pre, norm_mlp_post, w_in, w_out, nsa_pe_k, nsa_pe_v, nsa_cmp_k1, nsa_cmp_k2, nsa_cmp_v1, nsa_cmp_v2, fox_forget_bias, diff_lambda, diff_norm, mlp_w_up, mlp_conv_w, mlp_conv_b, mlp_w_down):
    return _forward(x, norm_mix_pre, norm_mix_post, norm_mlp_pre, norm_mlp_post, w_in, w_out,
                    nsa_pe_k, nsa_pe_v, nsa_cmp_k1, nsa_cmp_k2, nsa_cmp_v1, nsa_cmp_v2,
                    fox_forget_bias, diff_lambda, diff_norm, mlp_w_up, mlp_conv_w, mlp_conv_b,
                    mlp_w_down)
```

```python
import functools
import math

import jax
import jax.numpy as jnp
import numpy as np
from jax import lax
from jax.experimental import pallas as pl
from jax.experimental.pallas import tpu as pltpu

F32 = jnp.float32
BF16 = jnp.bfloat16

LANES = 128
HEAD_DIM = 128
N_HEADS = 4
GROUP = N_HEADS * HEAD_DIM
NSA_CMP_BLOCK = 32
NSA_CMP_STRIDE = 16
NSA_SLC_BLOCK = 64
NSA_TOPK = 16
NSA_WINDOW = 512
DIFF_QK = HEAD_DIM // 2
ROPE_HALF_HEAD = HEAD_DIM // 4 // 2
ROPE_HALF_DIFF = DIFF_QK // 4 // 2
EPS = 1e-6
NEG_INF = -1e30
FORCE_SCORE = 1e9
ROPE_THETA = 500000.0
LOG2E = math.log2(math.e)
SCALE_HEAD = HEAD_DIM ** -0.5 * LOG2E
SCALE_DIFF = DIFF_QK ** -0.5 * LOG2E

_IN_SPLITS = (GROUP, 128, 128, 128, 128, 128, 128, 3 * N_HEADS,
              GROUP, GROUP, GROUP, GROUP, GROUP, GROUP, N_HEADS, GROUP, GROUP, GROUP)
COL_AQ, COL_AQR, COL_AKS, COL_AVS, COL_AKW, COL_AVW = 0, 4, 8, 9, 10, 11
COL_BQ, COL_BK, COL_BV = 12, 16, 20
COL_CQ, COL_CK, COL_CV = 24, 28, 32
COL_DQ, COL_DK, COL_DV = 36, 40, 44
HEAD_BLOCKS = 48
AUX_KC, AUX_VC, AUX_G, AUX_F = 0, 1, 2, 3
AUX_BLOCKS = 4

PROJ_TN = GROUP
_TILES_SCALE = (0, 3, 6)
_TILE_ROPE_Q = 1
_TILE_ROPE_KV = 2
_TILES_ROPE_DIFF = (9, 10)
_ROPE_SET_STARTS = (2, 9, 10)

VMEM_LIMIT = 56 * 1024 * 1024


def _params(n_axes, vmem=VMEM_LIMIT):
    return pltpu.CompilerParams(dimension_semantics=("arbitrary",) * n_axes,
                                vmem_limit_bytes=vmem)


def _rms(x, g):
    ms = jnp.mean(x * x, axis=-1, keepdims=True)
    return x * lax.rsqrt(ms + EPS) * g


def _dot(a, b):
    return jnp.dot(a, b, preferred_element_type=F32)


def _dot_nt(a, b):
    return lax.dot_general(a, b, (((1,), (1,)), ((), ())), preferred_element_type=F32)


def _split3(x):
    hi = x.astype(BF16)
    r = x - hi.astype(F32)
    mid = r.astype(BF16)
    lo = (r - mid.astype(F32)).astype(BF16)
    return hi, mid, lo


def _dot3(x, m):
    hi, mid, lo = _split3(x)
    return _dot(hi, m) + _dot(mid, m) + _dot(lo, m)


def _dot3_left(m, x):
    hi, mid, lo = _split3(x)
    return _dot(m, hi) + _dot(m, mid) + _dot(m, lo)


def _rope(x, c, s, half):
    width = x.shape[1]
    lane = lax.broadcasted_iota(jnp.int32, (1, width), 1)
    first = (lane % (2 * half)) < half
    xs = jnp.where(first, pltpu.roll(x, width - half, 1), pltpu.roll(x, half, 1))
    return x * c + xs * s


def _lane_fold(x, op):
    out = x[:, :LANES]
    for b in range(1, x.shape[1] // LANES):
        out = op(out, x[:, b * LANES:(b + 1) * LANES])
    return out


def _row_max(s):
    return jnp.max(_lane_fold(s, jnp.maximum), axis=1, keepdims=True)


def _flash_scratch(n_chain, rows, tk):
    stat = pltpu.VMEM((n_chain, rows, LANES), F32)
    return [pltpu.VMEM((n_chain, rows, tk), F32), pltpu.VMEM((n_chain, rows, tk), F32),
            pltpu.VMEM((n_chain, rows, tk), BF16), stat, stat,
            pltpu.VMEM((n_chain, rows, 2 * LANES), F32)]


def _flash_sweep_ref(lo, n_full, n_chain, score_fn, value_fn, mask_fn, rows, tk, bufs, rc=16):
    s_even, s_odd, p_buf, m_ref, a_ref, acc_ref = bufs
    chains = range(n_chain)
    reps = tk // LANES
    ones = jnp.ones((tk, LANES), BF16)
    m_ref[...] = jnp.full(m_ref.shape, NEG_INF, F32)
    acc_ref[...] = jnp.zeros(acc_ref.shape, F32)

    def scores(s_buf, j):
        for c in chains:
            s_buf[c] = score_fn(c, j)

    def consume(s_buf, j, masked):
        for c in chains:
            for r0 in range(0, rows, rc):
                rr = slice(r0, r0 + rc)
                s = s_buf[c, rr, :]
                if masked:
                    s = mask_fn(j, s, r0)
                m_old = m_ref[c, rr, :]
                m_new = jnp.maximum(m_old, _row_max(s))
                m_ref[c, rr, :] = m_new
                a_ref[c, rr, :] = jnp.exp2(m_old - m_new)
                shifted = s - jnp.concatenate([m_new] * reps, axis=1)
                p_buf[c, rr, :] = jnp.exp2(shifted.astype(BF16))
            v_ones = jnp.concatenate([value_fn(c, j), ones], axis=1)
            alpha = a_ref[c]
            acc_ref[c] = (jnp.concatenate([alpha, alpha], axis=1) * acc_ref[c]
                          + _dot(p_buf[c], v_ones))

    scores(s_even, lo)

    def pair(t, carry):
        j = lo + 2 * t
        scores(s_odd, j + 1)
        consume(s_even, j, False)
        scores(s_even, j + 2)
        consume(s_odd, j + 1, False)
        return carry

    n_pair = (n_full - lo) // 2
    lax.fori_loop(0, n_pair, pair, 0)
    j = lo + 2 * n_pair
    last_full = j < n_full

    @pl.when(last_full)
    def _():
        scores(s_odd, j + 1)

    consume(s_even, j, True)

    @pl.when(last_full)
    def _():
        consume(s_odd, j + 1, True)

    return [acc_ref[c, :, :LANES] * (1.0 / acc_ref[c, :, LANES:]) for c in chains]


def _head(ref, h, rows=None):
    cols = slice(h * HEAD_DIM, (h + 1) * HEAD_DIM)
    return ref[:, cols] if rows is None else ref[rows, cols]


def _key_rows(j, tk):
    return pl.ds(pl.multiple_of(j * tk, tk), tk)


def _tile_in(j, tiles):
    hit = j == tiles[0]
    for t in tiles[1:]:
        hit = hit | (j == t)
    return hit


def _head_proj_kernel(x_ref, g_ref, w_ref, wa_ref, c_ref, s_ref, o_ref, aux_ref, xn_ref, *,
                      slabs):
    j = pl.program_id(1)

    @pl.when(j == 0)
    def _():
        xn_ref[...] = _rms(x_ref[...], g_ref[...]).astype(BF16)
        aux_ref[...] = _dot(xn_ref[...], wa_ref[...])

    is_scale = _tile_in(j, _TILES_SCALE)
    is_rope_d = _tile_in(j, _TILES_ROPE_DIFF)
    is_rope = (j == _TILE_ROPE_Q) | (j == _TILE_ROPE_KV) | is_rope_d
    slab = o_ref.shape[0] // slabs

    def tile(epilogue):
        for r in range(slabs):
            rows = slice(r * slab, (r + 1) * slab)
            o_ref[rows, :] = epilogue(_dot(xn_ref[rows, :], w_ref[...]), rows).astype(BF16)

    def rope_tile(half, rotated):
        def epilogue(acc, rows):
            ones = jnp.ones((slab, LANES), F32)
            c = jnp.concatenate([c_ref[rows, :] if r else ones for r in rotated], axis=1)
            s = jnp.concatenate([s_ref[rows, :] if r else 0.0 * ones for r in rotated], axis=1)
            return _rope(acc, c, s, half)
        tile(epilogue)

    every = (True,) * N_HEADS
    pl.when(is_scale)(lambda: tile(lambda acc, rows: acc * SCALE_HEAD))
    pl.when(j == _TILE_ROPE_Q)(lambda: rope_tile(ROPE_HALF_HEAD, every))
    pl.when(j == _TILE_ROPE_KV)(lambda: rope_tile(ROPE_HALF_HEAD, (True, False, True, False)))
    pl.when(is_rope_d)(lambda: rope_tile(ROPE_HALF_DIFF, every))
    pl.when(jnp.logical_not(is_scale | is_rope))(lambda: tile(lambda acc, rows: acc))


def _rope_set(j):
    idx = 0
    for start in _ROPE_SET_STARTS:
        idx = idx + (j >= start).astype(jnp.int32)
    return idx


def _head_proj(x2d, g, w, w_aux, layer, cos_sets, sin_sets, seq, tm=1024, slabs=4):
    t, d = x2d.shape
    n = w.shape[2]
    n_aux = w_aux.shape[2]
    per_seq = seq // tm
    table = pl.BlockSpec((None, tm, LANES), lambda i, j: (_rope_set(j), i % per_seq, 0))
    return pl.pallas_call(
        functools.partial(_head_proj_kernel, slabs=slabs),
        grid=(t // tm, n // PROJ_TN),
        in_specs=[pl.BlockSpec((tm, d), lambda i, j: (i, 0)),
                  pl.BlockSpec((1, d), lambda i, j: (0, 0)),
                  pl.BlockSpec((None, d, PROJ_TN), lambda i, j: (layer, 0, j)),
                  pl.BlockSpec((None, d, n_aux), lambda i, j: (layer, 0, 0)),
                  table, table],
        out_specs=[pl.BlockSpec((tm, PROJ_TN), lambda i, j: (i, j)),
                   pl.BlockSpec((tm, n_aux), lambda i, j: (i, 0))],
        out_shape=[jax.ShapeDtypeStruct((t, n), BF16), jax.ShapeDtypeStruct((t, n_aux), F32)],
        scratch_shapes=[pltpu.VMEM((tm, d), BF16)],
        compiler_params=_params(2),
        name="head_proj",
    )(x2d, g, w, w_aux, cos_sets, sin_sets)


def _sb_kernel(q_ref, k_ref, v_ref, o_ref,
               ls_e, ls_o, hl_e, hl_o, tot_e, tot_o, suf_ref, e_ref, c_ref, acc_ref,
               *, tq, tk, hp, rc):
    assert tq == tk
    i = pl.program_id(2)
    reps = tk // LANES
    heads = range(hp)
    jr = lax.broadcasted_iota(jnp.int32, (tk, tk), 0)
    jc = lax.broadcasted_iota(jnp.int32, (tk, tk), 1)
    u = jnp.where(jr > jc, 1.0, 0.0).astype(BF16)
    u2 = jnp.concatenate([u, u], axis=0)
    qs = [_head(q_ref, h) for h in heads]
    c_ref[...] = jnp.zeros(c_ref.shape, F32)
    acc_ref[...] = jnp.zeros(acc_ref.shape, F32)

    def prepare(bufs, j, diagonal):
        ls_ref, hl_ref, tot_ref = bufs
        for h in heads:
            z = _dot_nt(qs[h], _head(k_ref, h, _key_rows(j, tk)))
            for r0 in range(0, tq, rc):
                rr = slice(r0, r0 + rc)
                zc = z[rr]
                ls = jnp.minimum(zc, 0.0) - jnp.log2(1.0 + jnp.exp2(-jnp.abs(zc)))
                lk = ls - zc
                if diagonal:
                    row = r0 + lax.broadcasted_iota(jnp.int32, (rc, 1), 0)
                    col = lax.broadcasted_iota(jnp.int32, (1, tk), 1)
                    past = col < row
                    lk = jnp.where(past, lk, 0.0)
                    ls = jnp.where(past, ls, NEG_INF)
                hi = lk.astype(BF16)
                hl_ref[h * tq + r0:h * tq + r0 + rc, :tk] = hi
                hl_ref[h * tq + r0:h * tq + r0 + rc, tk:] = (lk - hi.astype(F32)).astype(BF16)
                ls_ref[h, rr, :] = ls
                tot = jnp.sum(_lane_fold(lk, jnp.add), axis=1, keepdims=True)
                tot_ref[h, rr, :] = jnp.broadcast_to(tot, (rc, LANES))

    def weigh(bufs, j):
        ls_ref, hl_ref, tot_ref = bufs
        suf_ref[...] = _dot(hl_ref[...], u2)
        for h in heads:
            for r0 in range(0, tq, rc):
                rr = slice(r0, r0 + rc)
                c = c_ref[h, rr, :]
                suf = suf_ref[h * tq + r0:h * tq + r0 + rc, :]
                pre = ls_ref[h, rr, :] + suf + jnp.concatenate([c] * reps, axis=1)
                e_ref[h, rr, :] = jnp.exp2(pre).astype(BF16)
                c_ref[h, rr, :] = c + tot_ref[h, rr, :]
            acc_ref[h] += _dot(e_ref[h], _head(v_ref, h, _key_rows(j, tk)))

    even = (ls_e, hl_e, tot_e)
    odd = (ls_o, hl_o, tot_o)
    prepare(even, i, True)

    def pair(p, carry):
        j = i - 2 * p
        prepare(odd, j - 1, False)
        weigh(even, j)
        prepare(even, j - 2, False)
        weigh(odd, j - 1)
        return carry

    n_pair = i // 2
    lax.fori_loop(0, n_pair, pair, 0)
    j = i - 2 * n_pair
    one_more = j > 0

    @pl.when(one_more)
    def _():
        prepare(odd, j - 1, False)

    weigh(even, j)

    @pl.when(one_more)
    def _():
        weigh(odd, j - 1)

    for h in heads:
        o_ref[:, h * HEAD_DIM:(h + 1) * HEAD_DIM] = acc_ref[h].astype(BF16)


def _sb_attention(proj, tq=256, tk=256, hp=4, rc=16):
    b, s, _ = proj.shape
    w = hp * HEAD_DIM
    wide_f32 = pltpu.VMEM((hp, tq, tk), F32)
    wide_bf16 = pltpu.VMEM((hp, tq, tk), BF16)
    stat = pltpu.VMEM((hp, tq, LANES), F32)
    return pl.pallas_call(
        functools.partial(_sb_kernel, tq=tq, tk=tk, hp=hp, rc=rc),
        grid=(b, N_HEADS // hp, s // tq),
        in_specs=[pl.BlockSpec((None, tq, w), lambda bi, hg, i: (bi, i, COL_BQ // hp + hg)),
                  pl.BlockSpec((None, s, w), lambda bi, hg, i: (bi, 0, COL_BK // hp + hg)),
                  pl.BlockSpec((None, s, w), lambda bi, hg, i: (bi, 0, COL_BV // hp + hg))],
        out_specs=pl.BlockSpec((None, tq, w), lambda bi, hg, i: (bi, i, hg)),
        out_shape=jax.ShapeDtypeStruct((b, s, GROUP), BF16),
        scratch_shapes=[wide_f32, wide_f32,
                        pltpu.VMEM((hp * tq, 2 * tk), BF16), pltpu.VMEM((hp * tq, 2 * tk), BF16),
                        stat, stat, pltpu.VMEM((hp * tq, tk), F32), wide_bf16, stat, stat],
        compiler_params=_params(3),
        name="sb_attention",
    )(proj, proj, proj)


def _fox_cum_kernel(cf_ref, bias_ref, col_ref, row_ref, *, chunk):
    s = cf_ref.shape[0]
    lf = jax.nn.log_sigmoid(cf_ref[...] + bias_ref[...]) * LOG2E
    rr = lax.broadcasted_iota(jnp.int32, (chunk, chunk), 0)
    cc = lax.broadcasted_iota(jnp.int32, (chunk, chunk), 1)
    tri = jnp.where(rr >= cc, 1.0, 0.0).astype(BF16)
    carry = jnp.zeros((1, LANES), F32)
    for c in range(s // chunk):
        pre = _dot3_left(tri, lf[c * chunk:(c + 1) * chunk]) + carry
        col_ref[c * chunk:(c + 1) * chunk, :] = pre
        carry = pre[chunk - 1:chunk, :]
    row_ref[...] = jnp.transpose(col_ref[...])[:8, :]


def _fox_cum(aux, bias_row, chunk=256):
    b, s, _ = aux.shape
    return pl.pallas_call(
        functools.partial(_fox_cum_kernel, chunk=chunk),
        grid=(b,),
        in_specs=[pl.BlockSpec((None, s, LANES), lambda bi: (bi, 0, AUX_F)),
                  pl.BlockSpec((1, LANES), lambda bi: (0, 0))],
        out_specs=[pl.BlockSpec((None, s, LANES), lambda bi: (bi, 0, 0)),
                   pl.BlockSpec((None, 8, s), lambda bi: (bi, 0, 0))],
        out_shape=[jax.ShapeDtypeStruct((b, s, LANES), F32),
                   jax.ShapeDtypeStruct((b, 8, s), F32)],
        compiler_params=_params(1),
        name="fox_cum",
    )(aux, bias_row)


def _fox_kernel(q_ref, k_ref, v_ref, cq_ref, ck_ref, o_ref, *bufs, tq, tk, hp):
    assert tk >= tq
    hg = pl.program_id(1)
    i = pl.program_id(2)
    lane = lax.broadcasted_iota(jnp.int32, (1, LANES), 1)
    cum_q = cq_ref[...]
    qs = [_head(q_ref, h) for h in range(hp)]
    cqs = [jnp.sum(jnp.where(lane == hg * hp + h, cum_q, 0.0), axis=1, keepdims=True)
           for h in range(hp)]

    def score(h, j):
        return (_dot_nt(qs[h], _head(k_ref, h, _key_rows(j, tk))) + cqs[h]
                - ck_ref[h, pl.ds(j, 1), :])

    def value(h, j):
        return _head(v_ref, h, _key_rows(j, tk))

    def causal(j, s, r0):
        row = i * tq + r0 + lax.broadcasted_iota(jnp.int32, (s.shape[0], 1), 0)
        col = j * tk + lax.broadcasted_iota(jnp.int32, (1, tk), 1)
        return jnp.where(col <= row, s, NEG_INF)

    outs = _flash_sweep_ref(0, (i * tq) // tk, hp, score, value, causal, tq, tk, bufs)
    for h in range(hp):
        o_ref[:, h * HEAD_DIM:(h + 1) * HEAD_DIM] = outs[h].astype(BF16)


def _fox_attention(proj, cum_col, cum_row, tq=512, tk=512, hp=4):
    b, s, _ = proj.shape
    w = hp * HEAD_DIM
    ck = cum_row.reshape(b, 8, s // tk, tk)
    return pl.pallas_call(
        functools.partial(_fox_kernel, tq=tq, tk=tk, hp=hp),
        grid=(b, N_HEADS // hp, s // tq),
        in_specs=[pl.BlockSpec((None, tq, w), lambda bi, hg, i: (bi, i, COL_CQ // hp + hg)),
                  pl.BlockSpec((None, s, w), lambda bi, hg, i: (bi, 0, COL_CK // hp + hg)),
                  pl.BlockSpec((None, s, w), lambda bi, hg, i: (bi, 0, COL_CV // hp + hg)),
                  pl.BlockSpec((None, tq, LANES), lambda bi, hg, i: (bi, i, 0)),
                  pl.BlockSpec((None, hp, s // tk, tk), lambda bi, hg, i: (bi, hg, 0, 0))],
        out_specs=pl.BlockSpec((None, tq, w), lambda bi, hg, i: (bi, i, hg)),
        out_shape=jax.ShapeDtypeStruct((b, s, GROUP), BF16),
        scratch_shapes=_flash_scratch(hp, tq, tk),
        compiler_params=_params(3),
        name="fox_attention",
    )(proj, proj, proj, cum_col, ck)


def _diff_kernel(q_ref, k_ref, v_ref, dl_ref, gn_ref, o_ref, *bufs, tq, tk, hp):
    assert tk >= tq
    i = pl.program_id(2)
    lane = lax.broadcasted_iota(jnp.int32, (1, LANES), 1)
    zero = jnp.zeros((), BF16)
    qs = []
    for h in range(hp):
        qh = _head(q_ref, h)
        qs.append(jnp.concatenate([jnp.where(lane < DIFF_QK, qh, zero),
                                   jnp.where(lane >= DIFF_QK, qh, zero)], axis=0))

    def score(h, j):
        return _dot_nt(qs[h], _head(k_ref, h, _key_rows(j, tk)))

    def value(h, j):
        return _head(v_ref, h, _key_rows(j, tk))

    def causal(j, s, r0):
        row = i * tq + r0 % tq + lax.broadcasted_iota(jnp.int32, (s.shape[0], 1), 0)
        col = j * tk + lax.broadcasted_iota(jnp.int32, (1, tk), 1)
        return jnp.where(col <= row, s, NEG_INF)

    outs = _flash_sweep_ref(0, (i * tq) // tk, hp, score, value, causal, 2 * tq, tk, bufs,
                            rc=16)
    dl = dl_ref[...]
    lam = (jnp.exp(jnp.sum(dl[0:1] * dl[1:2], axis=1, keepdims=True))
           - jnp.exp(jnp.sum(dl[2:3] * dl[3:4], axis=1, keepdims=True)) + dl[4:5, 0:1])
    for h in range(hp):
        od = outs[h][:tq] - lam * outs[h][tq:]
        o_ref[:, h * HEAD_DIM:(h + 1) * HEAD_DIM] = (
            _rms(od, gn_ref[...]) * dl[5:6, 0:1]).astype(BF16)


def _diff_attention(proj, dl, gn, tq=256, tk=512, hp=4):
    b, s, _ = proj.shape
    w = hp * HEAD_DIM
    return pl.pallas_call(
        functools.partial(_diff_kernel, tq=tq, tk=tk, hp=hp),
        grid=(b, N_HEADS // hp, s // tq),
        in_specs=[pl.BlockSpec((None, tq, w), lambda bi, hg, i: (bi, i, COL_DQ // hp + hg)),
                  pl.BlockSpec((None, s, w), lambda bi, hg, i: (bi, 0, COL_DK // hp + hg)),
                  pl.BlockSpec((None, s, w), lambda bi, hg, i: (bi, 0, COL_DV // hp + hg)),
                  pl.BlockSpec((8, LANES), lambda bi, hg, i: (0, 0)),
                  pl.BlockSpec((1, LANES), lambda bi, hg, i: (0, 0))],
        out_specs=pl.BlockSpec((None, tq, w), lambda bi, hg, i: (bi, i, hg)),
        out_shape=jax.ShapeDtypeStruct((b, s, GROUP), BF16),
        scratch_shapes=_flash_scratch(hp, 2 * tq, tk),
        compiler_params=_params(3),
        name="diff_attention",
    )(proj, proj, proj, dl, gn)


def _cmp_kernel(k2_ref, v2_ref, pek_ref, pev_ref, wk1_ref, wk2_ref, wv1_ref, wv2_ref,
                kc_ref, vc_ref):
    nb = k2_ref.shape[0]
    w = NSA_CMP_STRIDE * HEAD_DIM

    def compress(x2, pe, w1_ref, w2_ref):
        a = _dot((x2 + pe[:, :w]).astype(BF16), w1_ref[:w, :])
        bb = _dot((x2 + pe[:, w:]).astype(BF16), w1_ref[w:, :])
        hid = jax.nn.gelu(a + pltpu.roll(bb, nb - 1, 0), approximate=True)
        return _dot(hid.astype(BF16), w2_ref[...])

    kc_ref[...] = compress(k2_ref[...], pek_ref[...], wk1_ref, wk2_ref).astype(BF16)
    vc_ref[...] = compress(v2_ref[...], pev_ref[...], wv1_ref, wv2_ref).astype(BF16)


def _nsa_compress(k2, v2, pek, pev, wk1, wk2, wv1, wv2):
    b, nb, w = k2.shape
    hid = wk1.shape[1]
    full = lambda *shape: pl.BlockSpec(shape, lambda bi: (0,) * len(shape))
    return pl.pallas_call(
        _cmp_kernel,
        grid=(b,),
        in_specs=[pl.BlockSpec((None, nb, w), lambda bi: (bi, 0, 0)),
                  pl.BlockSpec((None, nb, w), lambda bi: (bi, 0, 0)),
                  full(1, 2 * w), full(1, 2 * w),
                  full(2 * w, hid), full(hid, HEAD_DIM), full(2 * w, hid), full(hid, HEAD_DIM)],
        out_specs=[pl.BlockSpec((None, nb, HEAD_DIM), lambda bi: (bi, 0, 0)),
                   pl.BlockSpec((None, nb, HEAD_DIM), lambda bi: (bi, 0, 0))],
        out_shape=[jax.ShapeDtypeStruct((b, nb, HEAD_DIM), BF16),
                   jax.ShapeDtypeStruct((b, nb, HEAD_DIM), BF16)],
        compiler_params=_params(1),
        name="nsa_compress",
    )(k2, v2, pek, pev, wk1, wk2, wv1, wv2)


def _nsa_kernel(q_ref, qr_ref, g_ref, kc_ref, vc_ref, ks_ref, kw_ref, vs_ref, vw_ref, o_ref,
                *bufs, tq, tk, n_top):
    assert tk >= tq
    i = pl.program_id(1)
    nb = kc_ref.shape[0]
    n_sel = NSA_SLC_BLOCK
    hq = N_HEADS * tq

    q_raw = jnp.concatenate([_head(q_ref, h) for h in range(N_HEADS)], axis=0)
    q_rot = jnp.concatenate([_head(qr_ref, h) for h in range(N_HEADS)], axis=0)
    row = i * tq + lax.broadcasted_iota(jnp.int32, (tq, 1), 0)
    rep = lambda a: jnp.concatenate([a] * N_HEADS, axis=0)
    n_full = (i * tq) // tk

    def causal(j, s, r0):
        rows = i * tq + r0 % tq + lax.broadcasted_iota(jnp.int32, (s.shape[0], 1), 0)
        col = j * tk + lax.broadcasted_iota(jnp.int32, (1, tk), 1)
        return jnp.where(col <= rows, s, NEG_INF)

    cidx = lax.broadcasted_iota(jnp.int32, (1, nb), 1)
    c_valid = rep((cidx * NSA_CMP_STRIDE + (NSA_CMP_BLOCK - 1)) <= row)
    sc = jnp.where(c_valid, _dot_nt(q_raw, kc_ref[...]), NEG_INF)
    e = jnp.exp2(sc - jnp.max(sc, axis=1, keepdims=True))
    p = jnp.where(c_valid, e * (1.0 / jnp.sum(e, axis=1, keepdims=True)), 0.0)
    o_cmp = _dot(p.astype(BF16), vc_ref[...])
    p_sum = p[0:tq]
    for h in range(1, N_HEADS):
        p_sum = p_sum + p[h * tq:(h + 1) * tq]
    oc = lax.broadcasted_iota(jnp.int32, (nb, LANES), 0)
    oj = lax.broadcasted_iota(jnp.int32, (nb, LANES), 1)
    per = NSA_SLC_BLOCK // NSA_CMP_STRIDE
    overlap = (jnp.where(oc // per == oj, 1.0, 0.0)
               + jnp.where((oc + 1) // per == oj, 1.0, 0.0)).astype(BF16)
    imp = _dot3(p_sum, overlap)

    imp_t = jnp.transpose(imp)[:n_sel, :]
    jb = lax.broadcasted_iota(jnp.int32, (n_sel, 1), 0)
    cur = (i * tq + lax.broadcasted_iota(jnp.int32, (1, tq), 1)) // NSA_SLC_BLOCK
    sel_valid = jb <= cur
    forced = sel_valid & ((jb == 0) | (jb >= cur - 1))
    score = jnp.where(forced, FORCE_SCORE, jnp.where(sel_valid, imp_t, NEG_INF))
    cnt = jnp.zeros((n_sel, tq), F32)
    for ii in range(n_sel):
        ri = score[ii:ii + 1, :]
        beats = (ri > score) | ((ri == score) & (jb > ii))
        cnt = cnt + jnp.where(beats, 1.0, 0.0)
    sel_t = jnp.where((cnt < n_top) & sel_valid, 1.0, 0.0)
    sel = jnp.transpose(jnp.concatenate([sel_t, jnp.zeros_like(sel_t)], axis=0)).astype(BF16)

    er = lax.broadcasted_iota(jnp.int32, (LANES, tk), 0)
    ec = lax.broadcasted_iota(jnp.int32, (LANES, tk), 1)

    def slc_score(_, j):
        expand = jnp.where(er == (j * tk + ec) // NSA_SLC_BLOCK, 1.0, 0.0).astype(BF16)
        ok = _dot(sel, expand) > 0.5
        return jnp.where(rep(ok), _dot_nt(q_rot, ks_ref[_key_rows(j, tk), :]), NEG_INF)

    o_slc, = _flash_sweep_ref(0, n_full, 1, slc_score,
                              lambda _, j: vs_ref[_key_rows(j, tk), :], causal, hq, tk, bufs,
                              rc=16)

    def win_score(_, j):
        col = j * tk + lax.broadcasted_iota(jnp.int32, (1, tk), 1)
        return jnp.where(rep(col > row - NSA_WINDOW),
                         _dot_nt(q_rot, kw_ref[_key_rows(j, tk), :]), NEG_INF)

    j_lo = (jnp.maximum(i * tq - NSA_WINDOW, 0) // (2 * tk)) * 2
    o_win, = _flash_sweep_ref(j_lo, n_full, 1, win_score,
                              lambda _, j: vw_ref[_key_rows(j, tk), :], causal, hq, tk, bufs,
                              rc=16)

    g = jax.nn.sigmoid(g_ref[...])
    for h in range(N_HEADS):
        rows = slice(h * tq, (h + 1) * tq)
        o_ref[:, h * HEAD_DIM:(h + 1) * HEAD_DIM] = (
            g[:, 3 * h:3 * h + 1] * o_cmp[rows] + g[:, 3 * h + 1:3 * h + 2] * o_slc[rows]
            + g[:, 3 * h + 2:3 * h + 3] * o_win[rows]).astype(BF16)


def _nsa_attention(proj, aux, kc, vc, tq=256, tk=256):
    b, s, _ = proj.shape
    nb = kc.shape[1]
    n_top = min(NSA_TOPK, s // NSA_SLC_BLOCK)
    kv = lambda col: pl.BlockSpec((None, s, HEAD_DIM), lambda bi, i: (bi, 0, col))
    return pl.pallas_call(
        functools.partial(_nsa_kernel, tq=tq, tk=tk, n_top=n_top),
        grid=(b, s // tq),
        in_specs=[pl.BlockSpec((None, tq, GROUP), lambda bi, i: (bi, i, COL_AQ // N_HEADS)),
                  pl.BlockSpec((None, tq, GROUP), lambda bi, i: (bi, i, COL_AQR // N_HEADS)),
                  pl.BlockSpec((None, tq, LANES), lambda bi, i: (bi, i, AUX_G)),
                  pl.BlockSpec((None, nb, HEAD_DIM), lambda bi, i: (bi, 0, 0)),
                  pl.BlockSpec((None, nb, HEAD_DIM), lambda bi, i: (bi, 0, 0)),
                  kv(COL_AKS), kv(COL_AKW), kv(COL_AVS), kv(COL_AVW)],
        out_specs=pl.BlockSpec((None, tq, GROUP), lambda bi, i: (bi, i, 0)),
        out_shape=jax.ShapeDtypeStruct((b, s, GROUP), BF16),
        scratch_shapes=_flash_scratch(1, N_HEADS * tq, tk),
        compiler_params=_params(2),
        name="nsa_attention",
    )(proj, proj, aux, kc, vc, proj, proj, proj, proj)


def _out_kernel(oa_ref, ob_ref, oc_ref, od_ref, w_ref, x_ref, g_ref, o_ref):
    acc = _dot(oa_ref[...], w_ref[0:GROUP, :])
    acc = acc + _dot(ob_ref[...], w_ref[GROUP:2 * GROUP, :])
    acc = acc + _dot(oc_ref[...], w_ref[2 * GROUP:3 * GROUP, :])
    acc = acc + _dot(od_ref[...], w_ref[3 * GROUP:4 * GROUP, :])
    o_ref[...] = x_ref[...] + _rms(acc, g_ref[...])


def _out_proj(oa, ob, oc, od, w, layer, x2d, g, tm=512):
    t, d = x2d.shape
    mix = pl.BlockSpec((tm, GROUP), lambda i: (i, 0))
    return pl.pallas_call(
        _out_kernel,
        grid=(t // tm,),
        in_specs=[mix, mix, mix, mix,
                  pl.BlockSpec((None, 4 * GROUP, d), lambda i: (layer, 0, 0)),
                  pl.BlockSpec((tm, d), lambda i: (i, 0)),
                  pl.BlockSpec((1, d), lambda i: (0, 0))],
        out_specs=pl.BlockSpec((tm, d), lambda i: (i, 0)),
        out_shape=jax.ShapeDtypeStruct((t, d), F32),
        compiler_params=_params(1),
        name="out_proj",
    )(oa, ob, oc, od, w, x2d, g)


HALO = 16


def _mlp_kernel(x_ref, xh_ref, gpre_ref, wg_ref, wu_ref, cwg_ref, cwu_ref, cbg_ref, cbu_ref,
                wd_ref, gpost_ref, o_ref, xn_ref, acc_ref, *, tm, tiles_per_seq):
    i = pl.program_id(0)
    f = pl.program_id(1)

    @pl.when(f == 0)
    def _():
        xn_ref[HALO:, :] = _rms(x_ref[...], gpre_ref[...]).astype(BF16)
        halo = _rms(xh_ref[...], gpre_ref[...])
        seq_start = (i % tiles_per_seq) == 0
        xn_ref[:HALO, :] = jnp.where(seq_start, 0.0, halo).astype(BF16)
        acc_ref[...] = jnp.zeros_like(acc_ref)

    xn = xn_ref[...]

    def conv(u, cw_ref, cb_ref):
        cw = cw_ref[...]
        out = cb_ref[...] + cw[0:1] * u[HALO - 2:HALO - 2 + tm]
        out = out + cw[1:2] * u[HALO - 1:HALO - 1 + tm]
        return out + cw[2:3] * u[HALO:HALO + tm]

    gate = conv(_dot(xn, wg_ref[...]), cwg_ref, cbg_ref)
    up = conv(_dot(xn, wu_ref[...]), cwu_ref, cbu_ref)
    hid = (jax.nn.gelu(gate, approximate=True) * up).astype(BF16)
    acc_ref[...] += _dot(hid, wd_ref[...])

    @pl.when(f == pl.num_programs(1) - 1)
    def _():
        o_ref[...] = x_ref[...] + _rms(acc_ref[...], gpost_ref[...])


def _mlp(x2d, gpre, w_up, conv_w, conv_b, w_down, layer, gpost, seq, tm=1024, tf=512):
    t, d = x2d.shape
    ff = w_down.shape[1]
    nf = ff // tf
    hb = tm // HALO
    once = pl.Buffered(1)
    return pl.pallas_call(
        functools.partial(_mlp_kernel, tm=tm, tiles_per_seq=seq // tm),
        grid=(t // tm, nf),
        in_specs=[pl.BlockSpec((tm, d), lambda i, f: (i, 0), pipeline_mode=once),
                  pl.BlockSpec((HALO, d), lambda i, f: (jnp.maximum(i * hb - 1, 0), 0)),
                  pl.BlockSpec((1, d), lambda i, f: (0, 0)),
                  pl.BlockSpec((None, d, tf), lambda i, f: (layer, 0, f)),
                  pl.BlockSpec((None, d, tf), lambda i, f: (layer, 0, nf + f)),
                  pl.BlockSpec((3, tf), lambda i, f: (0, f)),
                  pl.BlockSpec((3, tf), lambda i, f: (0, nf + f)),
                  pl.BlockSpec((1, tf), lambda i, f: (0, f)),
                  pl.BlockSpec((1, tf), lambda i, f: (0, nf + f)),
                  pl.BlockSpec((None, tf, d), lambda i, f: (layer, f, 0)),
                  pl.BlockSpec((1, d), lambda i, f: (0, 0))],
        out_specs=pl.BlockSpec((tm, d), lambda i, f: (i, 0), pipeline_mode=once),
        out_shape=jax.ShapeDtypeStruct((t, d), F32),
        scratch_shapes=[pltpu.VMEM((tm + HALO, d), BF16), pltpu.VMEM((tm, d), F32)],
        compiler_params=_params(2),
        name="conv_geglu_mlp",
    )(x2d, x2d, gpre, w_up, w_up, conv_w, conv_w, conv_b, conv_b, w_down, gpost)


_IN_OFF = np.concatenate([[0], np.cumsum(_IN_SPLITS)]).tolist()
_HEAD_RUNS = ((0, 0), (0, 0), (3, 6), (8, 13), (15, 17))


def _repack_kernel(w_ref, heads_ref, aux_ref):
    def run(first, last):
        return w_ref[:, _IN_OFF[first]:_IN_OFF[last + 1]].astype(BF16)

    at = 0
    for first, last in _HEAD_RUNS:
        width = _IN_OFF[last + 1] - _IN_OFF[first]
        heads_ref[:, at:at + width] = run(first, last)
        at += width
    aux_ref[...] = jnp.zeros(aux_ref.shape, BF16)
    aux_ref[:, 0:2 * LANES] = run(1, 2)
    aux_ref[:, AUX_G * LANES:AUX_G * LANES + _IN_SPLITS[7]] = run(7, 7)
    aux_ref[:, AUX_F * LANES:AUX_F * LANES + _IN_SPLITS[14]] = run(14, 14)


def _split_w_in(w, rows=256):
    w = jnp.pad(w, ((0, 0), (0, 0), (0, -w.shape[2] % LANES))).astype(BF16)
    depth, d, n_in = w.shape
    n_heads, n_aux = HEAD_BLOCKS * LANES, AUX_BLOCKS * LANES
    return pl.pallas_call(
        _repack_kernel,
        grid=(depth, d // rows),
        in_specs=[pl.BlockSpec((None, rows, n_in), lambda l, r: (l, r, 0))],
        out_specs=[pl.BlockSpec((None, rows, n_heads), lambda l, r: (l, r, 0)),
                   pl.BlockSpec((None, rows, n_aux), lambda l, r: (l, r, 0))],
        out_shape=[jax.ShapeDtypeStruct((depth, d, n_heads), BF16),
                   jax.ShapeDtypeStruct((depth, d, n_aux), BF16)],
        compiler_params=_params(2),
        name="repack_w_in",
    )(w)


def _rope_lane_tables(seq, rot_dim, period):
    pos = jnp.arange(seq, dtype=F32)
    inv = ROPE_THETA ** (-jnp.arange(0, rot_dim, 2, dtype=F32) / rot_dim)
    ang = pos[:, None] * inv[None, :]
    cos, sin = jnp.cos(ang), jnp.sin(ang)
    rest = period - rot_dim
    c = jnp.concatenate([cos, cos, jnp.ones((seq, rest), F32)], axis=-1)
    s = jnp.concatenate([-sin, sin, jnp.zeros((seq, rest), F32)], axis=-1)
    reps = LANES // period
    return jnp.tile(c, (1, reps)), jnp.tile(s, (1, reps))


@jax.jit
def _forward(x, norm_mix_pre, norm_mix_post, norm_mlp_pre, norm_mlp_post, w_in, w_out,
             nsa_pe_k, nsa_pe_v, nsa_cmp_k1, nsa_cmp_k2, nsa_cmp_v1, nsa_cmp_v2,
             fox_forget_bias, diff_lambda, diff_norm, mlp_w_up, mlp_conv_w, mlp_conv_b,
             mlp_w_down):
    b, s, d = x.shape
    depth = w_in.shape[0]
    cos_h, sin_h = _rope_lane_tables(s, HEAD_DIM // 4, LANES)
    cos_d, sin_d = _rope_lane_tables(s, DIFF_QK // 4, DIFF_QK)
    cos_sets = jnp.stack([cos_h * SCALE_HEAD, cos_h, cos_d * SCALE_DIFF, cos_d])
    sin_sets = jnp.stack([sin_h * SCALE_HEAD, sin_h, sin_d * SCALE_DIFF, sin_d])
    w_heads, w_aux = _split_w_in(w_in)
    w_out_b = w_out.astype(BF16)
    w_up_b = mlp_w_up.astype(BF16)
    w_down_b = mlp_w_down.astype(BF16)
    k1_b, k2_b = nsa_cmp_k1.astype(BF16), nsa_cmp_k2.astype(BF16)
    v1_b, v2_b = nsa_cmp_v1.astype(BF16), nsa_cmp_v2.astype(BF16)
    fox_bias = jnp.pad(fox_forget_bias, ((0, 0), (0, LANES - N_HEADS)))
    row2 = lambda a: a.reshape(1, -1)
    cmp_w = NSA_CMP_STRIDE * HEAD_DIM

    xt = x.reshape(b * s, d)
    for l in range(depth):
        g_pre = row2(norm_mix_pre[l])
        proj, aux = _head_proj(xt, g_pre, w_heads, w_aux, l, cos_sets, sin_sets, s,
                               tm=min(1024, s))
        proj = proj.reshape(b, s, HEAD_BLOCKS * LANES)
        aux = aux.reshape(b, s, AUX_BLOCKS * LANES)

        k2 = aux[:, :, AUX_KC * LANES:(AUX_KC + 1) * LANES].reshape(b, s // NSA_CMP_STRIDE, cmp_w)
        v2 = aux[:, :, AUX_VC * LANES:(AUX_VC + 1) * LANES].reshape(b, s // NSA_CMP_STRIDE, cmp_w)
        kc, vc = _nsa_compress(k2, v2, row2(nsa_pe_k[l]), row2(nsa_pe_v[l]),
                               k1_b[l], k2_b[l], v1_b[l], v2_b[l])
        o_a = _nsa_attention(proj, aux, kc, vc)

        o_b = _sb_attention(proj)

        cum_col, cum_row = _fox_cum(aux, fox_bias[l:l + 1])
        o_c = _fox_attention(proj, cum_col, cum_row)

        lam_init = 0.8 - 0.6 * math.exp(-0.3 * l)
        dl = jnp.concatenate([
            jnp.pad(diff_lambda[l], ((0, 0), (0, LANES - DIFF_QK))),
            jnp.full((1, LANES), lam_init, F32),
            jnp.full((1, LANES), 1.0 - lam_init, F32),
            jnp.zeros((2, LANES), F32)], axis=0)
        o_d = _diff_attention(proj, dl, row2(diff_norm[l]))

        flat = lambda o: o.reshape(b * s, GROUP)
        xt = _out_proj(flat(o_a), flat(o_b), flat(o_c), flat(o_d), w_out_b, l, xt,
                       row2(norm_mix_post[l]))
        xt = _mlp(xt, row2(norm_mlp_pre[l]), w_up_b, mlp_conv_w[l], row2(mlp_conv_b[l]),
                  w_down_b, l, row2(norm_mlp_post[l]), s, tm=min(1024, s))
    return xt.reshape(b, s, d)


def kernel(x, norm_mix_pre, norm_mix_post, norm_mlp_pre, norm_mlp_post, w_in, w_out, nsa_pe_k, nsa_pe_v, nsa_cmp_k1, nsa_cmp_k2, nsa_cmp_v1, nsa_cmp_v2, fox_forget_bias, diff_lambda, diff_norm, mlp_w_up, mlp_conv_w, mlp_conv_b, mlp_w_down):
    return _forward(x, norm_mix_pre, norm_mix_post, norm_mlp_pre, norm_mlp_post, w_in, w_out,
                    nsa_pe_k, nsa_pe_v, nsa_cmp_k1, nsa_cmp_k2, nsa_cmp_v1, nsa_cmp_v2,
                    fox_forget_bias, diff_lambda, diff_norm, mlp_w_up, mlp_conv_w, mlp_conv_b,
                    mlp_w_down)
```
